```python
import math
import jax, jax.numpy as jnp
from jax import lax
import numpy as np

D_MODEL = 2048
BATCH = 4
SEQ = 2048
DEPTH = 4
DEC_BATCH = 8
DEC_SEQ = 16
PAST_LEN = 4096

CHUNK = 64
QBLK = 128
EPS = 1e-6
MIX_DIM = D_MODEL
D_ATT = MIX_DIM // 2
D_SSM = MIX_DIM - D_ATT
H_A = 8
DH_V = D_ATT // H_A
DH_QK = DH_V // 2
ATT_QK = H_A * 2 * DH_QK
ATT_SCALE = DH_QK ** -0.5
SSM_P = 64
H_S = D_SSM // SSM_P
SSM_N = 128
SSM_G = 2
CONV_W = 4
CONV_DIM = D_SSM + 2 * SSM_G * SSM_N
IN_DIM = 2 * ATT_QK + D_ATT + D_SSM + CONV_DIM + H_S
SPLITS = (ATT_QK, 2 * ATT_QK, 2 * ATT_QK + D_ATT, 2 * ATT_QK + D_ATT + D_SSM, 2 * ATT_QK + D_ATT + D_SSM + CONV_DIM)
N_KEYS = 128
N_EXPERTS = N_KEYS * N_KEYS
PEER_HEADS = 8
PEER_TOPK = 16
PEER_DK = 256
PEER_BLOCK = 128
PLE_DIM = 256

kernel_name = 'diff_ssd_peer_stream_step'


def lambda_init(layer_idx):
    return 0.8 - 0.6 * math.exp(-0.3 * layer_idx)


def rmsnorm(x, g):
    xf = x.astype(jnp.float32)
    y = xf * lax.rsqrt(jnp.mean(xf * xf, axis=-1, keepdims=True) + EPS)
    return (y * g.astype(jnp.float32)).astype(x.dtype)


def causal_conv(u, prev, w, bias):
    L = u.shape[1]
    up = jnp.concatenate([prev.astype(u.dtype), u], axis=1)
    out = bias.astype(jnp.float32)
    for j in range(CONV_W):
        out = out + up[:, j:j + L].astype(jnp.float32) * w[j].astype(jnp.float32)
    return jax.nn.silu(out).astype(u.dtype), up[:, up.shape[1] - (CONV_W - 1):]


def ssd_scan(x, dt, A, Bm, Cm, h0, chunk):
    b, L, H, P = x.shape
    nc = L // chunk
    rep = H // Bm.shape[2]
    Bh = jnp.repeat(Bm.astype(jnp.float32), rep, axis=2).reshape(b, nc, chunk, H, SSM_N)
    Ch = jnp.repeat(Cm.astype(jnp.float32), rep, axis=2).reshape(b, nc, chunk, H, SSM_N)
    xdt = (x.astype(jnp.float32) * dt[..., None]).reshape(b, nc, chunk, H, P)
    a = (dt * A).reshape(b, nc, chunk, H).transpose(0, 3, 1, 2)
    a_cum = jnp.cumsum(a, axis=-1)
    tril = jnp.tril(jnp.ones((chunk, chunk), dtype=bool))
    Lm = jnp.exp(jnp.where(tril, a_cum[..., :, None] - a_cum[..., None, :], -jnp.inf))
    cb = jnp.einsum('bclhn,bcshn->bhcls', Ch, Bh) * Lm
    y_diag = jnp.einsum('bhcls,bcshp->bclhp', cb, xdt)
    decay_in = jnp.exp(a_cum[..., -1:] - a_cum)
    st = jnp.einsum('bclhn,bhcl,bclhp->bchpn', Bh, decay_in, xdt)
    chunk_decay = jnp.exp(a_cum[..., -1])

    def step(hc, inp):
        s_c, d_c = inp
        return d_c[..., None, None] * hc + s_c, hc

    h_fin, h_in = lax.scan(step, h0.astype(jnp.float32),
                           (st.transpose(1, 0, 2, 3, 4), chunk_decay.transpose(2, 0, 1)))
    y_off = jnp.einsum('bclhn,cbhpn,bhcl->bclhp', Ch, h_in, jnp.exp(a_cum))
    return (y_diag + y_off).reshape(b, L, H, P), h_fin


def diff_core(q, k, v, mask, lam):
    s = jnp.einsum('bqhcd,bkhcd->bhcqk', q.astype(jnp.float32), k.astype(jnp.float32)) * ATT_SCALE
    if mask is not None:
        s = jnp.where(mask, s, -jnp.inf)
    pr = jax.nn.softmax(s, axis=-1)
    w = pr[:, :, 0] - lam * pr[:, :, 1]
    return jnp.einsum('bhqk,bkhd->bqhd', w, v.astype(jnp.float32))


def diff_attn_blocked(q, k, v, lam):
    b, L = q.shape[:2]
    nb = L // QBLK
    qb = q.reshape(b, nb, QBLK, H_A, 2, DH_QK).transpose(1, 0, 2, 3, 4, 5)
    key_chunk = jnp.arange(L) // CHUNK

    def one_block(args):
        qi, i = args
        q_chunk = (i * QBLK + jnp.arange(QBLK)) // CHUNK
        mask = key_chunk[None, :] <= q_chunk[:, None]
        return diff_core(qi, k, v, mask, lam)

    o = lax.map(one_block, (qb, jnp.arange(nb)))
    return o.transpose(1, 0, 2, 3, 4).reshape(b, L, H_A, DH_V)


def peer(c, wq, sk1, sk2, U, V):
    b, L, D = c.shape
    T = b * L
    xt = c.reshape(T, D)
    q = (xt @ wq).astype(jnp.float32).reshape(T, PEER_HEADS, 2, PEER_DK // 2)
    s1 = jnp.einsum('thd,hkd->thk', q[:, :, 0], sk1.astype(jnp.float32))
    s2 = jnp.einsum('thd,hkd->thk', q[:, :, 1], sk2.astype(jnp.float32))
    v1, i1 = lax.top_k(s1, PEER_TOPK)
    v2, i2 = lax.top_k(s2, PEER_TOPK)
    cand = (v1[..., :, None] + v2[..., None, :]).reshape(T, PEER_HEADS, PEER_TOPK * PEER_TOPK)
    sc, ci = lax.top_k(cand, PEER_TOPK)
    e = (jnp.take_along_axis(i1, ci // PEER_TOPK, axis=-1) * N_KEYS
         + jnp.take_along_axis(i2, ci % PEER_TOPK, axis=-1))
    g = jax.nn.softmax(sc, axis=-1)
    blk = min(PEER_BLOCK, T)
    nb = -(-T // blk)
    pad = nb * blk - T
    xb = jnp.pad(xt, ((0, pad), (0, 0))).reshape(nb, blk, D)
    eb = jnp.pad(e, ((0, pad), (0, 0), (0, 0))).reshape(nb, blk, PEER_HEADS, PEER_TOPK)
    gb = jnp.pad(g, ((0, pad), (0, 0), (0, 0))).reshape(nb, blk, PEER_HEADS, PEER_TOPK)

    def run(args):
        xi, ei, gi = args
        act = jax.nn.gelu(jnp.einsum('thkd,td->thk', U[ei].astype(jnp.float32), xi.astype(jnp.float32)),
                          approximate=False)
        return jnp.einsum('thk,thkd->td', gi * act, V[ei].astype(jnp.float32))

    out = lax.map(run, (xb, eb, gb)).reshape(nb * blk, D)[:T]
    return out.reshape(b, L, D)


def trunk_layer(h, p_l, layer_idx, conv_prev, ssm_prev, k_past, v_past,
                g_mix, w_in, conv_w, conv_b, dt_bias, a_log, d_skip, g_ssd,
                q_gain, k_gain, lam_q1, lam_k1, lam_q2, lam_k2, g_sub, w_out,
                g_ffn, peer_wq, peer_k1, peer_k2, peer_u, peer_v, g_ple, w_ple, w_pgate):
    b, L, _ = h.shape
    dtype = h.dtype
    first = k_past is None
    a = rmsnorm(h, g_mix)
    q, k, v, z, xbc, dt_raw = jnp.split(a @ w_in, SPLITS, axis=-1)

    q = rmsnorm(q.reshape(b, L, H_A, 2, DH_QK), q_gain)
    k = rmsnorm(k.reshape(b, L, H_A, 2, DH_QK), k_gain).astype(dtype)
    v = v.reshape(b, L, H_A, DH_V).astype(dtype)
    lam0 = lambda_init(layer_idx)
    lam = (jnp.exp(jnp.sum(lam_q1.astype(jnp.float32) * lam_k1.astype(jnp.float32)))
           - jnp.exp(jnp.sum(lam_q2.astype(jnp.float32) * lam_k2.astype(jnp.float32))) + lam0)
    if first:
        o = diff_attn_blocked(q, k, v, lam)
    else:
        k_all = jnp.concatenate([k_past.astype(dtype), k], axis=1)
        v_all = jnp.concatenate([v_past.astype(dtype), v], axis=1)
        o = diff_core(q, k_all, v_all, None, lam)
    o = (rmsnorm(o, g_sub) * (1.0 - lam0)).reshape(b, L, D_ATT)

    if first:
        conv_prev = jnp.zeros((b, CONV_W - 1, CONV_DIM), dtype)
        ssm_prev = jnp.zeros((b, H_S, SSM_P, SSM_N), jnp.float32)
    xbc, conv_new = causal_conv(xbc, conv_prev, conv_w, conv_b)
    xs, Bm, Cm = jnp.split(xbc, (D_SSM, D_SSM + SSM_G * SSM_N), axis=-1)
    xs = xs.reshape(b, L, H_S, SSM_P)
    dt = jax.nn.softplus(dt_raw.astype(jnp.float32) + dt_bias.astype(jnp.float32))
    A = -jnp.exp(a_log.astype(jnp.float32))
    y, ssm_new = ssd_scan(xs, dt, A, Bm.reshape(b, L, SSM_G, SSM_N), Cm.reshape(b, L, SSM_G, SSM_N),
                          ssm_prev, CHUNK if first else L)
    y = y + d_skip.astype(jnp.float32)[:, None] * xs.astype(jnp.float32)
    y = y.reshape(b, L, D_SSM) * jax.nn.silu(z.astype(jnp.float32))
    y = rmsnorm(y.reshape(b, L, SSM_G, D_SSM // SSM_G), g_ssd.reshape(SSM_G, D_SSM // SSM_G)).reshape(b, L, D_SSM)

    h = h + (jnp.concatenate([y, o], axis=-1).astype(dtype) @ w_out).astype(dtype)
    h = h + peer(rmsnorm(h, g_ffn), peer_wq, peer_k1, peer_k2, peer_u, peer_v).astype(dtype)
    gate = jax.nn.sigmoid((rmsnorm(h, g_ple) @ w_pgate).astype(jnp.float32))
    h = h + ((p_l @ w_ple).astype(jnp.float32) * gate).astype(dtype)
    return h, k, v, ssm_new.astype(dtype), conv_new


def setup_inputs(seed: int = 0) -> dict:
    key = jax.random.key(seed)
    ks = iter(jax.random.split(key, 40))
    f32 = jnp.float32

    def nrm(shape, scale):
        return scale * jax.random.normal(next(ks), shape, f32)

    def gain(shape):
        return 1.0 + nrm(shape, 0.02)

    dt0 = jnp.exp(jax.random.uniform(next(ks), (DEPTH, H_S), f32, math.log(1e-3), math.log(1e-1)))
    a0 = jax.random.uniform(next(ks), (DEPTH, H_S), f32, 1.0, 16.0)
    return {
        'x_prompt': nrm((BATCH, SEQ, D_MODEL), 1.0),
        'x_sample': nrm((DEC_BATCH, DEC_SEQ, D_MODEL), 1.0),
        'cache_k': nrm((DEPTH, DEC_BATCH, PAST_LEN, H_A, 2, DH_QK), 1.0),
        'cache_v': nrm((DEPTH, DEC_BATCH, PAST_LEN, H_A, DH_V), 1.0),
        'state_ssm': nrm((DEPTH, DEC_BATCH, H_S, SSM_P, SSM_N), 0.3),
        'state_conv': nrm((DEPTH, DEC_BATCH, CONV_W - 1, CONV_DIM), 1.0),
        'p_prompt': nrm((DEPTH, BATCH, SEQ, PLE_DIM), 1.0),
        'p_sample': nrm((DEPTH, DEC_BATCH, DEC_SEQ, PLE_DIM), 1.0),
        'g_mix': gain((DEPTH, D_MODEL)),
        'w_in': nrm((DEPTH, D_MODEL, IN_DIM), D_MODEL ** -0.5),
        'conv_w': nrm((DEPTH, CONV_W, CONV_DIM), CONV_W ** -0.5),
        'conv_b': nrm((DEPTH, CONV_DIM), 0.02),
        'dt_bias': dt0 + jnp.log(-jnp.expm1(-dt0)),
        'a_log': jnp.log(a0),
        'd_skip': 1.0 + nrm((DEPTH, H_S), 0.1),
        'g_ssd': gain((DEPTH, D_SSM)),
        'q_gain': gain((DEPTH, 2, DH_QK)),
        'k_gain': gain((DEPTH, 2, DH_QK)),
        'lam_q1': nrm((DEPTH, DH_QK), 0.1),
        'lam_k1': nrm((DEPTH, DH_QK), 0.1),
        'lam_q2': nrm((DEPTH, DH_QK), 0.1),
        'lam_k2': nrm((DEPTH, DH_QK), 0.1),
        'g_sub': gain((DEPTH, DH_V)),
        'w_out': nrm((DEPTH, MIX_DIM, D_MODEL), MIX_DIM ** -0.5),
        'g_ffn': gain((DEPTH, D_MODEL)),
        'peer_wq': nrm((DEPTH, D_MODEL, PEER_HEADS * PEER_DK), D_MODEL ** -0.5),
        'peer_k1': nrm((DEPTH, PEER_HEADS, N_KEYS, PEER_DK // 2), (PEER_DK // 2) ** -0.5),
        'peer_k2': nrm((DEPTH, PEER_HEADS, N_KEYS, PEER_DK // 2), (PEER_DK // 2) ** -0.5),
        'peer_u': nrm((DEPTH, N_EXPERTS, D_MODEL), D_MODEL ** -0.5),
        'peer_v': nrm((DEPTH, N_EXPERTS, D_MODEL), 0.3),
        'g_ple': gain((DEPTH, D_MODEL)),
        'w_ple': nrm((DEPTH, PLE_DIM, D_MODEL), PLE_DIM ** -0.5),
        'w_pgate': nrm((DEPTH, D_MODEL, D_MODEL), D_MODEL ** -0.5),
    }


def reference(x_prompt, x_sample, cache_k, cache_v, state_ssm, state_conv, p_prompt, p_sample,
              g_mix, w_in, conv_w, conv_b, dt_bias, a_log, d_skip, g_ssd,
              q_gain, k_gain, lam_q1, lam_k1, lam_q2, lam_k2, g_sub, w_out,
              g_ffn, peer_wq, peer_k1, peer_k2, peer_u, peer_v, g_ple, w_ple, w_pgate):
    hp, hs = x_prompt, x_sample
    kp_l, vp_l, sp_l, cp_l = [], [], [], []
    ks_l, vs_l, ss_l, cs_l = [], [], [], []
    for l in range(DEPTH):
        w = (g_mix[l], w_in[l], conv_w[l], conv_b[l], dt_bias[l], a_log[l], d_skip[l], g_ssd[l],
             q_gain[l], k_gain[l], lam_q1[l], lam_k1[l], lam_q2[l], lam_k2[l], g_sub[l], w_out[l],
             g_ffn[l], peer_wq[l], peer_k1[l], peer_k2[l], peer_u[l], peer_v[l], g_ple[l], w_ple[l], w_pgate[l])
        hp, kp, vp, sp, cp = trunk_layer(hp, p_prompt[l], l, None, None, None, None, *w)
        hs, kq, vq, sq, cq = trunk_layer(hs, p_sample[l], l, state_conv[l], state_ssm[l], cache_k[l], cache_v[l], *w)
        kp_l.append(kp); vp_l.append(vp); sp_l.append(sp); cp_l.append(cp)
        ks_l.append(kq); vs_l.append(vq); ss_l.append(sq); cs_l.append(cq)
    return (hp, hs,
            jnp.stack(kp_l), jnp.stack(vp_l), jnp.stack(sp_l), jnp.stack(cp_l),
            jnp.stack(ks_l), jnp.stack(vs_l), jnp.stack(ss_l), jnp.stack(cs_l))
```

```python
import functools
import math

import jax
import jax.numpy as jnp
from jax import lax
from jax.experimental import pallas as pl
from jax.experimental.pallas import tpu as pltpu

F32 = jnp.float32
BF16 = jnp.bfloat16

EPS = 1e-6
CHUNK = 64
CHUNK_SHIFT = 6
H_A = 8
DH_QK = 64
DH_V = 128
D_ATT = H_A * DH_V
ATT_SCALE = DH_QK ** -0.5
SSM_P = 64
H_S = 16
SSM_N = 128
SSM_G = 2
D_SSM = H_S * SSM_P
CONV_W = 4
CONV_DIM = D_SSM + 2 * SSM_G * SSM_N
MAIN_DIM = 2 * D_ATT + D_ATT + D_SSM + CONV_DIM
N_KEYS = 128
PEER_HEADS = 8
PEER_TOPK = 16
TOPK_SHIFT = 4

LANE = 128
TM = 512
SSD_Q = 128
VMEM_LIMIT = 56 * 1024 * 1024

_NT = (((1,), (1,)), ((), ()))
_HI = lax.Precision.HIGHEST


def _cparams(sem):
    return pltpu.CompilerParams(dimension_semantics=sem, vmem_limit_bytes=VMEM_LIMIT)


def _rms(x, g):
    ms = jnp.mean(x * x, axis=-1, keepdims=True)
    return x * lax.rsqrt(ms + EPS) * g


def _norm_proj_kernel(x_ref, g_ref, w_ref, *rest, with_dt, with_xn):
    rest = list(rest)
    wdt_ref = rest.pop(0) if with_dt else None
    o_ref = rest.pop(0)
    odt_ref = rest.pop(0) if with_dt else None
    oxn_ref = rest.pop(0) if with_xn else None
    xn_ref = rest.pop(0)

    @pl.when(pl.program_id(1) == 0)
    def _():
        xn = _rms(x_ref[...], g_ref[...]).astype(BF16)
        xn_ref[...] = xn
        if with_xn:
            oxn_ref[...] = xn
        if with_dt:
            odt_ref[...] = jnp.dot(xn, wdt_ref[...], preferred_element_type=F32)

    o_ref[...] = jnp.dot(xn_ref[...], w_ref[...], preferred_element_type=F32)


def _norm_proj(x, g, w, wdt=None, with_xn=False, tn=512):
    t, d = x.shape
    n = w.shape[1]
    with_dt = wdt is not None
    in_specs = [
        pl.BlockSpec((TM, d), lambda i, j: (i, 0)),
        pl.BlockSpec((1, d), lambda i, j: (0, 0)),
        pl.BlockSpec((d, tn), lambda i, j: (0, j)),
    ]
    args = [x, g.reshape(1, d), w]
    out_shape = [jax.ShapeDtypeStruct((t, n), F32)]
    out_specs = [pl.BlockSpec((TM, tn), lambda i, j: (i, j))]
    if with_dt:
        in_specs.append(pl.BlockSpec((d, LANE), lambda i, j: (0, 0)))
        args.append(wdt)
        out_shape.append(jax.ShapeDtypeStruct((t, LANE), F32))
        out_specs.append(pl.BlockSpec((TM, LANE), lambda i, j: (i, 0)))
    if with_xn:
        out_shape.append(jax.ShapeDtypeStruct((t, d), BF16))
        out_specs.append(pl.BlockSpec((TM, d), lambda i, j: (i, 0)))
    return pl.pallas_call(
        functools.partial(_norm_proj_kernel, with_dt=with_dt, with_xn=with_xn),
        grid=(t // TM, n // tn),
        in_specs=in_specs,
        out_specs=out_specs,
        out_shape=out_shape,
        scratch_shapes=[pltpu.VMEM((TM, d), BF16)],
        compiler_params=_cparams(("parallel", "arbitrary")),
        name="norm_proj",
    )(*args)


def _qk_norm_kernel(q_ref, k_ref, v_ref, qg_ref, kg_ref, qn_ref, kf_ref, kb_ref, vb_ref):
    lane = lax.broadcasted_iota(jnp.int32, (1, DH_V), 1)
    lo = lane < DH_QK

    def norm(x, g):
        xx = x * x
        s_lo = jnp.sum(jnp.where(lo, xx, 0.0), axis=-1, keepdims=True)
        s_all = jnp.sum(xx, axis=-1, keepdims=True)
        ms = jnp.where(lo, s_lo, s_all - s_lo) * (1.0 / DH_QK)
        return x * lax.rsqrt(ms + EPS) * g

    for h in range(H_A):
        sl = slice(h * DH_V, (h + 1) * DH_V)
        qn_ref[:, sl] = (norm(q_ref[:, sl], qg_ref[...]) * ATT_SCALE).astype(BF16)
        kn = norm(k_ref[:, sl], kg_ref[...])
        kf_ref[:, sl] = kn
        kb_ref[:, sl] = kn.astype(BF16)
    vb_ref[...] = v_ref[...].astype(BF16)


def _qk_norm(proj, q_gain, k_gain):
    t = proj.shape[0]
    blk = lambda c: pl.BlockSpec((TM, D_ATT), lambda i, c=c: (i, c))
    gspec = pl.BlockSpec((1, DH_V), lambda i: (0, 0))
    return pl.pallas_call(
        _qk_norm_kernel,
        grid=(t // TM,),
        in_specs=[blk(0), blk(1), blk(2), gspec, gspec],
        out_specs=[blk(0)] * 4,
        out_shape=[
            jax.ShapeDtypeStruct((t, D_ATT), BF16),
            jax.ShapeDtypeStruct((t, D_ATT), F32),
            jax.ShapeDtypeStruct((t, D_ATT), BF16),
            jax.ShapeDtypeStruct((t, D_ATT), BF16),
        ],
        compiler_params=_cparams(("parallel",)),
        name="qk_norm",
    )(proj, proj, proj, q_gain.reshape(1, DH_V), k_gain.reshape(1, DH_V))


def _sub_norm(o, lam_ref, gsub):
    return _rms(o, gsub) * lam_ref[1]


def _attn_prompt_kernel(lam_ref, q_ref, k_ref, v_ref, gsub_ref, o_ref, *, tq):
    qi = pl.program_id(2)
    q = q_ref[...]
    lane = lax.broadcasted_iota(jnp.int32, (1, DH_V), 1)
    zero = jnp.zeros((), BF16)
    q0 = jnp.where(lane < DH_QK, q, zero)
    q1 = jnp.where(lane >= DH_QK, q, zero)

    def scores(kb):
        s0 = lax.dot_general(q0, kb, _NT, preferred_element_type=F32)
        s1 = lax.dot_general(q1, kb, _NT, preferred_element_type=F32)
        return s0, s1

    start = pl.multiple_of(qi * tq, tq)
    kb = k_ref[pl.ds(start, tq), :]
    vb = v_ref[pl.ds(start, tq), :]
    row = jnp.right_shift(lax.broadcasted_iota(jnp.int32, (tq, tq), 0), CHUNK_SHIFT)
    col = jnp.right_shift(lax.broadcasted_iota(jnp.int32, (tq, tq), 1), CHUNK_SHIFT)
    mask = col <= row
    s0, s1 = scores(kb)
    s0 = jnp.where(mask, s0, -jnp.inf)
    s1 = jnp.where(mask, s1, -jnp.inf)
    m0 = jnp.max(s0, axis=-1, keepdims=True)
    m1 = jnp.max(s1, axis=-1, keepdims=True)
    p0 = jnp.exp(s0 - m0)
    p1 = jnp.exp(s1 - m1)
    l0 = jnp.sum(p0, axis=-1, keepdims=True)
    l1 = jnp.sum(p1, axis=-1, keepdims=True)
    a0 = jnp.dot(p0.astype(BF16), vb, preferred_element_type=F32)
    a1 = jnp.dot(p1.astype(BF16), vb, preferred_element_type=F32)

    def body(j, carry):
        m0, l0, a0, m1, l1, a1 = carry
        st = pl.multiple_of(j * tq, tq)
        kb = k_ref[pl.ds(st, tq), :]
        vb = v_ref[pl.ds(st, tq), :]
        s0, s1 = scores(kb)

        def upd(s, m, l, a):
            mn = jnp.maximum(m, jnp.max(s, axis=-1, keepdims=True))
            alpha = jnp.exp(m - mn)
            p = jnp.exp(s - mn)
            l = alpha * l + jnp.sum(p, axis=-1, keepdims=True)
            a = alpha * a + jnp.dot(p.astype(BF16), vb, preferred_element_type=F32)
            return mn, l, a

        m0, l0, a0 = upd(s0, m0, l0, a0)
        m1, l1, a1 = upd(s1, m1, l1, a1)
        return m0, l0, a0, m1, l1, a1

    m0, l0, a0, m1, l1, a1 = lax.fori_loop(0, qi, body, (m0, l0, a0, m1, l1, a1))
    o = a0 / l0 - lam_ref[0] * (a1 / l1)
    o_ref[...] = _sub_norm(o, lam_ref, gsub_ref[...]).astype(BF16)


def _attn_prompt(lam, qn, kb, vb, g_sub, nb, seq):
    tq = min(512, seq)
    nq = seq // tq
    return pl.pallas_call(
        functools.partial(_attn_prompt_kernel, tq=tq),
        grid=(nb, H_A, nq),
        in_specs=[
            pl.BlockSpec(memory_space=pltpu.SMEM),
            pl.BlockSpec((tq, DH_V), lambda b, h, i: (b * nq + i, h)),
            pl.BlockSpec((seq, DH_V), lambda b, h, i: (b, h)),
            pl.BlockSpec((seq, DH_V), lambda b, h, i: (b, h)),
            pl.BlockSpec((1, DH_V), lambda b, h, i: (0, 0)),
        ],
        out_specs=pl.BlockSpec((tq, DH_V), lambda b, h, i: (b * nq + i, h)),
        out_shape=jax.ShapeDtypeStruct((nb * seq, D_ATT), BF16),
        compiler_params=_cparams(("parallel", "parallel", "arbitrary")),
        name="attn_prompt",
    )(lam, qn, kb, vb, g_sub.reshape(1, DH_V))


def _attn_sample_kernel(lam_ref, q_ref, kn_ref, vn_ref, ck_ref, cv_ref, gsub_ref, o_ref):
    q = q_ref[...]
    nq = q.shape[0]
    lane = lax.broadcasted_iota(jnp.int32, (1, DH_V), 1)
    zero = jnp.zeros((), BF16)
    q01 = jnp.concatenate([jnp.where(lane < DH_QK, q, zero), jnp.where(lane >= DH_QK, q, zero)], axis=0)
    sc = lax.dot_general(q01, ck_ref[...].astype(BF16), _NT, preferred_element_type=F32)
    sn = lax.dot_general(q01, kn_ref[...], _NT, preferred_element_type=F32)
    m = jnp.maximum(jnp.max(sc, axis=-1, keepdims=True), jnp.max(sn, axis=-1, keepdims=True))
    pc = jnp.exp(sc - m)
    pn = jnp.exp(sn - m)
    inv = 1.0 / (jnp.sum(pc, axis=-1, keepdims=True) + jnp.sum(pn, axis=-1, keepdims=True))
    pc = pc * inv
    pn = pn * inv
    lam = lam_ref[0]
    wc = (pc[:nq] - lam * pc[nq:]).astype(BF16)
    wn = (pn[:nq] - lam * pn[nq:]).astype(BF16)
    o = jnp.dot(wc, cv_ref[...].astype(BF16), preferred_element_type=F32)
    o = o + jnp.dot(wn, vn_ref[...], preferred_element_type=F32)
    o_ref[...] = _sub_norm(o, lam_ref, gsub_ref[...]).astype(BF16)


def _attn_sample(lam, qn, kb, vb, cache_k, cache_v, layer, g_sub, row0, nb, nq):
    past = cache_k.shape[2]
    r0 = row0 // nq
    new = lambda: pl.BlockSpec((nq, DH_V), lambda b, h: (r0 + b, h))
    cache = lambda: pl.BlockSpec((None, None, past, DH_V), lambda b, h: (layer, b, 0, h))
    return pl.pallas_call(
        _attn_sample_kernel,
        grid=(nb, H_A),
        in_specs=[pl.BlockSpec(memory_space=pltpu.SMEM), new(), new(), new(), cache(), cache(),
                  pl.BlockSpec((1, DH_V), lambda b, h: (0, 0))],
        out_specs=pl.BlockSpec((nq, DH_V), lambda b, h: (b, h)),
        out_shape=jax.ShapeDtypeStruct((nb * nq, D_ATT), BF16),
        compiler_params=_cparams(("parallel", "parallel")),
        name="attn_sample",
    )(lam, qn, kb, vb, cache_k, cache_v, g_sub.reshape(1, DH_V))


def _ssd_kernel(xs_ref, b_ref, c_ref, z_ref, dt_ref, prev_ref, h0_ref, cw_ref, cb_ref, dtb_ref,
                aneg_ref, dexp_ref, gssd_ref, expand_ref, y_ref, hout_ref, ext_ref, ht_ref, *, valid):
    q = SSD_Q
    c = pl.program_id(1)
    halo = 8

    @pl.when(c == 0)
    def _():
        ext_ref[0:halo, :] = prev_ref[...]
        ht_ref[...] = h0_ref[...].T

    ext_ref[halo:halo + q, 0:D_SSM] = xs_ref[...]
    ext_ref[halo:halo + q, D_SSM:D_SSM + SSM_G * SSM_N] = b_ref[...]
    ext_ref[halo:halo + q, D_SSM + SSM_G * SSM_N:CONV_DIM] = c_ref[...]
    conv = cb_ref[...]
    for j in range(CONV_W):
        r0 = halo - (CONV_W - 1) + j
        conv = conv + ext_ref[r0:r0 + q, :] * cw_ref[j:j + 1, :]
    tail = ext_ref[q:q + halo, :]
    ext_ref[0:halo, :] = tail
    xbc = conv * jax.nn.sigmoid(conv)
    xs = xbc[:, 0:D_SSM]
    bm = xbc[:, D_SSM:D_SSM + SSM_G * SSM_N]
    cm = xbc[:, D_SSM + SSM_G * SSM_N:CONV_DIM]

    dt = jax.nn.softplus(dt_ref[...] + dtb_ref[...])
    if valid < q:
        rowv = lax.broadcasted_iota(jnp.int32, (q, LANE), 0) < valid
        dt = jnp.where(rowv, dt, 0.0)
    a = dt * aneg_ref[...]
    r_i = lax.broadcasted_iota(jnp.int32, (q, q), 0)
    c_i = lax.broadcasted_iota(jnp.int32, (q, q), 1)
    tril = c_i <= r_i
    acum = jnp.dot(tril.astype(F32), a, precision=_HI, preferred_element_type=F32)
    eye = (r_i == c_i).astype(F32)
    acum_t = lax.dot_general(eye, acum, _NT, precision=_HI, preferred_element_type=F32)
    a_last = acum[q - 1:q, :]
    decay_in = jnp.exp(a_last - acum)
    stack = jnp.concatenate(
        [dt, dt * decay_in, jnp.exp(acum), jnp.broadcast_to(jnp.exp(a_last), (8, LANE))], axis=0)
    ex = jnp.dot(stack, expand_ref[...], precision=_HI, preferred_element_type=F32)
    dt_e = ex[0:q]
    dtd_e = ex[q:2 * q]
    eacum_e = ex[2 * q:3 * q]
    cd_e = ex[3 * q:3 * q + 1]

    xdt = (xs * dt_e).astype(BF16)
    xdtd = (xs * dtd_e).astype(BF16)
    lane = lax.broadcasted_iota(jnp.int32, (1, LANE), 1)
    zero = jnp.zeros((), BF16)
    gw = D_SSM // SSM_G
    hpg = H_S // SSM_G
    y_diag_parts = []
    y_off_parts = []
    for g in range(SSM_G):
        gs = slice(g * gw, (g + 1) * gw)
        bg = bm[:, g * SSM_N:(g + 1) * SSM_N]
        cg = cm[:, g * SSM_N:(g + 1) * SSM_N].astype(BF16)
        cb = lax.dot_general(cg, bg.astype(BF16), _NT, preferred_element_type=F32)
        ht_prev = ht_ref[:, gs]
        y_off = jnp.dot(cg, ht_prev.astype(BF16), preferred_element_type=F32)
        ht_ref[:, gs] = cd_e[:, gs] * ht_prev + jnp.dot(
            bg.T.astype(BF16), xdtd[:, gs], preferred_element_type=F32)
        for k in range(hpg // 2):
            ms = []
            for h in (g * hpg + 2 * k, g * hpg + 2 * k + 1):
                seg = acum[:, h:h + 1] - acum_t[h:h + 1, :]
                ms.append((cb * jnp.exp(jnp.where(tril, seg, -jnp.inf))).astype(BF16))
            pair = slice((g * hpg + 2 * k) * SSM_P, (g * hpg + 2 * k + 2) * SSM_P)
            xp = xdt[:, pair]
            rhs = jnp.concatenate([jnp.where(lane < SSM_P, xp, zero), jnp.where(lane >= SSM_P, xp, zero)], axis=0)
            y_diag_parts.append(jnp.dot(jnp.concatenate(ms, axis=1), rhs, preferred_element_type=F32))
        y_off_parts.append(y_off)
    y_diag = jnp.concatenate(y_diag_parts, axis=1)
    y_off = jnp.concatenate(y_off_parts, axis=1)
    y = y_diag + y_off * eacum_e + dexp_ref[...] * xs
    zz = z_ref[...]
    y = y * (zz * jax.nn.sigmoid(zz))
    for g in range(SSM_G):
        gs = slice(g * gw, (g + 1) * gw)
        y_ref[:, gs] = _rms(y[:, gs], gssd_ref[:, gs]).astype(BF16)

    @pl.when(c == pl.num_programs(1) - 1)
    def _():
        hout_ref[...] = ht_ref[...].T


def _ssd(proj, dtp, prev, h0, lw, nb, seq, valid):
    nc = seq // SSD_Q
    row = lambda w, col: pl.BlockSpec((SSD_Q, w), lambda b, c, col=col: (b * nc + c, col))
    const = lambda shp: pl.BlockSpec(shp, lambda b, c: (0,) * len(shp))
    per_b = lambda shp: pl.BlockSpec((None,) + shp, lambda b, c: (b, 0, 0))
    return pl.pallas_call(
        functools.partial(_ssd_kernel, valid=valid),
        grid=(nb, nc),
        in_specs=[
            row(D_SSM, 4), row(SSM_G * SSM_N, 20), row(SSM_G * SSM_N, 21), row(D_SSM, 3), row(LANE, 0),
            per_b((8, CONV_DIM)), per_b((D_SSM, SSM_N)),
            const((CONV_W, CONV_DIM)), const((1, CONV_DIM)), const((1, LANE)), const((1, LANE)),
            const((1, D_SSM)), const((1, D_SSM)), const((LANE, D_SSM)),
        ],
        out_specs=[row(D_SSM, 0), per_b((D_SSM, SSM_N))],
        out_shape=[jax.ShapeDtypeStruct((nb * seq, D_SSM), BF16),
                   jax.ShapeDtypeStruct((nb, D_SSM, SSM_N), F32)],
        scratch_shapes=[pltpu.VMEM((SSD_Q + 8, CONV_DIM), F32), pltpu.VMEM((SSM_N, D_SSM), F32)],
        compiler_params=_cparams(("parallel", "arbitrary")),
        name="ssd",
    )(proj, proj, proj, proj, dtp, prev, h0, lw["conv_w"], lw["conv_b"], lw["dt_bias"], lw["a_neg"],
      lw["d_exp"], lw["g_ssd"], lw["expand"])


def _out_proj_kernel(y_ref, o_ref, w_ref, h_ref, out_ref):
    acc = jnp.dot(y_ref[...], w_ref[0:D_SSM, :], preferred_element_type=F32)
    acc = acc + jnp.dot(o_ref[...], w_ref[D_SSM:D_SSM + D_ATT, :], preferred_element_type=F32)
    out_ref[...] = h_ref[...] + acc


def _out_proj(y, o, w, h, tn=512):
    t, d = h.shape
    return pl.pallas_call(
        _out_proj_kernel,
        grid=(t // TM, d // tn),
        in_specs=[
            pl.BlockSpec((TM, D_SSM), lambda i, j: (i, 0)),
            pl.BlockSpec((TM, D_ATT), lambda i, j: (i, 0)),
            pl.BlockSpec((D_SSM + D_ATT, tn), lambda i, j: (0, j)),
            pl.BlockSpec((TM, tn), lambda i, j: (i, j)),
        ],
        out_specs=pl.BlockSpec((TM, tn), lambda i, j: (i, j)),
        out_shape=jax.ShapeDtypeStruct((t, d), F32),
        compiler_params=_cparams(("parallel", "parallel")),
        name="out_proj",
    )(y, o, w, h)


def _top16(s, n_out=None):
    n, t = s.shape
    iota = lax.broadcasted_iota(jnp.int32, (n, t), 0)
    row16 = lax.broadcasted_iota(jnp.int32, (PEER_TOPK, t), 0)
    rank = jnp.full((n, t), PEER_TOPK, jnp.int32)
    vals = jnp.zeros((PEER_TOPK, t), F32)
    cnt = jnp.zeros((n_out, t), jnp.int32) if n_out else None
    for r in range(PEER_TOPK):
        m = jnp.max(s, axis=0, keepdims=True)
        idx = jnp.min(jnp.where(s == m, iota, n), axis=0, keepdims=True)
        sel = iota == idx
        rank = jnp.where(sel, r, rank)
        s = jnp.where(sel, -jnp.inf, s)
        vals = jnp.where(row16 == r, m, vals)
        if n_out:
            cnt = cnt + (row16 == jnp.right_shift(idx, TOPK_SHIFT)).astype(jnp.int32)
    return vals, rank, cnt


def _route_kernel(q_ref, k1_ref, k2_ref, e1_ref, n1_ref, e2_ref, r2_ref, *, tm):
    for lt in range(tm // LANE):
        ls = slice(lt * LANE, (lt + 1) * LANE)
        qb = q_ref[ls, :].astype(BF16)
        s1 = lax.dot_general(k1_ref[...], qb[:, 0:N_KEYS], _NT, preferred_element_type=F32)
        s2 = lax.dot_general(k2_ref[...], qb[:, N_KEYS:2 * N_KEYS], _NT, preferred_element_type=F32)
        v1, rank1, _ = _top16(s1)
        v2, rank2, _ = _top16(s2)
        cand = jnp.concatenate([v1[j:j + 1, :] + v2 for j in range(PEER_TOPK)], axis=0)
        sc, _, cnt = _top16(cand, n_out=PEER_TOPK)
        zsum = jnp.sum(jnp.exp(sc - sc[0:1, :]), axis=0, keepdims=True)
        n1 = jnp.zeros((N_KEYS, LANE), F32)
        for j in range(PEER_TOPK):
            n1 = jnp.where(rank1 == j, cnt[j:j + 1, :].astype(F32), n1)
        e1 = jnp.where(rank1 < PEER_TOPK, jnp.exp(s1 - v1[0:1, :]) / zsum, 0.0)
        e2 = jnp.where(rank2 < PEER_TOPK, jnp.exp(s2 - v2[0:1, :]), 0.0)
        e1_ref[:, ls] = e1
        n1_ref[:, ls] = n1
        e2_ref[:, ls] = e2.astype(BF16)
        r2_ref[:, ls] = rank2.astype(F32).astype(BF16)


def _route(q, k1, k2, tm=TM):
    t = q.shape[0]
    kd = k1.shape[-1]
    out = lambda: pl.BlockSpec((None, N_KEYS, tm), lambda i, h: (h, 0, i))
    key = lambda: pl.BlockSpec((None, N_KEYS, kd), lambda i, h: (h, 0, 0))
    return pl.pallas_call(
        functools.partial(_route_kernel, tm=tm),
        grid=(t // tm, PEER_HEADS),
        in_specs=[pl.BlockSpec((tm, 2 * kd), lambda i, h: (i, h)), key(), key()],
        out_specs=[out(), out(), out(), out()],
        out_shape=[jax.ShapeDtypeStruct((PEER_HEADS, N_KEYS, t), F32),
                   jax.ShapeDtypeStruct((PEER_HEADS, N_KEYS, t), F32),
                   jax.ShapeDtypeStruct((PEER_HEADS, N_KEYS, t), BF16),
                   jax.ShapeDtypeStruct((PEER_HEADS, N_KEYS, t), BF16)],
        compiler_params=_cparams(("parallel", "parallel")),
        name="peer_route",
    )(q, k1, k2)


def _peer_kernel(x_ref, u_ref, vt_ref, e1_ref, n1_ref, e2_ref, r2_ref, h_ref, out_ref, acc_ref, *, te):
    j = pl.program_id(1)

    @pl.when(j == 0)
    def _():
        acc_ref[...] = jnp.zeros_like(acc_ref)

    at = lax.dot_general(u_ref[...], x_ref[...], _NT, preferred_element_type=F32)
    act = (0.5 * at * (1.0 + lax.erf(at * (2.0 ** -0.5)))).astype(BF16)
    zero = jnp.zeros((), BF16)
    parts = []
    for c in range(te // N_KEYS):
        w = None
        for h in range(PEER_HEADS):
            n1 = n1_ref[h, c:c + 1, :].astype(BF16)
            e1 = e1_ref[h, c:c + 1, :].astype(BF16)
            wh = jnp.where(r2_ref[h] < n1, e1 * e2_ref[h], zero)
            w = wh if w is None else w + wh
        parts.append(act[c * N_KEYS:(c + 1) * N_KEYS, :] * w)
    ht = jnp.concatenate(parts, axis=0)
    acc_ref[...] += jnp.dot(vt_ref[...], ht, preferred_element_type=F32)

    @pl.when(j == pl.num_programs(1) - 1)
    def _():
        out_ref[...] = h_ref[...] + acc_ref[...].T


def _peer(x, u, vt, e1, n1, e2, r2, h, te=1024):
    t, d = h.shape
    ne = u.shape[0]
    rows = te // N_KEYS
    return pl.pallas_call(
        functools.partial(_peer_kernel, te=te),
        grid=(t // TM, ne // te),
        in_specs=[
            pl.BlockSpec((TM, d), lambda i, j: (i, 0)),
            pl.BlockSpec((te, d), lambda i, j: (j, 0)),
            pl.BlockSpec((d, te), lambda i, j: (0, j)),
            pl.BlockSpec((PEER_HEADS, rows, TM), lambda i, j: (0, j, i)),
            pl.BlockSpec((PEER_HEADS, rows, TM), lambda i, j: (0, j, i)),
            pl.BlockSpec((PEER_HEADS, N_KEYS, TM), lambda i, j: (0, 0, i)),
            pl.BlockSpec((PEER_HEADS, N_KEYS, TM), lambda i, j: (0, 0, i)),
            pl.BlockSpec((TM, d), lambda i, j: (i, 0)),
        ],
        out_specs=pl.BlockSpec((TM, d), lambda i, j: (i, 0)),
        out_shape=jax.ShapeDtypeStruct((t, d), F32),
        scratch_shapes=[pltpu.VMEM((d, TM), F32)],
        compiler_params=_cparams(("parallel", "arbitrary")),
        name="peer_experts",
    )(x, u, vt, e1, n1, e2, r2, h)


def _ple_kernel(h_ref, g_ref, p_ref, wg_ref, wp_ref, hc_ref, out_ref, xn_ref):
    @pl.when(pl.program_id(1) == 0)
    def _():
        xn_ref[...] = _rms(h_ref[...], g_ref[...]).astype(BF16)

    gate = jax.nn.sigmoid(jnp.dot(xn_ref[...], wg_ref[...], preferred_element_type=F32))
    emb = jnp.dot(p_ref[...].astype(BF16), wp_ref[...], preferred_element_type=F32)
    out_ref[...] = hc_ref[...] + emb * gate


def _ple(h, g, p, wg, wp, tn=512):
    t, d = h.shape
    pd = p.shape[1]
    return pl.pallas_call(
        _ple_kernel,
        grid=(t // TM, d // tn),
        in_specs=[
            pl.BlockSpec((TM, d), lambda i, j: (i, 0)),
            pl.BlockSpec((1, d), lambda i, j: (0, 0)),
            pl.BlockSpec((TM, pd), lambda i, j: (i, 0)),
            pl.BlockSpec((d, tn), lambda i, j: (0, j)),
            pl.BlockSpec((pd, tn), lambda i, j: (0, j)),
            pl.BlockSpec((TM, tn), lambda i, j: (i, j)),
        ],
        out_specs=pl.BlockSpec((TM, tn), lambda i, j: (i, j)),
        out_shape=jax.ShapeDtypeStruct((t, d), F32),
        scratch_shapes=[pltpu.VMEM((TM, d), BF16)],
        compiler_params=_cparams(("parallel", "arbitrary")),
        name="ple_gate",
    )(h, g.reshape(1, d), p, wg, wp, h)


def _lambda_init(layer_idx):
    return 0.8 - 0.6 * math.exp(-0.3 * layer_idx)


def _pad_rows(x, rows):
    return jnp.pad(x, ((0, rows - x.shape[0]),) + ((0, 0),) * (x.ndim - 1))


def kernel(x_prompt, x_sample, cache_k, cache_v, state_ssm, state_conv, p_prompt, p_sample, g_mix, w_in,
           conv_w, conv_b, dt_bias, a_log, d_skip, g_ssd, q_gain, k_gain, lam_q1, lam_k1, lam_q2, lam_k2,
           g_sub, w_out, g_ffn, peer_wq, peer_k1, peer_k2, peer_u, peer_v, g_ple, w_ple, w_pgate):
    nbp, seq, d = x_prompt.shape
    nbs, dseq, _ = x_sample.shape
    depth = w_in.shape[0]
    past = cache_k.shape[2]
    n_p = nbp * seq
    n_s = nbs * dseq
    t_pad = -(-(n_p + n_s) // TM) * TM

    h = _pad_rows(jnp.concatenate([x_prompt.reshape(n_p, d), x_sample.reshape(n_s, d)], axis=0), t_pad)
    ple_dim = p_prompt.shape[-1]
    p_all = jnp.concatenate([p_prompt.reshape(depth, n_p, ple_dim), p_sample.reshape(depth, n_s, ple_dim)], axis=1)
    p_all = jnp.pad(p_all, ((0, 0), (0, t_pad - n_p - n_s), (0, 0)))
    ck = cache_k.reshape(depth, nbs, past, D_ATT)
    cv = cache_v.reshape(depth, nbs, past, D_ATT)
    expand = (jnp.arange(LANE)[:, None] == jnp.arange(D_SSM)[None, :] // SSM_P).astype(F32)
    prev_zero = jnp.zeros((nbp, 8, CONV_DIM), F32)
    h0_zero = jnp.zeros((nbp, D_SSM, SSM_N), F32)

    outs = [[] for _ in range(8)]
    for l in range(depth):
        lam0 = _lambda_init(l)
        lam = (jnp.exp(jnp.sum(lam_q1[l] * lam_k1[l])) - jnp.exp(jnp.sum(lam_q2[l] * lam_k2[l])) + lam0)
        lam_s = jnp.stack([lam, jnp.asarray(1.0 - lam0, F32)]).astype(F32)
        lw = dict(
            conv_w=conv_w[l], conv_b=conv_b[l].reshape(1, CONV_DIM),
            dt_bias=_pad_rows(dt_bias[l], LANE).reshape(1, LANE),
            a_neg=_pad_rows(-jnp.exp(a_log[l]), LANE).reshape(1, LANE),
            d_exp=jnp.repeat(d_skip[l], SSM_P).reshape(1, D_SSM),
            g_ssd=g_ssd[l].reshape(1, D_SSM), expand=expand)
        w_main = w_in[l, :, :MAIN_DIM].astype(BF16)
        w_dt = jnp.pad(w_in[l, :, MAIN_DIM:], ((0, 0), (0, LANE - H_S))).astype(BF16)

        proj, dtp = _norm_proj(h, g_mix[l], w_main, wdt=w_dt)
        qn, kf, kb, vb = _qk_norm(proj, q_gain[l], k_gain[l])
        o_p = _attn_prompt(lam_s, qn, kb, vb, g_sub[l], nbp, seq)
        o_s = _attn_sample(lam_s, qn, kb, vb, ck, cv, l, g_sub[l], n_p, nbs, dseq)
        y_p, hT_p = _ssd(proj, dtp, prev_zero, h0_zero, lw, nbp, seq, SSD_Q)
        pad_seq = lambda a: jnp.pad(a[n_p:n_p + n_s].reshape(nbs, dseq, -1),
                                    ((0, 0), (0, SSD_Q - dseq), (0, 0))).reshape(nbs * SSD_Q, -1)
        prev_s = jnp.pad(state_conv[l], ((0, 0), (8 - (CONV_W - 1), 0), (0, 0)))
        y_s, hT_s = _ssd(pad_seq(proj), pad_seq(dtp), prev_s, state_ssm[l].reshape(nbs, D_SSM, SSM_N),
                         lw, nbs, SSD_Q, dseq)
        y_s = y_s.reshape(nbs, SSD_Q, D_SSM)[:, :dseq].reshape(n_s, D_SSM)
        y_all = _pad_rows(jnp.concatenate([y_p, y_s], axis=0), t_pad)
        o_all = _pad_rows(jnp.concatenate([o_p, o_s], axis=0), t_pad)
        h = _out_proj(y_all, o_all, w_out[l].astype(BF16), h)

        q_peer, c = _norm_proj(h, g_ffn[l], peer_wq[l].astype(BF16), with_xn=True)
        e1, n1, e2, r2 = _route(q_peer, peer_k1[l].astype(BF16), peer_k2[l].astype(BF16))
        h = _peer(c, peer_u[l].astype(BF16), peer_v[l].T.astype(BF16), e1, n1, e2, r2, h)

        h = _ple(h, g_ple[l], p_all[l], w_pgate[l].astype(BF16), w_ple[l].astype(BF16))

        xbc_raw = proj[:, 2 * D_ATT + D_ATT + D_SSM:MAIN_DIM]
        outs[0].append(kf[:n_p].reshape(nbp, seq, H_A, 2, DH_QK))
        outs[1].append(proj[:n_p, 2 * D_ATT:3 * D_ATT].reshape(nbp, seq, H_A, DH_V))
        outs[2].append(hT_p.reshape(nbp, H_S, SSM_P, SSM_N))
        outs[3].append(xbc_raw[:n_p].reshape(nbp, seq, CONV_DIM)[:, seq - (CONV_W - 1):])
        outs[4].append(kf[n_p:n_p + n_s].reshape(nbs, dseq, H_A, 2, DH_QK))
        outs[5].append(proj[n_p:n_p + n_s, 2 * D_ATT:3 * D_ATT].reshape(nbs, dseq, H_A, DH_V))
        outs[6].append(hT_s.reshape(nbs, H_S, SSM_P, SSM_N))
        outs[7].append(xbc_raw[n_p:n_p + n_s].reshape(nbs, dseq, CONV_DIM)[:, dseq - (CONV_W - 1):])

    y_prompt = h[:n_p].reshape(nbp, seq, d)
    y_sample = h[n_p:n_p + n_s].reshape(nbs, dseq, d)
    return (y_prompt, y_sample) + tuple(jnp.stack(o) for o in outs)
```

```python
import functools
import math

import jax
import jax.numpy as jnp
from jax import lax
from jax.experimental import pallas as pl
from jax.experimental.pallas import tpu as pltpu

F32 = jnp.float32
BF16 = jnp.bfloat16

EPS = 1e-6
CHUNK = 64
CHUNK_SHIFT = 6
H_A = 8
DH_QK = 64
DH_V = 128
D_ATT = H_A * DH_V
ATT_SCALE = DH_QK ** -0.5
SSM_P = 64
H_S = 16
SSM_N = 128
SSM_G = 2
D_SSM = H_S * SSM_P
CONV_W = 4
CONV_DIM = D_SSM + 2 * SSM_G * SSM_N
MAIN_DIM = 2 * D_ATT + D_ATT + D_SSM + CONV_DIM
N_KEYS = 128
PEER_HEADS = 8
PEER_TOPK = 16
TOPK_SHIFT = 4

LANE = 128
TM = 512
SSD_Q = 128
VMEM_LIMIT = 56 * 1024 * 1024

_NT = (((1,), (1,)), ((), ()))
_HI = lax.Precision.HIGHEST


def _cparams(sem):
    return pltpu.CompilerParams(dimension_semantics=sem, vmem_limit_bytes=VMEM_LIMIT)


def _rms(x, g):
    ms = jnp.mean(x * x, axis=-1, keepdims=True)
    return x * lax.rsqrt(ms + EPS) * g


def _norm_proj_kernel(x_ref, g_ref, w_ref, *rest, with_dt, with_xn):
    rest = list(rest)
    wdt_ref = rest.pop(0) if with_dt else None
    o_ref = rest.pop(0)
    odt_ref = rest.pop(0) if with_dt else None
    oxn_ref = rest.pop(0) if with_xn else None
    xn_ref = rest.pop(0)

    @pl.when(pl.program_id(1) == 0)
    def _():
        xn = _rms(x_ref[...], g_ref[...]).astype(BF16)
        xn_ref[...] = xn
        if with_xn:
            oxn_ref[...] = xn
        if with_dt:
            odt_ref[...] = jnp.dot(xn, wdt_ref[...], preferred_element_type=F32)

    o_ref[...] = jnp.dot(xn_ref[...], w_ref[...], preferred_element_type=F32)


def _norm_proj(x, g, w, layer, wdt=None, with_xn=False, tn=512):
    t, d = x.shape
    n = w.shape[2]
    with_dt = wdt is not None
    in_specs = [
        pl.BlockSpec((TM, d), lambda i, j: (i, 0)),
        pl.BlockSpec((1, d), lambda i, j: (0, 0)),
        pl.BlockSpec((None, d, tn), lambda i, j: (layer, 0, j)),
    ]
    args = [x, g.reshape(1, d), w]
    out_shape = [jax.ShapeDtypeStruct((t, n), F32)]
    out_specs = [pl.BlockSpec((TM, tn), lambda i, j: (i, j))]
    if with_dt:
        in_specs.append(pl.BlockSpec((None, d, LANE), lambda i, j: (layer, 0, 0)))
        args.append(wdt)
        out_shape.append(jax.ShapeDtypeStruct((t, LANE), F32))
        out_specs.append(pl.BlockSpec((TM, LANE), lambda i, j: (i, 0)))
    if with_xn:
        out_shape.append(jax.ShapeDtypeStruct((t, d), BF16))
        out_specs.append(pl.BlockSpec((TM, d), lambda i, j: (i, 0)))
    return pl.pallas_call(
        functools.partial(_norm_proj_kernel, with_dt=with_dt, with_xn=with_xn),
        grid=(t // TM, n // tn),
        in_specs=in_specs,
        out_specs=out_specs,
        out_shape=out_shape,
        scratch_shapes=[pltpu.VMEM((TM, d), BF16)],
        compiler_params=_cparams(("parallel", "arbitrary")),
        name="norm_proj",
    )(*args)


def _qk_norm_kernel(q_ref, k_ref, v_ref, qg_ref, kg_ref, qn_ref, kf_ref, kb_ref, vb_ref):
    lane = lax.broadcasted_iota(jnp.int32, (1, DH_V), 1)
    lo = lane < DH_QK

    def norm(x, g):
        xx = x * x
        s_lo = jnp.sum(jnp.where(lo, xx, 0.0), axis=-1, keepdims=True)
        s_all = jnp.sum(xx, axis=-1, keepdims=True)
        ms = jnp.where(lo, s_lo, s_all - s_lo) * (1.0 / DH_QK)
        return x * lax.rsqrt(ms + EPS) * g

    for h in range(H_A):
        sl = slice(h * DH_V, (h + 1) * DH_V)
        qn_ref[:, sl] = (norm(q_ref[:, sl], qg_ref[...]) * ATT_SCALE).astype(BF16)
        kn = norm(k_ref[:, sl], kg_ref[...])
        kf_ref[:, sl] = kn
        kb_ref[:, sl] = kn.astype(BF16)
    vb_ref[...] = v_ref[...].astype(BF16)


def _qk_norm(proj, q_gain, k_gain):
    t = proj.shape[0]
    blk = lambda c: pl.BlockSpec((TM, D_ATT), lambda i, c=c: (i, c))
    gspec = pl.BlockSpec((1, DH_V), lambda i: (0, 0))
    return pl.pallas_call(
        _qk_norm_kernel,
        grid=(t // TM,),
        in_specs=[blk(0), blk(1), blk(2), gspec, gspec],
        out_specs=[blk(0)] * 4,
        out_shape=[
            jax.ShapeDtypeStruct((t, D_ATT), BF16),
            jax.ShapeDtypeStruct((t, D_ATT), F32),
            jax.ShapeDtypeStruct((t, D_ATT), BF16),
            jax.ShapeDtypeStruct((t, D_ATT), BF16),
        ],
        compiler_params=_cparams(("parallel",)),
        name="qk_norm",
    )(proj, proj, proj, q_gain.reshape(1, DH_V), k_gain.reshape(1, DH_V))


def _sub_norm(o, lam_ref, gsub):
    return _rms(o, gsub) * lam_ref[1]


def _attn_prompt_kernel(lam_ref, q_ref, k_ref, v_ref, gsub_ref, o_ref, *, tq):
    qi = pl.program_id(2)
    q = q_ref[...]
    lane = lax.broadcasted_iota(jnp.int32, (1, DH_V), 1)
    zero = jnp.zeros((), BF16)
    q0 = jnp.where(lane < DH_QK, q, zero)
    q1 = jnp.where(lane >= DH_QK, q, zero)

    def scores(kb):
        s0 = lax.dot_general(q0, kb, _NT, preferred_element_type=F32)
        s1 = lax.dot_general(q1, kb, _NT, preferred_element_type=F32)
        return s0, s1

    start = pl.multiple_of(qi * tq, tq)
    kb = k_ref[pl.ds(start, tq), :]
    vb = v_ref[pl.ds(start, tq), :]
    row = jnp.right_shift(lax.broadcasted_iota(jnp.int32, (tq, tq), 0), CHUNK_SHIFT)
    col = jnp.right_shift(lax.broadcasted_iota(jnp.int32, (tq, tq), 1), CHUNK_SHIFT)
    mask = col <= row
    s0, s1 = scores(kb)
    s0 = jnp.where(mask, s0, -jnp.inf)
    s1 = jnp.where(mask, s1, -jnp.inf)
    m0 = jnp.max(s0, axis=-1, keepdims=True)
    m1 = jnp.max(s1, axis=-1, keepdims=True)
    p0 = jnp.exp(s0 - m0)
    p1 = jnp.exp(s1 - m1)
    l0 = jnp.sum(p0, axis=-1, keepdims=True)
    l1 = jnp.sum(p1, axis=-1, keepdims=True)
    a0 = jnp.dot(p0.astype(BF16), vb, preferred_element_type=F32)
    a1 = jnp.dot(p1.astype(BF16), vb, preferred_element_type=F32)

    def body(j, carry):
        m0, l0, a0, m1, l1, a1 = carry
        st = pl.multiple_of(j * tq, tq)
        kb = k_ref[pl.ds(st, tq), :]
        vb = v_ref[pl.ds(st, tq), :]
        s0, s1 = scores(kb)

        def upd(s, m, l, a):
            mn = jnp.maximum(m, jnp.max(s, axis=-1, keepdims=True))
            alpha = jnp.exp(m - mn)
            p = jnp.exp(s - mn)
            l = alpha * l + jnp.sum(p, axis=-1, keepdims=True)
            a = alpha * a + jnp.dot(p.astype(BF16), vb, preferred_element_type=F32)
            return mn, l, a

        m0, l0, a0 = upd(s0, m0, l0, a0)
        m1, l1, a1 = upd(s1, m1, l1, a1)
        return m0, l0, a0, m1, l1, a1

    m0, l0, a0, m1, l1, a1 = lax.fori_loop(0, qi, body, (m0, l0, a0, m1, l1, a1))
    o = a0 / l0 - lam_ref[0] * (a1 / l1)
    o_ref[...] = _sub_norm(o, lam_ref, gsub_ref[...]).astype(BF16)


def _attn_prompt(lam, qn, kb, vb, g_sub, nb, seq):
    tq = min(512, seq)
    nq = seq // tq
    return pl.pallas_call(
        functools.partial(_attn_prompt_kernel, tq=tq),
        grid=(nb, H_A, nq),
        in_specs=[
            pl.BlockSpec(memory_space=pltpu.SMEM),
            pl.BlockSpec((tq, DH_V), lambda b, h, i: (b * nq + i, h)),
            pl.BlockSpec((seq, DH_V), lambda b, h, i: (b, h)),
            pl.BlockSpec((seq, DH_V), lambda b, h, i: (b, h)),
            pl.BlockSpec((1, DH_V), lambda b, h, i: (0, 0)),
        ],
        out_specs=pl.BlockSpec((tq, DH_V), lambda b, h, i: (b * nq + i, h)),
        out_shape=jax.ShapeDtypeStruct((nb * seq, D_ATT), BF16),
        compiler_params=_cparams(("parallel", "parallel", "arbitrary")),
        name="attn_prompt",
    )(lam, qn, kb, vb, g_sub.reshape(1, DH_V))


def _attn_sample_kernel(lam_ref, q_ref, kn_ref, vn_ref, ck_ref, cv_ref, gsub_ref, o_ref):
    q = q_ref[...]
    nq = q.shape[0]
    lane = lax.broadcasted_iota(jnp.int32, (1, DH_V), 1)
    zero = jnp.zeros((), BF16)
    q01 = jnp.concatenate([jnp.where(lane < DH_QK, q, zero), jnp.where(lane >= DH_QK, q, zero)], axis=0)
    sc = lax.dot_general(q01, ck_ref[...].astype(BF16), _NT, preferred_element_type=F32)
    sn = lax.dot_general(q01, kn_ref[...], _NT, preferred_element_type=F32)
    m = jnp.maximum(jnp.max(sc, axis=-1, keepdims=True), jnp.max(sn, axis=-1, keepdims=True))
    pc = jnp.exp(sc - m)
    pn = jnp.exp(sn - m)
    inv = 1.0 / (jnp.sum(pc, axis=-1, keepdims=True) + jnp.sum(pn, axis=-1, keepdims=True))
    pc = pc * inv
    pn = pn * inv
    lam = lam_ref[0]
    wc = (pc[:nq] - lam * pc[nq:]).astype(BF16)
    wn = (pn[:nq] - lam * pn[nq:]).astype(BF16)
    o = jnp.dot(wc, cv_ref[...].astype(BF16), preferred_element_type=F32)
    o = o + jnp.dot(wn, vn_ref[...], preferred_element_type=F32)
    o_ref[...] = _sub_norm(o, lam_ref, gsub_ref[...]).astype(BF16)


def _attn_sample(lam, qn, kb, vb, cache_k, cache_v, layer, g_sub, row0, nb, nq):
    past = cache_k.shape[2]
    r0 = row0 // nq
    new = lambda: pl.BlockSpec((nq, DH_V), lambda b, h: (r0 + b, h))
    cache = lambda: pl.BlockSpec((None, None, past, DH_V), lambda b, h: (layer, b, 0, h))
    return pl.pallas_call(
        _attn_sample_kernel,
        grid=(nb, H_A),
        in_specs=[pl.BlockSpec(memory_space=pltpu.SMEM), new(), new(), new(), cache(), cache(),
                  pl.BlockSpec((1, DH_V), lambda b, h: (0, 0))],
        out_specs=pl.BlockSpec((nq, DH_V), lambda b, h: (b, h)),
        out_shape=jax.ShapeDtypeStruct((nb * nq, D_ATT), BF16),
        compiler_params=_cparams(("parallel", "parallel")),
        name="attn_sample",
    )(lam, qn, kb, vb, cache_k, cache_v, g_sub.reshape(1, DH_V))


def _ssd_kernel(xs_ref, b_ref, c_ref, z_ref, dt_ref, prev_ref, h0_ref, cw_ref, cb_ref, dtb_ref,
                aneg_ref, dexp_ref, gssd_ref, expand_ref, y_ref, hout_ref, ext_ref, ht_ref, *, valid):
    q = SSD_Q
    c = pl.program_id(1)
    halo = 8

    @pl.when(c == 0)
    def _():
        ext_ref[0:halo, :] = prev_ref[...]
        ht_ref[...] = h0_ref[...].T

    ext_ref[halo:halo + q, 0:D_SSM] = xs_ref[...]
    ext_ref[halo:halo + q, D_SSM:D_SSM + SSM_G * SSM_N] = b_ref[...]
    ext_ref[halo:halo + q, D_SSM + SSM_G * SSM_N:CONV_DIM] = c_ref[...]
    conv = cb_ref[...]
    for j in range(CONV_W):
        r0 = halo - (CONV_W - 1) + j
        conv = conv + ext_ref[r0:r0 + q, :] * cw_ref[j:j + 1, :]
    tail = ext_ref[q:q + halo, :]
    ext_ref[0:halo, :] = tail
    xbc = conv * jax.nn.sigmoid(conv)
    xs = xbc[:, 0:D_SSM]
    bm = xbc[:, D_SSM:D_SSM + SSM_G * SSM_N]
    cm = xbc[:, D_SSM + SSM_G * SSM_N:CONV_DIM]

    dt = jax.nn.softplus(dt_ref[...] + dtb_ref[...])
    if valid < q:
        rowv = lax.broadcasted_iota(jnp.int32, (q, LANE), 0) < valid
        dt = jnp.where(rowv, dt, 0.0)
    a = dt * aneg_ref[...]
    r_i = lax.broadcasted_iota(jnp.int32, (q, q), 0)
    c_i = lax.broadcasted_iota(jnp.int32, (q, q), 1)
    tril = c_i <= r_i
    acum = jnp.dot(tril.astype(F32), a, precision=_HI, preferred_element_type=F32)
    eye = (r_i == c_i).astype(F32)
    acum_t = lax.dot_general(eye, acum, _NT, precision=_HI, preferred_element_type=F32)
    a_last = acum[q - 1:q, :]
    decay_in = jnp.exp(a_last - acum)
    stack = jnp.concatenate(
        [dt, dt * decay_in, jnp.exp(acum), jnp.broadcast_to(jnp.exp(a_last), (8, LANE))], axis=0)
    ex = jnp.dot(stack, expand_ref[...], precision=_HI, preferred_element_type=F32)
    dt_e = ex[0:q]
    dtd_e = ex[q:2 * q]
    eacum_e = ex[2 * q:3 * q]
    cd_e = ex[3 * q:3 * q + 1]

    xdt = (xs * dt_e).astype(BF16)
    xdtd = (xs * dtd_e).astype(BF16)
    lane = lax.broadcasted_iota(jnp.int32, (1, LANE), 1)
    zero = jnp.zeros((), BF16)
    gw = D_SSM // SSM_G
    hpg = H_S // SSM_G
    y_diag_parts = []
    y_off_parts = []
    for g in range(SSM_G):
        gs = slice(g * gw, (g + 1) * gw)
        bg = bm[:, g * SSM_N:(g + 1) * SSM_N]
        cg = cm[:, g * SSM_N:(g + 1) * SSM_N].astype(BF16)
        cb = lax.dot_general(cg, bg.astype(BF16), _NT, preferred_element_type=F32)
        ht_prev = ht_ref[:, gs]
        y_off = jnp.dot(cg, ht_prev.astype(BF16), preferred_element_type=F32)
        ht_ref[:, gs] = cd_e[:, gs] * ht_prev + jnp.dot(
            bg.T.astype(BF16), xdtd[:, gs], preferred_element_type=F32)
        for k in range(hpg // 2):
            ms = []
            for h in (g * hpg + 2 * k, g * hpg + 2 * k + 1):
                seg = acum[:, h:h + 1] - acum_t[h:h + 1, :]
                ms.append((cb * jnp.exp(jnp.where(tril, seg, -jnp.inf))).astype(BF16))
            pair = slice((g * hpg + 2 * k) * SSM_P, (g * hpg + 2 * k + 2) * SSM_P)
            xp = xdt[:, pair]
            rhs = jnp.concatenate([jnp.where(lane < SSM_P, xp, zero), jnp.where(lane >= SSM_P, xp, zero)], axis=0)
            y_diag_parts.append(jnp.dot(jnp.concatenate(ms, axis=1), rhs, preferred_element_type=F32))
        y_off_parts.append(y_off)
    y_diag = jnp.concatenate(y_diag_parts, axis=1)
    y_off = jnp.concatenate(y_off_parts, axis=1)
    y = y_diag + y_off * eacum_e + dexp_ref[...] * xs
    zz = z_ref[...]
    y = y * (zz * jax.nn.sigmoid(zz))
    for g in range(SSM_G):
        gs = slice(g * gw, (g + 1) * gw)
        y_ref[:, gs] = _rms(y[:, gs], gssd_ref[:, gs]).astype(BF16)

    @pl.when(c == pl.num_programs(1) - 1)
    def _():
        hout_ref[...] = ht_ref[...].T


def _ssd(proj, dtp, prev, h0, lw, nb, seq, valid):
    nc = seq // SSD_Q
    row = lambda w, col: pl.BlockSpec((SSD_Q, w), lambda b, c, col=col: (b * nc + c, col))
    const = lambda shp: pl.BlockSpec(shp, lambda b, c: (0,) * len(shp))
    per_b = lambda shp: pl.BlockSpec((None,) + shp, lambda b, c: (b, 0, 0))
    return pl.pallas_call(
        functools.partial(_ssd_kernel, valid=valid),
        grid=(nb, nc),
        in_specs=[
            row(D_SSM, 4), row(SSM_G * SSM_N, 20), row(SSM_G * SSM_N, 21), row(D_SSM, 3), row(LANE, 0),
            per_b((8, CONV_DIM)), per_b((D_SSM, SSM_N)),
            const((CONV_W, CONV_DIM)), const((1, CONV_DIM)), const((1, LANE)), const((1, LANE)),
            const((1, D_SSM)), const((1, D_SSM)), const((LANE, D_SSM)),
        ],
        out_specs=[row(D_SSM, 0), per_b((D_SSM, SSM_N))],
        out_shape=[jax.ShapeDtypeStruct((nb * seq, D_SSM), BF16),
                   jax.ShapeDtypeStruct((nb, D_SSM, SSM_N), F32)],
        scratch_shapes=[pltpu.VMEM((SSD_Q + 8, CONV_DIM), F32), pltpu.VMEM((SSM_N, D_SSM), F32)],
        compiler_params=_cparams(("parallel", "arbitrary")),
        name="ssd",
    )(proj, proj, proj, proj, dtp, prev, h0, lw["conv_w"], lw["conv_b"], lw["dt_bias"], lw["a_neg"],
      lw["d_exp"], lw["g_ssd"], lw["expand"])


def _out_proj_kernel(y_ref, o_ref, w_ref, h_ref, out_ref):
    acc = jnp.dot(y_ref[...], w_ref[0:D_SSM, :], preferred_element_type=F32)
    acc = acc + jnp.dot(o_ref[...], w_ref[D_SSM:D_SSM + D_ATT, :], preferred_element_type=F32)
    out_ref[...] = h_ref[...] + acc


def _out_proj(y, o, w, layer, h, tn=512):
    t, d = h.shape
    return pl.pallas_call(
        _out_proj_kernel,
        grid=(t // TM, d // tn),
        in_specs=[
            pl.BlockSpec((TM, D_SSM), lambda i, j: (i, 0)),
            pl.BlockSpec((TM, D_ATT), lambda i, j: (i, 0)),
            pl.BlockSpec((None, D_SSM + D_ATT, tn), lambda i, j: (layer, 0, j)),
            pl.BlockSpec((TM, tn), lambda i, j: (i, j)),
        ],
        out_specs=pl.BlockSpec((TM, tn), lambda i, j: (i, j)),
        out_shape=jax.ShapeDtypeStruct((t, d), F32),
        compiler_params=_cparams(("parallel", "parallel")),
        name="out_proj",
    )(y, o, w, h)


def _top16(s, n_out=None):
    n, t = s.shape
    iota = lax.broadcasted_iota(jnp.int32, (n, t), 0)
    row16 = lax.broadcasted_iota(jnp.int32, (PEER_TOPK, t), 0)
    rank = jnp.full((n, t), PEER_TOPK, jnp.int32)
    vals = jnp.zeros((PEER_TOPK, t), F32)
    cnt = jnp.zeros((n_out, t), jnp.int32) if n_out else None
    for r in range(PEER_TOPK):
        m = jnp.max(s, axis=0, keepdims=True)
        idx = jnp.min(jnp.where(s == m, iota, n), axis=0, keepdims=True)
        sel = iota == idx
        rank = jnp.where(sel, r, rank)
        s = jnp.where(sel, -jnp.inf, s)
        vals = jnp.where(row16 == r, m, vals)
        if n_out:
            cnt = cnt + (row16 == jnp.right_shift(idx, TOPK_SHIFT)).astype(jnp.int32)
    return vals, rank, cnt


_MARK = 2.0 ** 126


def _peel16(s):
    t = s.shape[1]
    row16 = lax.broadcasted_iota(jnp.int32, (PEER_TOPK, t), 0)
    vals = jnp.zeros((PEER_TOPK, t), F32)
    for r in range(PEER_TOPK):
        m = jnp.max(s, axis=0, keepdims=True)
        s = jnp.where(s == m, -(1.0 + (r + 1) / 32.0) * _MARK, s)
        vals = jnp.where(row16 == r, m, vals)
    marked = s < -_MARK
    rank = jnp.where(marked, s * (-32.0 / _MARK) - 33.0, float(PEER_TOPK))
    count = jnp.sum(jnp.where(marked, 1.0, 0.0), axis=0, keepdims=True)
    return vals, rank, count


def _route_outputs(s1, s2, v1, v2, rank1, rank2, cnt, zsum):
    n1 = jnp.zeros(s1.shape, F32)
    for j in range(PEER_TOPK):
        n1 = jnp.where(rank1 == j, cnt[j:j + 1, :], n1)
    e1 = jnp.where(rank1 < PEER_TOPK, jnp.exp(s1 - v1[0:1, :]) / zsum, 0.0)
    e2 = jnp.where(rank2 < PEER_TOPK, jnp.exp(s2 - v2[0:1, :]), 0.0)
    return e1, n1, e2.astype(BF16), rank2.astype(BF16)


def _route_exact(s1, s2):
    v1, rank1, _ = _top16(s1)
    v2, rank2, _ = _top16(s2)
    cand = jnp.concatenate([v1[j:j + 1, :] + v2 for j in range(PEER_TOPK)], axis=0)
    sc, _, cnt = _top16(cand, n_out=PEER_TOPK)
    zsum = jnp.sum(jnp.exp(sc - sc[0:1, :]), axis=0, keepdims=True)
    return _route_outputs(s1, s2, v1, v2, rank1.astype(F32), rank2.astype(F32), cnt.astype(F32), zsum)


def _route_fast(s1, s2):
    half = PEER_TOPK // 2
    t = s1.shape[1]
    v1, rank1, c1 = _peel16(s1)
    v2, rank2, c2 = _peel16(s2)
    pieces = [v1[j:j + 1, :] + v2[0:half, :] for j in range(half)]
    pieces.append(v1[half:, :] + v2[0:1, :])
    pieces.append(v1[0:1, :] + v2[half:, :])
    cand = jnp.concatenate(pieces, axis=0)
    sc, rankc, cc = _peel16(cand)
    sel = rankc < PEER_TOPK
    zsum = jnp.sum(jnp.where(sel, jnp.exp(cand - sc[0:1, :]), 0.0), axis=0, keepdims=True)
    mk = jnp.where(sel, 1.0, 0.0)
    row8 = lax.broadcasted_iota(jnp.int32, (half, t), 0)
    n_lo = jnp.zeros((half, t), F32)
    for j in range(half):
        rs = jnp.sum(mk[j * half:(j + 1) * half, :], axis=0, keepdims=True)
        if j == 0:
            rs = rs + jnp.sum(mk[(half + 1) * half:, :], axis=0, keepdims=True)
        n_lo = jnp.where(row8 == j, rs, n_lo)
    cnt = jnp.concatenate([n_lo, mk[half * half:(half + 1) * half, :]], axis=0)
    bad = jnp.abs(c1 - PEER_TOPK) + jnp.abs(c2 - PEER_TOPK) + jnp.abs(cc - PEER_TOPK)
    return _route_outputs(s1, s2, v1, v2, rank1, rank2, cnt, zsum), bad


def _route_kernel(q_ref, k1_ref, k2_ref, e1_ref, n1_ref, e2_ref, r2_ref, *, tm):
    def store(ls, outs):
        e1_ref[:, ls], n1_ref[:, ls], e2_ref[:, ls], r2_ref[:, ls] = outs

    for lt in range(tm // LANE):
        ls = slice(lt * LANE, (lt + 1) * LANE)
        qb = q_ref[ls, :].astype(BF16)
        s1 = lax.dot_general(k1_ref[...], qb[:, 0:N_KEYS], _NT, preferred_element_type=F32)
        s2 = lax.dot_general(k2_ref[...], qb[:, N_KEYS:2 * N_KEYS], _NT, preferred_element_type=F32)
        outs, bad = _route_fast(s1, s2)
        store(ls, outs)

        @pl.when(jnp.max(bad) > 0.0)
        def _(ls=ls, s1=s1, s2=s2):
            store(ls, _route_exact(s1, s2))


def _route(q, k1, k2, layer, tm=TM):
    t = q.shape[0]
    kd = k1.shape[-1]
    out = lambda: pl.BlockSpec((None, N_KEYS, tm), lambda i, h: (h, 0, i))
    key = lambda: pl.BlockSpec((None, None, N_KEYS, kd), lambda i, h: (layer, h, 0, 0))
    return pl.pallas_call(
        functools.partial(_route_kernel, tm=tm),
        grid=(t // tm, PEER_HEADS),
        in_specs=[pl.BlockSpec((tm, 2 * kd), lambda i, h: (i, h)), key(), key()],
        out_specs=[out(), out(), out(), out()],
        out_shape=[jax.ShapeDtypeStruct((PEER_HEADS, N_KEYS, t), F32),
                   jax.ShapeDtypeStruct((PEER_HEADS, N_KEYS, t), F32),
                   jax.ShapeDtypeStruct((PEER_HEADS, N_KEYS, t), BF16),
                   jax.ShapeDtypeStruct((PEER_HEADS, N_KEYS, t), BF16)],
        compiler_params=_cparams(("parallel", "parallel")),
        name="peer_route",
    )(q, k1, k2)


def _peer_kernel(x_ref, u_ref, vt_ref, e1_ref, n1_ref, e2_ref, r2_ref, h_ref, out_ref, acc_ref, *, te):
    j = pl.program_id(1)

    @pl.when(j == 0)
    def _():
        acc_ref[...] = jnp.zeros_like(acc_ref)

    zero = jnp.zeros((), BF16)
    pack = 16
    tw = x_ref.shape[0] // 2
    for s in range(2):
        ls = slice(s * tw, (s + 1) * tw)
        at = lax.dot_general(u_ref[...], x_ref[ls, :], _NT, preferred_element_type=F32)
        act = (0.5 * at * (1.0 + lax.erf(at * (2.0 ** -0.5)))).astype(BF16)
        parts = []
        for c in range(te // N_KEYS):
            w = None
            for h in range(PEER_HEADS):
                n1 = jnp.tile(jnp.broadcast_to(n1_ref[h, c:c + 1, ls], (pack, tw)).astype(BF16), (N_KEYS // pack, 1))
                e1 = jnp.tile(jnp.broadcast_to(e1_ref[h, c:c + 1, ls], (pack, tw)).astype(BF16), (N_KEYS // pack, 1))
                wh = jnp.where(r2_ref[h, :, ls] < n1, e1 * e2_ref[h, :, ls], zero)
                w = wh if w is None else w + wh
            parts.append(act[c * N_KEYS:(c + 1) * N_KEYS, :] * w)
        ht = jnp.concatenate(parts, axis=0)
        acc_ref[:, ls] += jnp.dot(vt_ref[...], ht, preferred_element_type=F32)

    @pl.when(j == pl.num_programs(1) - 1)
    def _():
        out_ref[...] = h_ref[...] + acc_ref[...].T


def _peer(x, u, vt, layer, e1, n1, e2, r2, h, te=1024):
    t, d = h.shape
    ne = u.shape[1]
    rows = te // N_KEYS
    return pl.pallas_call(
        functools.partial(_peer_kernel, te=te),
        grid=(t // TM, ne // te),
        in_specs=[
            pl.BlockSpec((TM, d), lambda i, j: (i, 0)),
            pl.BlockSpec((None, te, d), lambda i, j: (layer, j, 0)),
            pl.BlockSpec((None, d, te), lambda i, j: (layer, 0, j)),
            pl.BlockSpec((PEER_HEADS, rows, TM), lambda i, j: (0, j, i)),
            pl.BlockSpec((PEER_HEADS, rows, TM), lambda i, j: (0, j, i)),
            pl.BlockSpec((PEER_HEADS, N_KEYS, TM), lambda i, j: (0, 0, i)),
            pl.BlockSpec((PEER_HEADS, N_KEYS, TM), lambda i, j: (0, 0, i)),
            pl.BlockSpec((TM, d), lambda i, j: (i, 0)),
        ],
        out_specs=pl.BlockSpec((TM, d), lambda i, j: (i, 0)),
        out_shape=jax.ShapeDtypeStruct((t, d), F32),
        scratch_shapes=[pltpu.VMEM((d, TM), F32)],
        compiler_params=_cparams(("parallel", "arbitrary")),
        name="peer_experts",
    )(x, u, vt, e1, n1, e2, r2, h)


def _ple_kernel(h_ref, g_ref, p_ref, wg_ref, wp_ref, hc_ref, out_ref, xn_ref):
    @pl.when(pl.program_id(1) == 0)
    def _():
        xn_ref[...] = _rms(h_ref[...], g_ref[...]).astype(BF16)

    gate = jax.nn.sigmoid(jnp.dot(xn_ref[...], wg_ref[...], preferred_element_type=F32))
    emb = jnp.dot(p_ref[...].astype(BF16), wp_ref[...], preferred_element_type=F32)
    out_ref[...] = hc_ref[...] + emb * gate


def _ple(h, g, p, wg, wp, layer, tn=512):
    t, d = h.shape
    pd = p.shape[2]
    return pl.pallas_call(
        _ple_kernel,
        grid=(t // TM, d // tn),
        in_specs=[
            pl.BlockSpec((TM, d), lambda i, j: (i, 0)),
            pl.BlockSpec((1, d), lambda i, j: (0, 0)),
            pl.BlockSpec((None, TM, pd), lambda i, j: (layer, i, 0)),
            pl.BlockSpec((None, d, tn), lambda i, j: (layer, 0, j)),
            pl.BlockSpec((None, pd, tn), lambda i, j: (layer, 0, j)),
            pl.BlockSpec((TM, tn), lambda i, j: (i, j)),
        ],
        out_specs=pl.BlockSpec((TM, tn), lambda i, j: (i, j)),
        out_shape=jax.ShapeDtypeStruct((t, d), F32),
        scratch_shapes=[pltpu.VMEM((TM, d), BF16)],
        compiler_params=_cparams(("parallel", "arbitrary")),
        name="ple_gate",
    )(h, g.reshape(1, d), p, wg, wp, h)


def _lambda_init(layer_idx):
    return 0.8 - 0.6 * math.exp(-0.3 * layer_idx)


def _pad_rows(x, rows):
    return jnp.pad(x, ((0, rows - x.shape[0]),) + ((0, 0),) * (x.ndim - 1))


def kernel(x_prompt, x_sample, cache_k, cache_v, state_ssm, state_conv, p_prompt, p_sample, g_mix, w_in,
           conv_w, conv_b, dt_bias, a_log, d_skip, g_ssd, q_gain, k_gain, lam_q1, lam_k1, lam_q2, lam_k2,
           g_sub, w_out, g_ffn, peer_wq, peer_k1, peer_k2, peer_u, peer_v, g_ple, w_ple, w_pgate):
    nbp, seq, d = x_prompt.shape
    nbs, dseq, _ = x_sample.shape
    depth = w_in.shape[0]
    past = cache_k.shape[2]
    n_p = nbp * seq
    n_s = nbs * dseq
    t_pad = -(-(n_p + n_s) // TM) * TM

    h = _pad_rows(jnp.concatenate([x_prompt.reshape(n_p, d), x_sample.reshape(n_s, d)], axis=0), t_pad)
    ple_dim = p_prompt.shape[-1]
    p_all = jnp.concatenate([p_prompt.reshape(depth, n_p, ple_dim), p_sample.reshape(depth, n_s, ple_dim)], axis=1)
    p_all = jnp.pad(p_all, ((0, 0), (0, t_pad - n_p - n_s), (0, 0)))
    ck = cache_k.reshape(depth, nbs, past, D_ATT)
    cv = cache_v.reshape(depth, nbs, past, D_ATT)
    expand = (jnp.arange(LANE)[:, None] == jnp.arange(D_SSM)[None, :] // SSM_P).astype(F32)
    prev_zero = jnp.zeros((nbp, 8, CONV_DIM), F32)
    h0_zero = jnp.zeros((nbp, D_SSM, SSM_N), F32)

    w_main = w_in[:, :, :MAIN_DIM].astype(BF16)
    w_dt = jnp.pad(w_in[:, :, MAIN_DIM:], ((0, 0), (0, 0), (0, LANE - H_S))).astype(BF16)
    w_out_b = w_out.astype(BF16)
    wq_b = peer_wq.astype(BF16)
    k1_b = peer_k1.astype(BF16)
    k2_b = peer_k2.astype(BF16)
    u_b = peer_u.astype(BF16)
    vt_b = jnp.swapaxes(peer_v, 1, 2).astype(BF16)
    wg_b = w_pgate.astype(BF16)
    wp_b = w_ple.astype(BF16)

    outs = [[] for _ in range(8)]
    for l in range(depth):
        lam0 = _lambda_init(l)
        lam = (jnp.exp(jnp.sum(lam_q1[l] * lam_k1[l])) - jnp.exp(jnp.sum(lam_q2[l] * lam_k2[l])) + lam0)
        lam_s = jnp.stack([lam, jnp.asarray(1.0 - lam0, F32)]).astype(F32)
        lw = dict(
            conv_w=conv_w[l], conv_b=conv_b[l].reshape(1, CONV_DIM),
            dt_bias=_pad_rows(dt_bias[l], LANE).reshape(1, LANE),
            a_neg=_pad_rows(-jnp.exp(a_log[l]), LANE).reshape(1, LANE),
            d_exp=jnp.repeat(d_skip[l], SSM_P).reshape(1, D_SSM),
            g_ssd=g_ssd[l].reshape(1, D_SSM), expand=expand)

        proj, dtp = _norm_proj(h, g_mix[l], w_main, l, wdt=w_dt)
        qn, kf, kb, vb = _qk_norm(proj, q_gain[l], k_gain[l])
        o_p = _attn_prompt(lam_s, qn, kb, vb, g_sub[l], nbp, seq)
        o_s = _attn_sample(lam_s, qn, kb, vb, ck, cv, l, g_sub[l], n_p, nbs, dseq)
        y_p, hT_p = _ssd(proj, dtp, prev_zero, h0_zero, lw, nbp, seq, SSD_Q)
        proj_s = proj[n_p:n_p + n_s].reshape(nbs, dseq, MAIN_DIM)
        pad_seq = lambda a: jnp.pad(a, ((0, 0), (0, SSD_Q - dseq), (0, 0))).reshape(nbs * SSD_Q, a.shape[-1])
        prev_s = jnp.pad(state_conv[l], ((0, 0), (8 - (CONV_W - 1), 0), (0, 0)))
        y_s, hT_s = _ssd(pad_seq(proj_s), pad_seq(dtp[n_p:n_p + n_s].reshape(nbs, dseq, LANE)), prev_s,
                         state_ssm[l].reshape(nbs, D_SSM, SSM_N), lw, nbs, SSD_Q, dseq)
        y_s = y_s.reshape(nbs, SSD_Q, D_SSM)[:, :dseq].reshape(n_s, D_SSM)
        y_all = _pad_rows(jnp.concatenate([y_p, y_s], axis=0), t_pad)
        o_all = _pad_rows(jnp.concatenate([o_p, o_s], axis=0), t_pad)
        h = _out_proj(y_all, o_all, w_out_b, l, h)

        q_peer, c = _norm_proj(h, g_ffn[l], wq_b, l, with_xn=True)
        e1, n1, e2, r2 = _route(q_peer, k1_b, k2_b, l)
        h = _peer(c, u_b, vt_b, l, e1, n1, e2, r2, h)

        h = _ple(h, g_ple[l], p_all, wg_b, wp_b, l)

        xbc0 = 2 * D_ATT + D_ATT + D_SSM
        proj_p = proj[:n_p].reshape(nbp, seq, MAIN_DIM)
        outs[0].append(kf[:n_p].reshape(nbp, seq, H_A, 2, DH_QK))
        outs[1].append(proj_p[:, :, 2 * D_ATT:3 * D_ATT].reshape(nbp, seq, H_A, DH_V))
        outs[2].append(hT_p.reshape(nbp, H_S, SSM_P, SSM_N))
        outs[3].append(proj_p[:, seq - (CONV_W - 1):, xbc0:])
        outs[4].append(kf[n_p:n_p + n_s].reshape(nbs, dseq, H_A, 2, DH_QK))
        outs[5].append(proj_s[:, :, 2 * D_ATT:3 * D_ATT].reshape(nbs, dseq, H_A, DH_V))
        outs[6].append(hT_s.reshape(nbs, H_S, SSM_P, SSM_N))
        outs[7].append(proj_s[:, dseq - (CONV_W - 1):, xbc0:])

    y_prompt = h[:n_p].reshape(nbp, seq, d)
    y_sample = h[n_p:n_p + n_s].reshape(nbs, dseq, d)
    return (y_prompt, y_sample) + tuple(jnp.stack(o) for o in outs)
```

```python
import functools
import math

import jax
import jax.numpy as jnp
from jax import lax
from jax.experimental import pallas as pl
from jax.experimental.pallas import tpu as pltpu

F32 = jnp.float32
BF16 = jnp.bfloat16

EPS = 1e-6
CHUNK = 64
CHUNK_SHIFT = 6
H_A = 8
DH_QK = 64
DH_V = 128
D_ATT = H_A * DH_V
ATT_SCALE = DH_QK ** -0.5
SSM_P = 64
H_S = 16
SSM_N = 128
SSM_G = 2
D_SSM = H_S * SSM_P
CONV_W = 4
CONV_DIM = D_SSM + 2 * SSM_G * SSM_N
MAIN_DIM = 2 * D_ATT + D_ATT + D_SSM + CONV_DIM
N_KEYS = 128
PEER_HEADS = 8
PEER_TOPK = 16
TOPK_SHIFT = 4

LANE = 128
TM = 512
MM_ROWS = 1152
SSD_Q = 128
VMEM_LIMIT = 56 * 1024 * 1024

_NT = (((1,), (1,)), ((), ()))
_HI = lax.Precision.HIGHEST


def _cparams(sem):
    return pltpu.CompilerParams(dimension_semantics=sem, vmem_limit_bytes=VMEM_LIMIT)


def _row_tile(t):
    for k in range(1, t // 16 + 1):
        if t % k == 0 and (t // k) % 16 == 0 and t // k <= MM_ROWS:
            return t // k
    return TM


def _rms(x, g):
    ms = jnp.mean(x * x, axis=-1, keepdims=True)
    return x * lax.rsqrt(ms + EPS) * g


def _norm_proj_kernel(x_ref, g_ref, w_ref, *rest, with_dt, with_xn):
    rest = list(rest)
    wdt_ref = rest.pop(0) if with_dt else None
    o_ref = rest.pop(0)
    odt_ref = rest.pop(0) if with_dt else None
    oxn_ref = rest.pop(0) if with_xn else None
    xn_ref = rest.pop(0)

    @pl.when(pl.program_id(1) == 0)
    def _():
        xn = _rms(x_ref[...], g_ref[...]).astype(BF16)
        xn_ref[...] = xn
        if with_xn:
            oxn_ref[...] = xn
        if with_dt:
            odt_ref[...] = jnp.dot(xn, wdt_ref[...], preferred_element_type=F32)

    o_ref[...] = jnp.dot(xn_ref[...], w_ref[...], preferred_element_type=F32)


def _norm_proj(x, g, w, layer, wdt=None, with_xn=False, tn=512):
    t, d = x.shape
    n = w.shape[2]
    tm = _row_tile(t)
    with_dt = wdt is not None
    in_specs = [
        pl.BlockSpec((tm, d), lambda i, j: (i, 0)),
        pl.BlockSpec((1, d), lambda i, j: (0, 0)),
        pl.BlockSpec((None, d, tn), lambda i, j: (layer, 0, j)),
    ]
    args = [x, g.reshape(1, d), w]
    out_shape = [jax.ShapeDtypeStruct((t, n), F32)]
    out_specs = [pl.BlockSpec((tm, tn), lambda i, j: (i, j))]
    if with_dt:
        in_specs.append(pl.BlockSpec((None, d, LANE), lambda i, j: (layer, 0, 0)))
        args.append(wdt)
        out_shape.append(jax.ShapeDtypeStruct((t, LANE), F32))
        out_specs.append(pl.BlockSpec((tm, LANE), lambda i, j: (i, 0)))
    if with_xn:
        out_shape.append(jax.ShapeDtypeStruct((t, d), BF16))
        out_specs.append(pl.BlockSpec((tm, d), lambda i, j: (i, 0)))
    return pl.pallas_call(
        functools.partial(_norm_proj_kernel, with_dt=with_dt, with_xn=with_xn),
        grid=(t // tm, n // tn),
        in_specs=in_specs,
        out_specs=out_specs,
        out_shape=out_shape,
        scratch_shapes=[pltpu.VMEM((tm, d), BF16)],
        compiler_params=_cparams(("parallel", "arbitrary")),
        name="norm_proj",
    )(*args)


def _qk_norm_kernel(q_ref, k_ref, v_ref, qg_ref, kg_ref, kp_in, vp_in, ks_in, vs_in,
                    qn_ref, kb_ref, vb_ref, kp_ref, vp_ref, ks_ref, vs_ref, kf_ref, *, np_tiles, n_s):
    del kp_in, vp_in, ks_in, vs_in
    i = pl.program_id(0)
    lane = lax.broadcasted_iota(jnp.int32, (1, DH_V), 1)
    lo = lane < DH_QK

    def norm(x, g):
        xx = x * x
        s_lo = jnp.sum(jnp.where(lo, xx, 0.0), axis=-1, keepdims=True)
        s_all = jnp.sum(xx, axis=-1, keepdims=True)
        ms = jnp.where(lo, s_lo, s_all - s_lo) * (1.0 / DH_QK)
        return x * lax.rsqrt(ms + EPS) * g

    for h in range(H_A):
        sl = slice(h * DH_V, (h + 1) * DH_V)
        qn_ref[:, sl] = (norm(q_ref[:, sl], qg_ref[...]) * ATT_SCALE).astype(BF16)
        kn = norm(k_ref[:, sl], kg_ref[...])
        kf_ref[:, sl] = kn
        kb_ref[:, sl] = kn.astype(BF16)
    vb_ref[...] = v_ref[...].astype(BF16)

    @pl.when(i < np_tiles)
    def _():
        kp_ref[...] = kf_ref[...]
        vp_ref[...] = v_ref[...]

    @pl.when(i == np_tiles)
    def _():
        ks_ref[...] = kf_ref[0:n_s, :]
        vs_ref[...] = v_ref[0:n_s, :]


def _qk_norm(proj, q_gain, k_gain, layer, kp, vp, ks, vs):
    t = proj.shape[0]
    n_p, n_s = kp.shape[1], ks.shape[1]
    assert n_p % TM == 0 and n_s <= TM and t >= n_p + TM
    np_tiles = n_p // TM
    blk = lambda c: pl.BlockSpec((TM, D_ATT), lambda i, c=c: (i, c))
    gspec = pl.BlockSpec((1, DH_V), lambda i: (0, 0))
    hbm = pl.BlockSpec(memory_space=pl.ANY)
    p_out = pl.BlockSpec((None, TM, D_ATT), lambda i: (layer, jnp.minimum(i, np_tiles - 1), 0))
    s_out = pl.BlockSpec((None, n_s, D_ATT), lambda i: (layer, 0, 0))
    return pl.pallas_call(
        functools.partial(_qk_norm_kernel, np_tiles=np_tiles, n_s=n_s),
        grid=(t // TM,),
        in_specs=[blk(0), blk(1), blk(2), gspec, gspec, hbm, hbm, hbm, hbm],
        out_specs=[blk(0), blk(0), blk(0), p_out, p_out, s_out, s_out],
        out_shape=[
            jax.ShapeDtypeStruct((t, D_ATT), BF16),
            jax.ShapeDtypeStruct((t, D_ATT), BF16),
            jax.ShapeDtypeStruct((t, D_ATT), BF16),
            jax.ShapeDtypeStruct(kp.shape, F32),
            jax.ShapeDtypeStruct(vp.shape, F32),
            jax.ShapeDtypeStruct(ks.shape, F32),
            jax.ShapeDtypeStruct(vs.shape, F32),
        ],
        scratch_shapes=[pltpu.VMEM((TM, D_ATT), F32)],
        input_output_aliases={5: 3, 6: 4, 7: 5, 8: 6},
        compiler_params=_cparams(("arbitrary",)),
        name="qk_norm",
    )(proj, proj, proj, q_gain.reshape(1, DH_V), k_gain.reshape(1, DH_V), kp, vp, ks, vs)


def _sub_norm(o, lam_ref, gsub):
    return _rms(o, gsub) * lam_ref[1]


def _attn_prompt_kernel(lam_ref, q_ref, k_ref, v_ref, gsub_ref, o_ref, *, tq):
    qi = pl.program_id(2)
    q = q_ref[...]
    lane = lax.broadcasted_iota(jnp.int32, (1, DH_V), 1)
    zero = jnp.zeros((), BF16)
    q0 = jnp.where(lane < DH_QK, q, zero)
    q1 = jnp.where(lane >= DH_QK, q, zero)

    def scores(kb):
        s0 = lax.dot_general(q0, kb, _NT, preferred_element_type=F32)
        s1 = lax.dot_general(q1, kb, _NT, preferred_element_type=F32)
        return s0, s1

    start = pl.multiple_of(qi * tq, tq)
    kb = k_ref[pl.ds(start, tq), :]
    vb = v_ref[pl.ds(start, tq), :]
    row = jnp.right_shift(lax.broadcasted_iota(jnp.int32, (tq, tq), 0), CHUNK_SHIFT)
    col = jnp.right_shift(lax.broadcasted_iota(jnp.int32, (tq, tq), 1), CHUNK_SHIFT)
    mask = col <= row
    s0, s1 = scores(kb)
    s0 = jnp.where(mask, s0, -jnp.inf)
    s1 = jnp.where(mask, s1, -jnp.inf)
    m0 = jnp.max(s0, axis=-1, keepdims=True)
    m1 = jnp.max(s1, axis=-1, keepdims=True)
    p0 = jnp.exp(s0 - m0)
    p1 = jnp.exp(s1 - m1)
    l0 = jnp.sum(p0, axis=-1, keepdims=True)
    l1 = jnp.sum(p1, axis=-1, keepdims=True)
    a0 = jnp.dot(p0.astype(BF16), vb, preferred_element_type=F32)
    a1 = jnp.dot(p1.astype(BF16), vb, preferred_element_type=F32)

    def body(j, carry):
        m0, l0, a0, m1, l1, a1 = carry
        st = pl.multiple_of(j * tq, tq)
        kb = k_ref[pl.ds(st, tq), :]
        vb = v_ref[pl.ds(st, tq), :]
        s0, s1 = scores(kb)

        def upd(s, m, l, a):
            mn = jnp.maximum(m, jnp.max(s, axis=-1, keepdims=True))
            alpha = jnp.exp(m - mn)
            p = jnp.exp(s - mn)
            l = alpha * l + jnp.sum(p, axis=-1, keepdims=True)
            a = alpha * a + jnp.dot(p.astype(BF16), vb, preferred_element_type=F32)
            return mn, l, a

        m0, l0, a0 = upd(s0, m0, l0, a0)
        m1, l1, a1 = upd(s1, m1, l1, a1)
        return m0, l0, a0, m1, l1, a1

    m0, l0, a0, m1, l1, a1 = lax.fori_loop(0, qi, body, (m0, l0, a0, m1, l1, a1))
    o = a0 / l0 - lam_ref[0] * (a1 / l1)
    o_ref[...] = _sub_norm(o, lam_ref, gsub_ref[...]).astype(BF16)


def _attn_prompt(lam, qn, kb, vb, g_sub, nb, seq):
    tq = min(512, seq)
    nq = seq // tq
    return pl.pallas_call(
        functools.partial(_attn_prompt_kernel, tq=tq),
        grid=(nb, H_A, nq),
        in_specs=[
            pl.BlockSpec(memory_space=pltpu.SMEM),
            pl.BlockSpec((tq, DH_V), lambda b, h, i: (b * nq + i, h)),
            pl.BlockSpec((seq, DH_V), lambda b, h, i: (b, h)),
            pl.BlockSpec((seq, DH_V), lambda b, h, i: (b, h)),
            pl.BlockSpec((1, DH_V), lambda b, h, i: (0, 0)),
        ],
        out_specs=pl.BlockSpec((tq, DH_V), lambda b, h, i: (b * nq + i, h)),
        out_shape=jax.ShapeDtypeStruct((nb * seq, D_ATT), BF16),
        compiler_params=_cparams(("parallel", "parallel", "arbitrary")),
        name="attn_prompt",
    )(lam, qn, kb, vb, g_sub.reshape(1, DH_V))


def _attn_sample_kernel(lam_ref, q_ref, kn_ref, vn_ref, ck_ref, cv_ref, gsub_ref, o_ref):
    q = q_ref[...]
    nq = q.shape[0]
    lane = lax.broadcasted_iota(jnp.int32, (1, DH_V), 1)
    zero = jnp.zeros((), BF16)
    q01 = jnp.concatenate([jnp.where(lane < DH_QK, q, zero), jnp.where(lane >= DH_QK, q, zero)], axis=0)
    sc = lax.dot_general(q01, ck_ref[...].astype(BF16), _NT, preferred_element_type=F32)
    sn = lax.dot_general(q01, kn_ref[...], _NT, preferred_element_type=F32)
    m = jnp.maximum(jnp.max(sc, axis=-1, keepdims=True), jnp.max(sn, axis=-1, keepdims=True))
    pc = jnp.exp(sc - m)
    pn = jnp.exp(sn - m)
    inv = 1.0 / (jnp.sum(pc, axis=-1, keepdims=True) + jnp.sum(pn, axis=-1, keepdims=True))
    pc = pc * inv
    pn = pn * inv
    lam = lam_ref[0]
    wc = (pc[:nq] - lam * pc[nq:]).astype(BF16)
    wn = (pn[:nq] - lam * pn[nq:]).astype(BF16)
    o = jnp.dot(wc, cv_ref[...].astype(BF16), preferred_element_type=F32)
    o = o + jnp.dot(wn, vn_ref[...], preferred_element_type=F32)
    o_ref[...] = _sub_norm(o, lam_ref, gsub_ref[...]).astype(BF16)


def _attn_sample(lam, qn, kb, vb, cache_k, cache_v, layer, g_sub, row0, nb, nq):
    past = cache_k.shape[2]
    r0 = row0 // nq
    new = lambda: pl.BlockSpec((nq, DH_V), lambda b, h: (r0 + b, h))
    cache = lambda: pl.BlockSpec((None, None, past, DH_V), lambda b, h: (layer, b, 0, h))
    return pl.pallas_call(
        _attn_sample_kernel,
        grid=(nb, H_A),
        in_specs=[pl.BlockSpec(memory_space=pltpu.SMEM), new(), new(), new(), cache(), cache(),
                  pl.BlockSpec((1, DH_V), lambda b, h: (0, 0))],
        out_specs=pl.BlockSpec((nq, DH_V), lambda b, h: (b, h)),
        out_shape=jax.ShapeDtypeStruct((nb * nq, D_ATT), BF16),
        compiler_params=_cparams(("parallel", "parallel")),
        name="attn_sample",
    )(lam, qn, kb, vb, cache_k, cache_v, g_sub.reshape(1, DH_V))


def _ssd_kernel(xs_ref, b_ref, c_ref, z_ref, dt_ref, prev_ref, h0_ref, cw_ref, cb_ref, dtb_ref,
                aneg_ref, dexp_ref, gssd_ref, expand_ref, y_ref, hout_ref, ext_ref, ht_ref, *, valid):
    q = SSD_Q
    c = pl.program_id(1)
    halo = 8

    @pl.when(c == 0)
    def _():
        ext_ref[0:halo, :] = prev_ref[...]
        ht_ref[...] = h0_ref[...].T

    ext_ref[halo:halo + q, 0:D_SSM] = xs_ref[...]
    ext_ref[halo:halo + q, D_SSM:D_SSM + SSM_G * SSM_N] = b_ref[...]
    ext_ref[halo:halo + q, D_SSM + SSM_G * SSM_N:CONV_DIM] = c_ref[...]
    conv = cb_ref[...]
    for j in range(CONV_W):
        r0 = halo - (CONV_W - 1) + j
        conv = conv + ext_ref[r0:r0 + q, :] * cw_ref[j:j + 1, :]
    tail = ext_ref[q:q + halo, :]
    ext_ref[0:halo, :] = tail
    xbc = conv * jax.nn.sigmoid(conv)
    xs = xbc[:, 0:D_SSM]
    bm = xbc[:, D_SSM:D_SSM + SSM_G * SSM_N]
    cm = xbc[:, D_SSM + SSM_G * SSM_N:CONV_DIM]

    dt = jax.nn.softplus(dt_ref[...] + dtb_ref[...])
    if valid < q:
        rowv = lax.broadcasted_iota(jnp.int32, (q, LANE), 0) < valid
        dt = jnp.where(rowv, dt, 0.0)
    a = dt * aneg_ref[...]
    r_i = lax.broadcasted_iota(jnp.int32, (q, q), 0)
    c_i = lax.broadcasted_iota(jnp.int32, (q, q), 1)
    tril = c_i <= r_i
    acum = jnp.dot(tril.astype(F32), a, precision=_HI, preferred_element_type=F32)
    eye = (r_i == c_i).astype(F32)
    acum_t = lax.dot_general(eye, acum, _NT, precision=_HI, preferred_element_type=F32)
    a_last = acum[q - 1:q, :]
    decay_in = jnp.exp(a_last - acum)
    stack = jnp.concatenate(
        [dt, dt * decay_in, jnp.exp(acum), jnp.broadcast_to(jnp.exp(a_last), (8, LANE))], axis=0)
    ex = jnp.dot(stack, expand_ref[...], precision=_HI, preferred_element_type=F32)
    dt_e = ex[0:q]
    dtd_e = ex[q:2 * q]
    eacum_e = ex[2 * q:3 * q]
    cd_e = ex[3 * q:3 * q + 1]

    xdt = (xs * dt_e).astype(BF16)
    xdtd = (xs * dtd_e).astype(BF16)
    lane = lax.broadcasted_iota(jnp.int32, (1, LANE), 1)
    zero = jnp.zeros((), BF16)
    gw = D_SSM // SSM_G
    hpg = H_S // SSM_G
    y_diag_parts = []
    y_off_parts = []
    for g in range(SSM_G):
        gs = slice(g * gw, (g + 1) * gw)
        bg = bm[:, g * SSM_N:(g + 1) * SSM_N]
        cg = cm[:, g * SSM_N:(g + 1) * SSM_N].astype(BF16)
        cb = lax.dot_general(cg, bg.astype(BF16), _NT, preferred_element_type=F32)
        ht_prev = ht_ref[:, gs]
        y_off = jnp.dot(cg, ht_prev.astype(BF16), preferred_element_type=F32)
        ht_ref[:, gs] = cd_e[:, gs] * ht_prev + jnp.dot(
            bg.T.astype(BF16), xdtd[:, gs], preferred_element_type=F32)
        for k in range(hpg // 2):
            ms = []
            for h in (g * hpg + 2 * k, g * hpg + 2 * k + 1):
                seg = acum[:, h:h + 1] - acum_t[h:h + 1, :]
                ms.append((cb * jnp.exp(jnp.where(tril, seg, -jnp.inf))).astype(BF16))
            pair = slice((g * hpg + 2 * k) * SSM_P, (g * hpg + 2 * k + 2) * SSM_P)
            xp = xdt[:, pair]
            rhs = jnp.concatenate([jnp.where(lane < SSM_P, xp, zero), jnp.where(lane >= SSM_P, xp, zero)], axis=0)
            y_diag_parts.append(jnp.dot(jnp.concatenate(ms, axis=1), rhs, preferred_element_type=F32))
        y_off_parts.append(y_off)
    y_diag = jnp.concatenate(y_diag_parts, axis=1)
    y_off = jnp.concatenate(y_off_parts, axis=1)
    y = y_diag + y_off * eacum_e + dexp_ref[...] * xs
    zz = z_ref[...]
    y = y * (zz * jax.nn.sigmoid(zz))
    for g in range(SSM_G):
        gs = slice(g * gw, (g + 1) * gw)
        y_ref[:, gs] = _rms(y[:, gs], gssd_ref[:, gs]).astype(BF16)

    @pl.when(c == pl.num_programs(1) - 1)
    def _():
        hout_ref[...] = ht_ref[...].T


def _ssd(proj, dtp, prev, h0, lw, nb, seq, valid):
    nc = seq // SSD_Q
    row = lambda w, col: pl.BlockSpec((SSD_Q, w), lambda b, c, col=col: (b * nc + c, col))
    const = lambda shp: pl.BlockSpec(shp, lambda b, c: (0,) * len(shp))
    per_b = lambda shp: pl.BlockSpec((None,) + shp, lambda b, c: (b, 0, 0))
    return pl.pallas_call(
        functools.partial(_ssd_kernel, valid=valid),
        grid=(nb, nc),
        in_specs=[
            row(D_SSM, 4), row(SSM_G * SSM_N, 20), row(SSM_G * SSM_N, 21), row(D_SSM, 3), row(LANE, 0),
            per_b((8, CONV_DIM)), per_b((D_SSM, SSM_N)),
            const((CONV_W, CONV_DIM)), const((1, CONV_DIM)), const((1, LANE)), const((1, LANE)),
            const((1, D_SSM)), const((1, D_SSM)), const((LANE, D_SSM)),
        ],
        out_specs=[row(D_SSM, 0), per_b((D_SSM, SSM_N))],
        out_shape=[jax.ShapeDtypeStruct((nb * seq, D_SSM), BF16),
                   jax.ShapeDtypeStruct((nb, D_SSM, SSM_N), F32)],
        scratch_shapes=[pltpu.VMEM((SSD_Q + 8, CONV_DIM), F32), pltpu.VMEM((SSM_N, D_SSM), F32)],
        compiler_params=_cparams(("parallel", "arbitrary")),
        name="ssd",
    )(proj, proj, proj, proj, dtp, prev, h0, lw["conv_w"], lw["conv_b"], lw["dt_bias"], lw["a_neg"],
      lw["d_exp"], lw["g_ssd"], lw["expand"])


def _out_proj_kernel(y_ref, o_ref, w_ref, h_ref, out_ref):
    acc = jnp.dot(y_ref[...], w_ref[0:D_SSM, :], preferred_element_type=F32)
    acc = acc + jnp.dot(o_ref[...], w_ref[D_SSM:D_SSM + D_ATT, :], preferred_element_type=F32)
    out_ref[...] = h_ref[...] + acc


def _out_proj(y, o, w, layer, h, tn=512):
    t, d = h.shape
    tm = _row_tile(t)
    return pl.pallas_call(
        _out_proj_kernel,
        grid=(t // tm, d // tn),
        in_specs=[
            pl.BlockSpec((tm, D_SSM), lambda i, j: (i, 0)),
            pl.BlockSpec((tm, D_ATT), lambda i, j: (i, 0)),
            pl.BlockSpec((None, D_SSM + D_ATT, tn), lambda i, j: (layer, 0, j)),
            pl.BlockSpec((tm, tn), lambda i, j: (i, j)),
        ],
        out_specs=pl.BlockSpec((tm, tn), lambda i, j: (i, j)),
        out_shape=jax.ShapeDtypeStruct((t, d), F32),
        compiler_params=_cparams(("parallel", "parallel")),
        name="out_proj",
    )(y, o, w, h)


def _top16(s, n_out=None):
    n, t = s.shape
    iota = lax.broadcasted_iota(jnp.int32, (n, t), 0)
    row16 = lax.broadcasted_iota(jnp.int32, (PEER_TOPK, t), 0)
    rank = jnp.full((n, t), PEER_TOPK, jnp.int32)
    vals = jnp.zeros((PEER_TOPK, t), F32)
    cnt = jnp.zeros((n_out, t), jnp.int32) if n_out else None
    for r in range(PEER_TOPK):
        m = jnp.max(s, axis=0, keepdims=True)
        idx = jnp.min(jnp.where(s == m, iota, n), axis=0, keepdims=True)
        sel = iota == idx
        rank = jnp.where(sel, r, rank)
        s = jnp.where(sel, -jnp.inf, s)
        vals = jnp.where(row16 == r, m, vals)
        if n_out:
            cnt = cnt + (row16 == jnp.right_shift(idx, TOPK_SHIFT)).astype(jnp.int32)
    return vals, rank, cnt


_MARK = 2.0 ** 126


def _peel16(s):
    t = s.shape[1]
    row16 = lax.broadcasted_iota(jnp.int32, (PEER_TOPK, t), 0)
    vals = jnp.zeros((PEER_TOPK, t), F32)
    for r in range(PEER_TOPK):
        m = jnp.max(s, axis=0, keepdims=True)
        s = jnp.where(s == m, -(1.0 + (r + 1) / 32.0) * _MARK, s)
        vals = jnp.where(row16 == r, m, vals)
    marked = s < -_MARK
    rank = jnp.where(marked, s * (-32.0 / _MARK) - 33.0, float(PEER_TOPK))
    count = jnp.sum(jnp.where(marked, 1.0, 0.0), axis=0, keepdims=True)
    return vals, rank, count


def _route_outputs(s1, s2, v1, v2, rank1, rank2, cnt, zsum):
    n1 = jnp.zeros(s1.shape, F32)
    for j in range(PEER_TOPK):
        n1 = jnp.where(rank1 == j, cnt[j:j + 1, :], n1)
    e1 = jnp.where(rank1 < PEER_TOPK, jnp.exp(s1 - v1[0:1, :]) / zsum, 0.0)
    e2 = jnp.where(rank2 < PEER_TOPK, jnp.exp(s2 - v2[0:1, :]), 0.0)
    return e1, n1, e2.astype(BF16), rank2.astype(BF16)


def _route_exact(s1, s2):
    v1, rank1, _ = _top16(s1)
    v2, rank2, _ = _top16(s2)
    cand = jnp.concatenate([v1[j:j + 1, :] + v2 for j in range(PEER_TOPK)], axis=0)
    sc, _, cnt = _top16(cand, n_out=PEER_TOPK)
    zsum = jnp.sum(jnp.exp(sc - sc[0:1, :]), axis=0, keepdims=True)
    return _route_outputs(s1, s2, v1, v2, rank1.astype(F32), rank2.astype(F32), cnt.astype(F32), zsum)


def _route_fast(s1, s2):
    half = PEER_TOPK // 2
    t = s1.shape[1]
    v1, rank1, c1 = _peel16(s1)
    v2, rank2, c2 = _peel16(s2)
    pieces = [v1[j:j + 1, :] + v2[0:half, :] for j in range(half)]
    pieces.append(v1[half:, :] + v2[0:1, :])
    pieces.append(v1[0:1, :] + v2[half:, :])
    cand = jnp.concatenate(pieces, axis=0)
    sc, rankc, cc = _peel16(cand)
    sel = rankc < PEER_TOPK
    zsum = jnp.sum(jnp.where(sel, jnp.exp(cand - sc[0:1, :]), 0.0), axis=0, keepdims=True)
    mk = jnp.where(sel, 1.0, 0.0)
    row8 = lax.broadcasted_iota(jnp.int32, (half, t), 0)
    n_lo = jnp.zeros((half, t), F32)
    for j in range(half):
        rs = jnp.sum(mk[j * half:(j + 1) * half, :], axis=0, keepdims=True)
        if j == 0:
            rs = rs + jnp.sum(mk[(half + 1) * half:, :], axis=0, keepdims=True)
        n_lo = jnp.where(row8 == j, rs, n_lo)
    cnt = jnp.concatenate([n_lo, mk[half * half:(half + 1) * half, :]], axis=0)
    bad = jnp.abs(c1 - PEER_TOPK) + jnp.abs(c2 - PEER_TOPK) + jnp.abs(cc - PEER_TOPK)
    return _route_outputs(s1, s2, v1, v2, rank1, rank2, cnt, zsum), bad


def _route_kernel(q_ref, k1_ref, k2_ref, e1_ref, n1_ref, e2_ref, r2_ref, *, tm):
    def store(ls, outs):
        e1_ref[:, ls], n1_ref[:, ls], e2_ref[:, ls], r2_ref[:, ls] = outs

    for lt in range(tm // LANE):
        ls = slice(lt * LANE, (lt + 1) * LANE)
        qb = q_ref[ls, :].astype(BF16)
        s1 = lax.dot_general(k1_ref[...], qb[:, 0:N_KEYS], _NT, preferred_element_type=F32)
        s2 = lax.dot_general(k2_ref[...], qb[:, N_KEYS:2 * N_KEYS], _NT, preferred_element_type=F32)
        outs, bad = _route_fast(s1, s2)
        store(ls, outs)

        @pl.when(jnp.max(bad) > 0.0)
        def _(ls=ls, s1=s1, s2=s2):
            store(ls, _route_exact(s1, s2))


def _route(q, k1, k2, layer, tm=TM):
    t = q.shape[0]
    kd = k1.shape[-1]
    out = lambda: pl.BlockSpec((None, N_KEYS, tm), lambda i, h: (h, 0, i))
    key = lambda: pl.BlockSpec((None, None, N_KEYS, kd), lambda i, h: (layer, h, 0, 0))
    return pl.pallas_call(
        functools.partial(_route_kernel, tm=tm),
        grid=(t // tm, PEER_HEADS),
        in_specs=[pl.BlockSpec((tm, 2 * kd), lambda i, h: (i, h)), key(), key()],
        out_specs=[out(), out(), out(), out()],
        out_shape=[jax.ShapeDtypeStruct((PEER_HEADS, N_KEYS, t), F32),
                   jax.ShapeDtypeStruct((PEER_HEADS, N_KEYS, t), F32),
                   jax.ShapeDtypeStruct((PEER_HEADS, N_KEYS, t), BF16),
                   jax.ShapeDtypeStruct((PEER_HEADS, N_KEYS, t), BF16)],
        compiler_params=_cparams(("parallel", "parallel")),
        name="peer_route",
    )(q, k1, k2)


def _peer_kernel(x_ref, u_ref, vt_ref, e1_ref, n1_ref, e2_ref, r2_ref, h_ref, out_ref, acc_ref, *, te):
    j = pl.program_id(1)

    @pl.when(j == 0)
    def _():
        acc_ref[...] = jnp.zeros_like(acc_ref)

    at = lax.dot_general(u_ref[...], x_ref[...], _NT, preferred_element_type=F32)
    act = (0.5 * at * (1.0 + lax.erf(at * (2.0 ** -0.5)))).astype(BF16)
    zero = jnp.zeros((), BF16)
    parts = []
    for c in range(te // N_KEYS):
        w = None
        for h in range(PEER_HEADS):
            n1 = n1_ref[h, c:c + 1, :].astype(BF16)
            e1 = e1_ref[h, c:c + 1, :].astype(BF16)
            wh = jnp.where(r2_ref[h] < n1, e1 * e2_ref[h], zero)
            w = wh if w is None else w + wh
        parts.append(act[c * N_KEYS:(c + 1) * N_KEYS, :] * w)
    ht = jnp.concatenate(parts, axis=0)
    acc_ref[...] += jnp.dot(vt_ref[...], ht, preferred_element_type=F32)

    @pl.when(j == pl.num_programs(1) - 1)
    def _():
        out_ref[...] = h_ref[...] + acc_ref[...].T


def _peer(x, u, vt, layer, e1, n1, e2, r2, h, te=1024):
    t, d = h.shape
    ne = u.shape[1]
    rows = te // N_KEYS
    return pl.pallas_call(
        functools.partial(_peer_kernel, te=te),
        grid=(t // TM, ne // te),
        in_specs=[
            pl.BlockSpec((TM, d), lambda i, j: (i, 0)),
            pl.BlockSpec((None, te, d), lambda i, j: (layer, j, 0)),
            pl.BlockSpec((None, d, te), lambda i, j: (layer, 0, j)),
            pl.BlockSpec((PEER_HEADS, rows, TM), lambda i, j: (0, j, i)),
            pl.BlockSpec((PEER_HEADS, rows, TM), lambda i, j: (0, j, i)),
            pl.BlockSpec((PEER_HEADS, N_KEYS, TM), lambda i, j: (0, 0, i)),
            pl.BlockSpec((PEER_HEADS, N_KEYS, TM), lambda i, j: (0, 0, i)),
            pl.BlockSpec((TM, d), lambda i, j: (i, 0)),
        ],
        out_specs=pl.BlockSpec((TM, d), lambda i, j: (i, 0)),
        out_shape=jax.ShapeDtypeStruct((t, d), F32),
        scratch_shapes=[pltpu.VMEM((d, TM), F32)],
        compiler_params=_cparams(("parallel", "arbitrary")),
        name="peer_experts",
    )(x, u, vt, e1, n1, e2, r2, h)


def _ple_kernel(h_ref, g_ref, p_ref, wg_ref, wp_ref, hc_ref, out_ref, xn_ref):
    @pl.when(pl.program_id(1) == 0)
    def _():
        xn_ref[...] = _rms(h_ref[...], g_ref[...]).astype(BF16)

    gate = jax.nn.sigmoid(jnp.dot(xn_ref[...], wg_ref[...], preferred_element_type=F32))
    emb = jnp.dot(p_ref[...].astype(BF16), wp_ref[...], preferred_element_type=F32)
    out_ref[...] = hc_ref[...] + emb * gate


def _ple(h, g, p, wg, wp, layer, tn=512):
    t, d = h.shape
    pd = p.shape[2]
    tm = _row_tile(t)
    return pl.pallas_call(
        _ple_kernel,
        grid=(t // tm, d // tn),
        in_specs=[
            pl.BlockSpec((tm, d), lambda i, j: (i, 0)),
            pl.BlockSpec((1, d), lambda i, j: (0, 0)),
            pl.BlockSpec((None, tm, pd), lambda i, j: (layer, i, 0)),
            pl.BlockSpec((None, d, tn), lambda i, j: (layer, 0, j)),
            pl.BlockSpec((None, pd, tn), lambda i, j: (layer, 0, j)),
            pl.BlockSpec((tm, tn), lambda i, j: (i, j)),
        ],
        out_specs=pl.BlockSpec((tm, tn), lambda i, j: (i, j)),
        out_shape=jax.ShapeDtypeStruct((t, d), F32),
        scratch_shapes=[pltpu.VMEM((tm, d), BF16)],
        compiler_params=_cparams(("parallel", "arbitrary")),
        name="ple_gate",
    )(h, g.reshape(1, d), p, wg, wp, h)


def _lambda_init(layer_idx):
    return 0.8 - 0.6 * math.exp(-0.3 * layer_idx)


def _pad_rows(x, rows):
    return jnp.pad(x, ((0, rows - x.shape[0]),) + ((0, 0),) * (x.ndim - 1))


def kernel(x_prompt, x_sample, cache_k, cache_v, state_ssm, state_conv, p_prompt, p_sample, g_mix, w_in,
           conv_w, conv_b, dt_bias, a_log, d_skip, g_ssd, q_gain, k_gain, lam_q1, lam_k1, lam_q2, lam_k2,
           g_sub, w_out, g_ffn, peer_wq, peer_k1, peer_k2, peer_u, peer_v, g_ple, w_ple, w_pgate):
    nbp, seq, d = x_prompt.shape
    nbs, dseq, _ = x_sample.shape
    depth = w_in.shape[0]
    past = cache_k.shape[2]
    n_p = nbp * seq
    n_s = nbs * dseq
    t_pad = -(-(n_p + n_s) // TM) * TM

    h = _pad_rows(jnp.concatenate([x_prompt.reshape(n_p, d), x_sample.reshape(n_s, d)], axis=0), t_pad)
    ple_dim = p_prompt.shape[-1]
    p_all = jnp.concatenate([p_prompt.reshape(depth, n_p, ple_dim), p_sample.reshape(depth, n_s, ple_dim)], axis=1)
    p_all = jnp.pad(p_all, ((0, 0), (0, t_pad - n_p - n_s), (0, 0)))
    ck = cache_k.reshape(depth, nbs, past, D_ATT)
    cv = cache_v.reshape(depth, nbs, past, D_ATT)
    expand = (jnp.arange(LANE)[:, None] == jnp.arange(D_SSM)[None, :] // SSM_P).astype(F32)
    prev_zero = jnp.zeros((nbp, 8, CONV_DIM), F32)
    h0_zero = jnp.zeros((nbp, D_SSM, SSM_N), F32)

    w_main = w_in[:, :, :MAIN_DIM].astype(BF16)
    w_dt = jnp.pad(w_in[:, :, MAIN_DIM:], ((0, 0), (0, 0), (0, LANE - H_S))).astype(BF16)
    w_out_b = w_out.astype(BF16)
    wq_b = peer_wq.astype(BF16)
    k1_b = peer_k1.astype(BF16)
    k2_b = peer_k2.astype(BF16)
    u_b = peer_u.astype(BF16)
    vt_b = jnp.swapaxes(peer_v, 1, 2).astype(BF16)
    wg_b = w_pgate.astype(BF16)
    wp_b = w_ple.astype(BF16)

    kp_all = jnp.zeros((depth, n_p, D_ATT), F32)
    vp_all = jnp.zeros((depth, n_p, D_ATT), F32)
    ks_all = jnp.zeros((depth, n_s, D_ATT), F32)
    vs_all = jnp.zeros((depth, n_s, D_ATT), F32)

    outs = [[] for _ in range(4)]
    for l in range(depth):
        lam0 = _lambda_init(l)
        lam = (jnp.exp(jnp.sum(lam_q1[l] * lam_k1[l])) - jnp.exp(jnp.sum(lam_q2[l] * lam_k2[l])) + lam0)
        lam_s = jnp.stack([lam, jnp.asarray(1.0 - lam0, F32)]).astype(F32)
        lw = dict(
            conv_w=conv_w[l], conv_b=conv_b[l].reshape(1, CONV_DIM),
            dt_bias=_pad_rows(dt_bias[l], LANE).reshape(1, LANE),
            a_neg=_pad_rows(-jnp.exp(a_log[l]), LANE).reshape(1, LANE),
            d_exp=jnp.repeat(d_skip[l], SSM_P).reshape(1, D_SSM),
            g_ssd=g_ssd[l].reshape(1, D_SSM), expand=expand)

        proj, dtp = _norm_proj(h, g_mix[l], w_main, l, wdt=w_dt)
        qn, kb, vb, kp_all, vp_all, ks_all, vs_all = _qk_norm(
            proj, q_gain[l], k_gain[l], l, kp_all, vp_all, ks_all, vs_all)
        o_p = _attn_prompt(lam_s, qn, kb, vb, g_sub[l], nbp, seq)
        o_s = _attn_sample(lam_s, qn, kb, vb, ck, cv, l, g_sub[l], n_p, nbs, dseq)
        y_p, hT_p = _ssd(proj, dtp, prev_zero, h0_zero, lw, nbp, seq, SSD_Q)
        proj_s = proj[n_p:n_p + n_s].reshape(nbs, dseq, MAIN_DIM)
        pad_seq = lambda a: jnp.pad(a, ((0, 0), (0, SSD_Q - dseq), (0, 0))).reshape(nbs * SSD_Q, a.shape[-1])
        prev_s = jnp.pad(state_conv[l], ((0, 0), (8 - (CONV_W - 1), 0), (0, 0)))
        y_s, hT_s = _ssd(pad_seq(proj_s), pad_seq(dtp[n_p:n_p + n_s].reshape(nbs, dseq, LANE)), prev_s,
                         state_ssm[l].reshape(nbs, D_SSM, SSM_N), lw, nbs, SSD_Q, dseq)
        y_s = y_s.reshape(nbs, SSD_Q, D_SSM)[:, :dseq].reshape(n_s, D_SSM)
        y_all = _pad_rows(jnp.concatenate([y_p, y_s], axis=0), t_pad)
        o_all = _pad_rows(jnp.concatenate([o_p, o_s], axis=0), t_pad)
        h = _out_proj(y_all, o_all, w_out_b, l, h)

        q_peer, c = _norm_proj(h, g_ffn[l], wq_b, l, with_xn=True)
        e1, n1, e2, r2 = _route(q_peer, k1_b, k2_b, l)
        h = _peer(c, u_b, vt_b, l, e1, n1, e2, r2, h)

        h = _ple(h, g_ple[l], p_all, wg_b, wp_b, l)

        xbc0 = 2 * D_ATT + D_ATT + D_SSM
        proj_p = proj[:n_p].reshape(nbp, seq, MAIN_DIM)
        outs[0].append(hT_p.reshape(nbp, H_S, SSM_P, SSM_N))
        outs[1].append(proj_p[:, seq - (CONV_W - 1):, xbc0:])
        outs[2].append(hT_s.reshape(nbs, H_S, SSM_P, SSM_N))
        outs[3].append(proj_s[:, dseq - (CONV_W - 1):, xbc0:])

    y_prompt = h[:n_p].reshape(nbp, seq, d)
    y_sample = h[n_p:n_p + n_s].reshape(nbs, dseq, d)
    ssm_p, conv_p, ssm_s, conv_s = (jnp.stack(o) for o in outs)
    return (y_prompt, y_sample,
            kp_all.reshape(depth, nbp, seq, H_A, 2, DH_QK), vp_all.reshape(depth, nbp, seq, H_A, DH_V), ssm_p, conv_p,
            ks_all.reshape(depth, nbs, dseq, H_A, 2, DH_QK), vs_all.reshape(depth, nbs, dseq, H_A, DH_V), ssm_s, conv_s)
```

```python
import functools
import math

import jax
import jax.numpy as jnp
from jax import lax
from jax.experimental import pallas as pl
from jax.experimental.pallas import tpu as pltpu

F32 = jnp.float32
BF16 = jnp.bfloat16

EPS = 1e-6
CHUNK = 64
CHUNK_SHIFT = 6
H_A = 8
DH_QK = 64
DH_V = 128
D_ATT = H_A * DH_V
ATT_SCALE = DH_QK ** -0.5
SSM_P = 64
H_S = 16
SSM_N = 128
SSM_G = 2
D_SSM = H_S * SSM_P
CONV_W = 4
CONV_DIM = D_SSM + 2 * SSM_G * SSM_N
MAIN_DIM = 2 * D_ATT + D_ATT + D_SSM + CONV_DIM
N_KEYS = 128
PEER_HEADS = 8
PEER_TOPK = 16
TOPK_SHIFT = 4

LANE = 128
TM = 512
MM_ROWS = 1152
SSD_Q = 128
VMEM_LIMIT = 56 * 1024 * 1024

_NT = (((1,), (1,)), ((), ()))
_HI = lax.Precision.HIGHEST


def _cparams(sem):
    return pltpu.CompilerParams(dimension_semantics=sem, vmem_limit_bytes=VMEM_LIMIT)


def _row_tile(t):
    for k in range(1, t // 16 + 1):
        if t % k == 0 and (t // k) % 16 == 0 and t // k <= MM_ROWS:
            return t // k
    return TM


def _rms(x, g):
    ms = jnp.mean(x * x, axis=-1, keepdims=True)
    return x * lax.rsqrt(ms + EPS) * g


def _norm_proj_kernel(x_ref, g_ref, w_ref, *rest, with_dt, with_xn):
    rest = list(rest)
    wdt_ref = rest.pop(0) if with_dt else None
    o_ref = rest.pop(0)
    odt_ref = rest.pop(0) if with_dt else None
    oxn_ref = rest.pop(0) if with_xn else None
    xn_ref = rest.pop(0)

    @pl.when(pl.program_id(1) == 0)
    def _():
        xn = _rms(x_ref[...], g_ref[...]).astype(BF16)
        xn_ref[...] = xn
        if with_xn:
            oxn_ref[...] = xn
        if with_dt:
            odt_ref[...] = jnp.dot(xn, wdt_ref[...], preferred_element_type=F32)

    o_ref[...] = jnp.dot(xn_ref[...], w_ref[...], preferred_element_type=F32)


def _norm_proj(x, g, w, layer, n, wdt=None, with_xn=False, tn=512):
    t, d = x.shape
    tm = _row_tile(t)
    with_dt = wdt is not None
    in_specs = [
        pl.BlockSpec((tm, d), lambda i, j: (i, 0)),
        pl.BlockSpec((1, d), lambda i, j: (0, 0)),
        pl.BlockSpec((None, d, tn), lambda i, j: (layer, 0, j)),
    ]
    args = [x, g.reshape(1, d), w]
    out_shape = [jax.ShapeDtypeStruct((t, n), F32)]
    out_specs = [pl.BlockSpec((tm, tn), lambda i, j: (i, j))]
    if with_dt:
        in_specs.append(pl.BlockSpec((None, d, LANE), lambda i, j: (layer, 0, 0)))
        args.append(wdt)
        out_shape.append(jax.ShapeDtypeStruct((t, LANE), F32))
        out_specs.append(pl.BlockSpec((tm, LANE), lambda i, j: (i, 0)))
    if with_xn:
        out_shape.append(jax.ShapeDtypeStruct((t, d), BF16))
        out_specs.append(pl.BlockSpec((tm, d), lambda i, j: (i, 0)))
    return pl.pallas_call(
        functools.partial(_norm_proj_kernel, with_dt=with_dt, with_xn=with_xn),
        grid=(t // tm, n // tn),
        in_specs=in_specs,
        out_specs=out_specs,
        out_shape=out_shape,
        scratch_shapes=[pltpu.VMEM((tm, d), BF16)],
        compiler_params=_cparams(("parallel", "arbitrary")),
        name="norm_proj",
    )(*args)


def _qk_norm_kernel(q_ref, k_ref, v_ref, qg_ref, kg_ref, vp_in, vs_in,
                    qn_ref, kb_ref, vb_ref, kf_ref, vp_ref, vs_ref, *, np_tiles, n_s):
    del vp_in, vs_in
    i = pl.program_id(0)
    lane = lax.broadcasted_iota(jnp.int32, (1, DH_V), 1)
    lo = lane < DH_QK

    def norm(x, g):
        xx = x * x
        s_lo = jnp.sum(jnp.where(lo, xx, 0.0), axis=-1, keepdims=True)
        s_all = jnp.sum(xx, axis=-1, keepdims=True)
        ms = jnp.where(lo, s_lo, s_all - s_lo) * (1.0 / DH_QK)
        return x * lax.rsqrt(ms + EPS) * g

    for h in range(H_A):
        sl = slice(h * DH_V, (h + 1) * DH_V)
        qn_ref[:, sl] = (norm(q_ref[:, sl], qg_ref[...]) * ATT_SCALE).astype(BF16)
        kn = norm(k_ref[:, sl], kg_ref[...])
        kf_ref[:, sl] = kn
        kb_ref[:, sl] = kn.astype(BF16)
    vb_ref[...] = v_ref[...].astype(BF16)

    @pl.when(i < np_tiles)
    def _():
        vp_ref[...] = v_ref[...]

    @pl.when(i == np_tiles)
    def _():
        vs_ref[...] = v_ref[0:n_s, :]


def _qk_norm(proj, q_gain, k_gain, layer, vp, vs):
    t = proj.shape[0]
    n_p, n_s = vp.shape[1], vs.shape[1]
    assert n_p % TM == 0 and n_s <= TM and t >= n_p + TM
    np_tiles = n_p // TM
    blk = lambda c: pl.BlockSpec((TM, D_ATT), lambda i, c=c: (i, c))
    gspec = pl.BlockSpec((1, DH_V), lambda i: (0, 0))
    hbm = pl.BlockSpec(memory_space=pl.ANY)
    p_out = pl.BlockSpec((None, TM, D_ATT), lambda i: (layer, jnp.minimum(i, np_tiles - 1), 0))
    s_out = pl.BlockSpec((None, n_s, D_ATT), lambda i: (layer, 0, 0))
    return pl.pallas_call(
        functools.partial(_qk_norm_kernel, np_tiles=np_tiles, n_s=n_s),
        grid=(t // TM,),
        in_specs=[blk(0), blk(1), blk(2), gspec, gspec, hbm, hbm],
        out_specs=[blk(0), blk(0), blk(0), blk(0), p_out, s_out],
        out_shape=[
            jax.ShapeDtypeStruct((t, D_ATT), BF16),
            jax.ShapeDtypeStruct((t, D_ATT), BF16),
            jax.ShapeDtypeStruct((t, D_ATT), BF16),
            jax.ShapeDtypeStruct((t, D_ATT), F32),
            jax.ShapeDtypeStruct(vp.shape, F32),
            jax.ShapeDtypeStruct(vs.shape, F32),
        ],
        input_output_aliases={5: 4, 6: 5},
        compiler_params=_cparams(("arbitrary",)),
        name="qk_norm",
    )(proj, proj, proj, q_gain.reshape(1, DH_V), k_gain.reshape(1, DH_V), vp, vs)


def _sub_norm(o, lam_ref, gsub):
    return _rms(o, gsub) * lam_ref[1]


def _attn_prompt_kernel(lam_ref, q_ref, k_ref, v_ref, gsub_ref, o_ref, *, tq, tk):
    qi = pl.program_id(2)
    q = q_ref[...]
    lane = lax.broadcasted_iota(jnp.int32, (1, DH_V), 1)
    zero = jnp.zeros((), BF16)
    q0 = jnp.where(lane < DH_QK, q, zero)
    q1 = jnp.where(lane >= DH_QK, q, zero)

    def block(start, mask):
        kb = k_ref[pl.ds(start, tk), :]
        s0 = lax.dot_general(q0, kb, _NT, preferred_element_type=F32)
        s1 = lax.dot_general(q1, kb, _NT, preferred_element_type=F32)
        if mask is not None:
            s0 = jnp.where(mask, s0, -jnp.inf)
            s1 = jnp.where(mask, s1, -jnp.inf)
        return s0, s1, v_ref[pl.ds(start, tk), :]

    def upd(s, vb, m, l, a):
        mn = jnp.maximum(m, jnp.max(s, axis=-1, keepdims=True))
        alpha = jnp.exp(m - mn)
        p = jnp.exp(s - mn)
        l = alpha * l + jnp.sum(p, axis=-1, keepdims=True)
        a = alpha * a + jnp.dot(p.astype(BF16), vb, preferred_element_type=F32)
        return mn, l, a

    row = jnp.right_shift(lax.broadcasted_iota(jnp.int32, (tq, tk), 0), CHUNK_SHIFT)
    col = jnp.right_shift(lax.broadcasted_iota(jnp.int32, (tq, tk), 1), CHUNK_SHIFT)
    base = pl.multiple_of(qi * tq, tq)
    s0, s1, vb = block(base, col <= row)
    m0 = jnp.max(s0, axis=-1, keepdims=True)
    m1 = jnp.max(s1, axis=-1, keepdims=True)
    p0 = jnp.exp(s0 - m0)
    p1 = jnp.exp(s1 - m1)
    l0 = jnp.sum(p0, axis=-1, keepdims=True)
    l1 = jnp.sum(p1, axis=-1, keepdims=True)
    a0 = jnp.dot(p0.astype(BF16), vb, preferred_element_type=F32)
    a1 = jnp.dot(p1.astype(BF16), vb, preferred_element_type=F32)
    for dblk in range(1, tq // tk):
        s0, s1, vb = block(base + dblk * tk, col + dblk * (tk // CHUNK) <= row)
        m0, l0, a0 = upd(s0, vb, m0, l0, a0)
        m1, l1, a1 = upd(s1, vb, m1, l1, a1)

    def body(j, carry):
        m0, l0, a0, m1, l1, a1 = carry
        s0, s1, vb = block(pl.multiple_of(j * tk, tk), None)
        m0, l0, a0 = upd(s0, vb, m0, l0, a0)
        m1, l1, a1 = upd(s1, vb, m1, l1, a1)
        return m0, l0, a0, m1, l1, a1

    m0, l0, a0, m1, l1, a1 = lax.fori_loop(0, qi * (tq // tk), body, (m0, l0, a0, m1, l1, a1))
    o = a0 / l0 - lam_ref[0] * (a1 / l1)
    o_ref[...] = _sub_norm(o, lam_ref, gsub_ref[...]).astype(BF16)


def _attn_prompt(lam, qn, kb, vb, g_sub, nb, seq, tq=512, tk=512):
    tq = min(tq, seq)
    tk = min(tk, tq)
    nq = seq // tq
    return pl.pallas_call(
        functools.partial(_attn_prompt_kernel, tq=tq, tk=tk),
        grid=(nb, H_A, nq),
        in_specs=[
            pl.BlockSpec(memory_space=pltpu.SMEM),
            pl.BlockSpec((tq, DH_V), lambda b, h, i: (b * nq + i, h)),
            pl.BlockSpec((seq, DH_V), lambda b, h, i: (b, h)),
            pl.BlockSpec((seq, DH_V), lambda b, h, i: (b, h)),
            pl.BlockSpec((1, DH_V), lambda b, h, i: (0, 0)),
        ],
        out_specs=pl.BlockSpec((tq, DH_V), lambda b, h, i: (b * nq + i, h)),
        out_shape=jax.ShapeDtypeStruct((nb * seq, D_ATT), BF16),
        compiler_params=_cparams(("parallel", "parallel", "arbitrary")),
        name="attn_prompt",
    )(lam, qn, kb, vb, g_sub.reshape(1, DH_V))


def _attn_sample_kernel(lam_ref, q_ref, kn_ref, vn_ref, ck_ref, cv_ref, gsub_ref, o_ref):
    q = q_ref[...]
    nq = q.shape[0]
    lane = lax.broadcasted_iota(jnp.int32, (1, DH_V), 1)
    zero = jnp.zeros((), BF16)
    q01 = jnp.concatenate([jnp.where(lane < DH_QK, q, zero), jnp.where(lane >= DH_QK, q, zero)], axis=0)
    sc = lax.dot_general(q01, ck_ref[...].astype(BF16), _NT, preferred_element_type=F32)
    sn = lax.dot_general(q01, kn_ref[...], _NT, preferred_element_type=F32)
    m = jnp.maximum(jnp.max(sc, axis=-1, keepdims=True), jnp.max(sn, axis=-1, keepdims=True))
    pc = jnp.exp(sc - m)
    pn = jnp.exp(sn - m)
    inv = 1.0 / (jnp.sum(pc, axis=-1, keepdims=True) + jnp.sum(pn, axis=-1, keepdims=True))
    pc = pc * inv
    pn = pn * inv
    lam = lam_ref[0]
    wc = (pc[:nq] - lam * pc[nq:]).astype(BF16)
    wn = (pn[:nq] - lam * pn[nq:]).astype(BF16)
    o = jnp.dot(wc, cv_ref[...].astype(BF16), preferred_element_type=F32)
    o = o + jnp.dot(wn, vn_ref[...], preferred_element_type=F32)
    o_ref[...] = _sub_norm(o, lam_ref, gsub_ref[...]).astype(BF16)


def _attn_sample(lam, qn, kb, vb, cache_k, cache_v, layer, g_sub, row0, nb, nq):
    past = cache_k.shape[2]
    r0 = row0 // nq
    new = lambda: pl.BlockSpec((nq, DH_V), lambda b, h: (r0 + b, h))
    cache = lambda: pl.BlockSpec((None, None, past, DH_V), lambda b, h: (layer, b, 0, h))
    return pl.pallas_call(
        _attn_sample_kernel,
        grid=(nb, H_A),
        in_specs=[pl.BlockSpec(memory_space=pltpu.SMEM), new(), new(), new(), cache(), cache(),
                  pl.BlockSpec((1, DH_V), lambda b, h: (0, 0))],
        out_specs=pl.BlockSpec((nq, DH_V), lambda b, h: (b, h)),
        out_shape=jax.ShapeDtypeStruct((nb * nq, D_ATT), BF16),
        compiler_params=_cparams(("parallel", "parallel")),
        name="attn_sample",
    )(lam, qn, kb, vb, cache_k, cache_v, g_sub.reshape(1, DH_V))


def _ssd_kernel(xs_ref, b_ref, c_ref, z_ref, dt_ref, prev_ref, h0_ref, cw_ref, cb_ref, dtb_ref,
                aneg_ref, dexp_ref, gssd_ref, expand_ref, y_ref, hout_ref, ext_ref, ht_ref, *, valid):
    q = SSD_Q
    c = pl.program_id(1)
    halo = 8

    @pl.when(c == 0)
    def _():
        ext_ref[0:halo, :] = prev_ref[...]
        ht_ref[...] = h0_ref[...].T

    ext_ref[halo:halo + q, 0:D_SSM] = xs_ref[...]
    ext_ref[halo:halo + q, D_SSM:D_SSM + SSM_G * SSM_N] = b_ref[...]
    ext_ref[halo:halo + q, D_SSM + SSM_G * SSM_N:CONV_DIM] = c_ref[...]
    conv = cb_ref[...]
    for j in range(CONV_W):
        r0 = halo - (CONV_W - 1) + j
        conv = conv + ext_ref[r0:r0 + q, :] * cw_ref[j:j + 1, :]
    tail = ext_ref[q:q + halo, :]
    ext_ref[0:halo, :] = tail
    xbc = conv * jax.nn.sigmoid(conv)
    xs = xbc[:, 0:D_SSM]
    bm = xbc[:, D_SSM:D_SSM + SSM_G * SSM_N]
    cm = xbc[:, D_SSM + SSM_G * SSM_N:CONV_DIM]

    dt = jax.nn.softplus(dt_ref[...] + dtb_ref[...])
    if valid < q:
        rowv = lax.broadcasted_iota(jnp.int32, (q, LANE), 0) < valid
        dt = jnp.where(rowv, dt, 0.0)
    a = dt * aneg_ref[...]
    r_i = lax.broadcasted_iota(jnp.int32, (q, q), 0)
    c_i = lax.broadcasted_iota(jnp.int32, (q, q), 1)
    tril = c_i <= r_i
    acum = jnp.dot(tril.astype(F32), a, precision=_HI, preferred_element_type=F32)
    eye = (r_i == c_i).astype(F32)
    acum_t = lax.dot_general(eye, acum, _NT, precision=_HI, preferred_element_type=F32)
    a_last = acum[q - 1:q, :]
    decay_in = jnp.exp(a_last - acum)
    stack = jnp.concatenate(
        [dt, dt * decay_in, jnp.exp(acum), jnp.broadcast_to(jnp.exp(a_last), (8, LANE))], axis=0)
    ex = jnp.dot(stack, expand_ref[...], precision=_HI, preferred_element_type=F32)
    dt_e = ex[0:q]
    dtd_e = ex[q:2 * q]
    eacum_e = ex[2 * q:3 * q]
    cd_e = ex[3 * q:3 * q + 1]

    xdt = (xs * dt_e).astype(BF16)
    xdtd = (xs * dtd_e).astype(BF16)
    lane = lax.broadcasted_iota(jnp.int32, (1, LANE), 1)
    zero = jnp.zeros((), BF16)
    gw = D_SSM // SSM_G
    hpg = H_S // SSM_G
    y_diag_parts = []
    y_off_parts = []
    for g in range(SSM_G):
        gs = slice(g * gw, (g + 1) * gw)
        bg = bm[:, g * SSM_N:(g + 1) * SSM_N]
        cg = cm[:, g * SSM_N:(g + 1) * SSM_N].astype(BF16)
        cb = lax.dot_general(cg, bg.astype(BF16), _NT, preferred_element_type=F32)
        ht_prev = ht_ref[:, gs]
        y_off = jnp.dot(cg, ht_prev.astype(BF16), preferred_element_type=F32)
        ht_ref[:, gs] = cd_e[:, gs] * ht_prev + jnp.dot(
            bg.T.astype(BF16), xdtd[:, gs], preferred_element_type=F32)
        for k in range(hpg // 2):
            ms = []
            for h in (g * hpg + 2 * k, g * hpg + 2 * k + 1):
                seg = acum[:, h:h + 1] - acum_t[h:h + 1, :]
                ms.append((cb * jnp.exp(jnp.where(tril, seg, -jnp.inf))).astype(BF16))
            pair = slice((g * hpg + 2 * k) * SSM_P, (g * hpg + 2 * k + 2) * SSM_P)
            xp = xdt[:, pair]
            rhs = jnp.concatenate([jnp.where(lane < SSM_P, xp, zero), jnp.where(lane >= SSM_P, xp, zero)], axis=0)
            y_diag_parts.append(jnp.dot(jnp.concatenate(ms, axis=1), rhs, preferred_element_type=F32))
        y_off_parts.append(y_off)
    y_diag = jnp.concatenate(y_diag_parts, axis=1)
    y_off = jnp.concatenate(y_off_parts, axis=1)
    y = y_diag + y_off * eacum_e + dexp_ref[...] * xs
    zz = z_ref[...]
    y = y * (zz * jax.nn.sigmoid(zz))
    for g in range(SSM_G):
        gs = slice(g * gw, (g + 1) * gw)
        y_ref[:, gs] = _rms(y[:, gs], gssd_ref[:, gs]).astype(BF16)

    @pl.when(c == pl.num_programs(1) - 1)
    def _():
        hout_ref[...] = ht_ref[...].T


def _ssd(proj, dtp, prev, h0, lw, nb, seq, valid):
    nc = seq // SSD_Q
    row = lambda w, col: pl.BlockSpec((SSD_Q, w), lambda b, c, col=col: (b * nc + c, col))
    const = lambda shp: pl.BlockSpec(shp, lambda b, c: (0,) * len(shp))
    per_b = lambda shp: pl.BlockSpec((None,) + shp, lambda b, c: (b, 0, 0))
    return pl.pallas_call(
        functools.partial(_ssd_kernel, valid=valid),
        grid=(nb, nc),
        in_specs=[
            row(D_SSM, 4), row(SSM_G * SSM_N, 20), row(SSM_G * SSM_N, 21), row(D_SSM, 3), row(LANE, 0),
            per_b((8, CONV_DIM)), per_b((D_SSM, SSM_N)),
            const((CONV_W, CONV_DIM)), const((1, CONV_DIM)), const((1, LANE)), const((1, LANE)),
            const((1, D_SSM)), const((1, D_SSM)), const((LANE, D_SSM)),
        ],
        out_specs=[row(D_SSM, 0), per_b((D_SSM, SSM_N))],
        out_shape=[jax.ShapeDtypeStruct((nb * seq, D_SSM), BF16),
                   jax.ShapeDtypeStruct((nb, D_SSM, SSM_N), F32)],
        scratch_shapes=[pltpu.VMEM((SSD_Q + 8, CONV_DIM), F32), pltpu.VMEM((SSM_N, D_SSM), F32)],
        compiler_params=_cparams(("parallel", "arbitrary")),
        name="ssd",
    )(proj, proj, proj, proj, dtp, prev, h0, lw["conv_w"], lw["conv_b"], lw["dt_bias"], lw["a_neg"],
      lw["d_exp"], lw["g_ssd"], lw["expand"])


def _out_proj_kernel(y_ref, o_ref, w_ref, h_ref, out_ref):
    acc = jnp.dot(y_ref[...], w_ref[0:D_SSM, :], preferred_element_type=F32)
    acc = acc + jnp.dot(o_ref[...], w_ref[D_SSM:D_SSM + D_ATT, :], preferred_element_type=F32)
    out_ref[...] = h_ref[...] + acc


def _out_proj(y, o, w, layer, h, tn=512):
    t, d = h.shape
    tm = _row_tile(t)
    return pl.pallas_call(
        _out_proj_kernel,
        grid=(t // tm, d // tn),
        in_specs=[
            pl.BlockSpec((tm, D_SSM), lambda i, j: (i, 0)),
            pl.BlockSpec((tm, D_ATT), lambda i, j: (i, 0)),
            pl.BlockSpec((None, D_SSM + D_ATT, tn), lambda i, j: (layer, 0, j)),
            pl.BlockSpec((tm, tn), lambda i, j: (i, j)),
        ],
        out_specs=pl.BlockSpec((tm, tn), lambda i, j: (i, j)),
        out_shape=jax.ShapeDtypeStruct((t, d), F32),
        compiler_params=_cparams(("parallel", "parallel")),
        name="out_proj",
    )(y, o, w, h)


def _top16(s, n_out=None):
    n, t = s.shape
    iota = lax.broadcasted_iota(jnp.int32, (n, t), 0)
    row16 = lax.broadcasted_iota(jnp.int32, (PEER_TOPK, t), 0)
    rank = jnp.full((n, t), PEER_TOPK, jnp.int32)
    vals = jnp.zeros((PEER_TOPK, t), F32)
    cnt = jnp.zeros((n_out, t), jnp.int32) if n_out else None
    for r in range(PEER_TOPK):
        m = jnp.max(s, axis=0, keepdims=True)
        idx = jnp.min(jnp.where(s == m, iota, n), axis=0, keepdims=True)
        sel = iota == idx
        rank = jnp.where(sel, r, rank)
        s = jnp.where(sel, -jnp.inf, s)
        vals = jnp.where(row16 == r, m, vals)
        if n_out:
            cnt = cnt + (row16 == jnp.right_shift(idx, TOPK_SHIFT)).astype(jnp.int32)
    return vals, rank, cnt


_MARK = 2.0 ** 126


def _peel16(s):
    t = s.shape[1]
    row16 = lax.broadcasted_iota(jnp.int32, (PEER_TOPK, t), 0)
    vals = jnp.zeros((PEER_TOPK, t), F32)
    for r in range(PEER_TOPK):
        m = jnp.max(s, axis=0, keepdims=True)
        s = jnp.where(s == m, -(1.0 + (r + 1) / 32.0) * _MARK, s)
        vals = jnp.where(row16 == r, m, vals)
    marked = s < -_MARK
    rank = jnp.where(marked, s * (-32.0 / _MARK) - 33.0, float(PEER_TOPK))
    count = jnp.sum(jnp.where(marked, 1.0, 0.0), axis=0, keepdims=True)
    return vals, rank, count


def _route_outputs(s1, s2, v1, v2, rank1, rank2, cnt, zsum):
    n1 = jnp.zeros(s1.shape, F32)
    for j in range(PEER_TOPK):
        n1 = jnp.where(rank1 == j, cnt[j:j + 1, :], n1)
    e1 = jnp.where(rank1 < PEER_TOPK, jnp.exp(s1 - v1[0:1, :]) / zsum, 0.0)
    e2 = jnp.where(rank2 < PEER_TOPK, jnp.exp(s2 - v2[0:1, :]), 0.0)
    return e1, n1, e2.astype(BF16), rank2.astype(BF16)


def _route_exact(s1, s2):
    v1, rank1, _ = _top16(s1)
    v2, rank2, _ = _top16(s2)
    cand = jnp.concatenate([v1[j:j + 1, :] + v2 for j in range(PEER_TOPK)], axis=0)
    sc, _, cnt = _top16(cand, n_out=PEER_TOPK)
    zsum = jnp.sum(jnp.exp(sc - sc[0:1, :]), axis=0, keepdims=True)
    return _route_outputs(s1, s2, v1, v2, rank1.astype(F32), rank2.astype(F32), cnt.astype(F32), zsum)


def _route_fast(s1, s2):
    half = PEER_TOPK // 2
    t = s1.shape[1]
    v1, rank1, c1 = _peel16(s1)
    v2, rank2, c2 = _peel16(s2)
    pieces = [v1[j:j + 1, :] + v2[0:half, :] for j in range(half)]
    pieces.append(v1[half:, :] + v2[0:1, :])
    pieces.append(v1[0:1, :] + v2[half:, :])
    cand = jnp.concatenate(pieces, axis=0)
    sc, rankc, cc = _peel16(cand)
    sel = rankc < PEER_TOPK
    zsum = jnp.sum(jnp.where(sel, jnp.exp(cand - sc[0:1, :]), 0.0), axis=0, keepdims=True)
    mk = jnp.where(sel, 1.0, 0.0)
    row8 = lax.broadcasted_iota(jnp.int32, (half, t), 0)
    n_lo = jnp.zeros((half, t), F32)
    for j in range(half):
        rs = jnp.sum(mk[j * half:(j + 1) * half, :], axis=0, keepdims=True)
        if j == 0:
            rs = rs + jnp.sum(mk[(half + 1) * half:, :], axis=0, keepdims=True)
        n_lo = jnp.where(row8 == j, rs, n_lo)
    cnt = jnp.concatenate([n_lo, mk[half * half:(half + 1) * half, :]], axis=0)
    bad = jnp.abs(c1 - PEER_TOPK) + jnp.abs(c2 - PEER_TOPK) + jnp.abs(cc - PEER_TOPK)
    return _route_outputs(s1, s2, v1, v2, rank1, rank2, cnt, zsum), bad


def _route_kernel(q_ref, k1_ref, k2_ref, e1_ref, n1_ref, e2_ref, r2_ref, *, tm):
    def store(ls, outs):
        e1_ref[:, ls], n1_ref[:, ls], e2_ref[:, ls], r2_ref[:, ls] = outs

    def scores(ls):
        qb = q_ref[ls, :].astype(BF16)
        s1 = lax.dot_general(k1_ref[...], qb[:, 0:N_KEYS], _NT, preferred_element_type=F32)
        s2 = lax.dot_general(k2_ref[...], qb[:, N_KEYS:2 * N_KEYS], _NT, preferred_element_type=F32)
        return s1, s2

    group = 4
    for g0 in range(0, tm // LANE, group):
        tiles = [slice(lt * LANE, (lt + 1) * LANE) for lt in range(g0, min(g0 + group, tm // LANE))]
        flags = []
        for ls in tiles:
            outs, bad = _route_fast(*scores(ls))
            store(ls, outs)
            flags.append(jnp.max(bad) > 0.0)
        for ls, flag in zip(tiles, flags):
            @pl.when(flag)
            def _(ls=ls):
                store(ls, _route_exact(*scores(ls)))


def _route(q, k1, k2, layer, tm=TM):
    t = q.shape[0]
    kd = k1.shape[-1]
    out = lambda: pl.BlockSpec((None, N_KEYS, tm), lambda i, h: (h, 0, i))
    key = lambda: pl.BlockSpec((None, None, N_KEYS, kd), lambda i, h: (layer, h, 0, 0))
    return pl.pallas_call(
        functools.partial(_route_kernel, tm=tm),
        grid=(t // tm, PEER_HEADS),
        in_specs=[pl.BlockSpec((tm, 2 * kd), lambda i, h: (i, h)), key(), key()],
        out_specs=[out(), out(), out(), out()],
        out_shape=[jax.ShapeDtypeStruct((PEER_HEADS, N_KEYS, t), F32),
                   jax.ShapeDtypeStruct((PEER_HEADS, N_KEYS, t), F32),
                   jax.ShapeDtypeStruct((PEER_HEADS, N_KEYS, t), BF16),
                   jax.ShapeDtypeStruct((PEER_HEADS, N_KEYS, t), BF16)],
        compiler_params=_cparams(("parallel", "parallel")),
        name="peer_route",
    )(q, k1, k2)


def _peer_kernel(x_ref, u_ref, vt_ref, e1_ref, n1_ref, e2_ref, r2_ref, h_ref, out_ref, acc_ref, *, te):
    j = pl.program_id(1)

    @pl.when(j == 0)
    def _():
        acc_ref[...] = jnp.zeros_like(acc_ref)

    at = lax.dot_general(u_ref[...], x_ref[...], _NT, preferred_element_type=F32)
    act = (0.5 * at * (1.0 + lax.erf(at * (2.0 ** -0.5)))).astype(BF16)
    zero = jnp.zeros((), BF16)
    parts = []
    for c in range(te // N_KEYS):
        w = None
        for h in range(PEER_HEADS):
            n1 = n1_ref[h, c:c + 1, :].astype(BF16)
            e1 = e1_ref[h, c:c + 1, :].astype(BF16)
            wh = jnp.where(r2_ref[h] < n1, e1 * e2_ref[h], zero)
            w = wh if w is None else w + wh
        parts.append(act[c * N_KEYS:(c + 1) * N_KEYS, :] * w)
    ht = jnp.concatenate(parts, axis=0)
    acc_ref[...] += jnp.dot(vt_ref[...], ht, preferred_element_type=F32)

    @pl.when(j == pl.num_programs(1) - 1)
    def _():
        out_ref[...] = h_ref[...] + acc_ref[...].T


def _peer(x, u, vt, layer, e1, n1, e2, r2, h, te=1024):
    t, d = h.shape
    ne = u.shape[1]
    rows = te // N_KEYS
    return pl.pallas_call(
        functools.partial(_peer_kernel, te=te),
        grid=(t // TM, ne // te),
        in_specs=[
            pl.BlockSpec((TM, d), lambda i, j: (i, 0)),
            pl.BlockSpec((None, te, d), lambda i, j: (layer, j, 0)),
            pl.BlockSpec((None, d, te), lambda i, j: (layer, 0, j)),
            pl.BlockSpec((PEER_HEADS, rows, TM), lambda i, j: (0, j, i)),
            pl.BlockSpec((PEER_HEADS, rows, TM), lambda i, j: (0, j, i)),
            pl.BlockSpec((PEER_HEADS, N_KEYS, TM), lambda i, j: (0, 0, i)),
            pl.BlockSpec((PEER_HEADS, N_KEYS, TM), lambda i, j: (0, 0, i)),
            pl.BlockSpec((TM, d), lambda i, j: (i, 0)),
        ],
        out_specs=pl.BlockSpec((TM, d), lambda i, j: (i, 0)),
        out_shape=jax.ShapeDtypeStruct((t, d), F32),
        scratch_shapes=[pltpu.VMEM((d, TM), F32)],
        compiler_params=_cparams(("parallel", "arbitrary")),
        name="peer_experts",
    )(x, u, vt, e1, n1, e2, r2, h)


def _ple_kernel(h_ref, g_ref, p_ref, wg_ref, wp_ref, hc_ref, out_ref, xn_ref):
    @pl.when(pl.program_id(1) == 0)
    def _():
        xn_ref[...] = _rms(h_ref[...], g_ref[...]).astype(BF16)

    gate = jax.nn.sigmoid(jnp.dot(xn_ref[...], wg_ref[...], preferred_element_type=F32))
    emb = jnp.dot(p_ref[...].astype(BF16), wp_ref[...], preferred_element_type=F32)
    out_ref[...] = hc_ref[...] + emb * gate


def _ple(h, g, p, wg, wp, layer, tn=512):
    t, d = h.shape
    pd = p.shape[2]
    tm = _row_tile(t)
    return pl.pallas_call(
        _ple_kernel,
        grid=(t // tm, d // tn),
        in_specs=[
            pl.BlockSpec((tm, d), lambda i, j: (i, 0)),
            pl.BlockSpec((1, d), lambda i, j: (0, 0)),
            pl.BlockSpec((None, tm, pd), lambda i, j: (layer, i, 0)),
            pl.BlockSpec((None, d, tn), lambda i, j: (layer, 0, j)),
            pl.BlockSpec((None, pd, tn), lambda i, j: (layer, 0, j)),
            pl.BlockSpec((tm, tn), lambda i, j: (i, j)),
        ],
        out_specs=pl.BlockSpec((tm, tn), lambda i, j: (i, j)),
        out_shape=jax.ShapeDtypeStruct((t, d), F32),
        scratch_shapes=[pltpu.VMEM((tm, d), BF16)],
        compiler_params=_cparams(("parallel", "arbitrary")),
        name="ple_gate",
    )(h, g.reshape(1, d), p, wg, wp, h)


def _lambda_init(layer_idx):
    return 0.8 - 0.6 * math.exp(-0.3 * layer_idx)


def _pad_rows(x, rows):
    return jnp.pad(x, ((0, rows - x.shape[0]),) + ((0, 0),) * (x.ndim - 1))


def kernel(x_prompt, x_sample, cache_k, cache_v, state_ssm, state_conv, p_prompt, p_sample, g_mix, w_in,
           conv_w, conv_b, dt_bias, a_log, d_skip, g_ssd, q_gain, k_gain, lam_q1, lam_k1, lam_q2, lam_k2,
           g_sub, w_out, g_ffn, peer_wq, peer_k1, peer_k2, peer_u, peer_v, g_ple, w_ple, w_pgate):
    nbp, seq, d = x_prompt.shape
    nbs, dseq, _ = x_sample.shape
    depth = w_in.shape[0]
    past = cache_k.shape[2]
    n_p = nbp * seq
    n_s = nbs * dseq
    t_pad = -(-(n_p + n_s) // TM) * TM

    h = _pad_rows(jnp.concatenate([x_prompt.reshape(n_p, d), x_sample.reshape(n_s, d)], axis=0), t_pad)
    ple_dim = p_prompt.shape[-1]
    p_all = jnp.concatenate([p_prompt.reshape(depth, n_p, ple_dim), p_sample.reshape(depth, n_s, ple_dim)], axis=1)
    p_all = jnp.pad(p_all, ((0, 0), (0, t_pad - n_p - n_s), (0, 0)))
    ck = cache_k.reshape(depth, nbs, past, D_ATT)
    cv = cache_v.reshape(depth, nbs, past, D_ATT)
    expand = (jnp.arange(LANE)[:, None] == jnp.arange(D_SSM)[None, :] // SSM_P).astype(F32)
    prev_zero = jnp.zeros((nbp, 8, CONV_DIM), F32)
    h0_zero = jnp.zeros((nbp, D_SSM, SSM_N), F32)

    w_main = w_in.astype(BF16)
    w_dt = jnp.pad(w_in[:, :, MAIN_DIM:], ((0, 0), (0, 0), (0, LANE - H_S))).astype(BF16)
    w_out_b = w_out.astype(BF16)
    wq_b = peer_wq.astype(BF16)
    k1_b = peer_k1.astype(BF16)
    k2_b = peer_k2.astype(BF16)
    u_b = peer_u.astype(BF16)
    vt_b = jnp.swapaxes(peer_v, 1, 2).astype(BF16)
    wg_b = w_pgate.astype(BF16)
    wp_b = w_ple.astype(BF16)

    vp_all = jnp.zeros((depth, n_p, D_ATT), F32)
    vs_all = jnp.zeros((depth, n_s, D_ATT), F32)

    outs = [[] for _ in range(6)]
    for l in range(depth):
        lam0 = _lambda_init(l)
        lam = (jnp.exp(jnp.sum(lam_q1[l] * lam_k1[l])) - jnp.exp(jnp.sum(lam_q2[l] * lam_k2[l])) + lam0)
        lam_s = jnp.stack([lam, jnp.asarray(1.0 - lam0, F32)]).astype(F32)
        lw = dict(
            conv_w=conv_w[l], conv_b=conv_b[l].reshape(1, CONV_DIM),
            dt_bias=_pad_rows(dt_bias[l], LANE).reshape(1, LANE),
            a_neg=_pad_rows(-jnp.exp(a_log[l]), LANE).reshape(1, LANE),
            d_exp=jnp.repeat(d_skip[l], SSM_P).reshape(1, D_SSM),
            g_ssd=g_ssd[l].reshape(1, D_SSM), expand=expand)

        proj, dtp = _norm_proj(h, g_mix[l], w_main, l, MAIN_DIM, wdt=w_dt)
        qn, kb, vb, kf, vp_all, vs_all = _qk_norm(proj, q_gain[l], k_gain[l], l, vp_all, vs_all)
        o_p = _attn_prompt(lam_s, qn, kb, vb, g_sub[l], nbp, seq)
        o_s = _attn_sample(lam_s, qn, kb, vb, ck, cv, l, g_sub[l], n_p, nbs, dseq)
        y_p, hT_p = _ssd(proj, dtp, prev_zero, h0_zero, lw, nbp, seq, SSD_Q)
        proj_s = proj[n_p:n_p + n_s].reshape(nbs, dseq, MAIN_DIM)
        pad_seq = lambda a: jnp.pad(a, ((0, 0), (0, SSD_Q - dseq), (0, 0))).reshape(nbs * SSD_Q, a.shape[-1])
        prev_s = jnp.pad(state_conv[l], ((0, 0), (8 - (CONV_W - 1), 0), (0, 0)))
        y_s, hT_s = _ssd(pad_seq(proj_s), pad_seq(dtp[n_p:n_p + n_s].reshape(nbs, dseq, LANE)), prev_s,
                         state_ssm[l].reshape(nbs, D_SSM, SSM_N), lw, nbs, SSD_Q, dseq)
        y_s = y_s.reshape(nbs, SSD_Q, D_SSM)[:, :dseq].reshape(n_s, D_SSM)
        y_all = _pad_rows(jnp.concatenate([y_p, y_s], axis=0), t_pad)
        o_all = _pad_rows(jnp.concatenate([o_p, o_s], axis=0), t_pad)
        h = _out_proj(y_all, o_all, w_out_b, l, h)

        q_peer, c = _norm_proj(h, g_ffn[l], wq_b, l, d, with_xn=True)
        e1, n1, e2, r2 = _route(q_peer, k1_b, k2_b, l)
        h = _peer(c, u_b, vt_b, l, e1, n1, e2, r2, h)

        h = _ple(h, g_ple[l], p_all, wg_b, wp_b, l)

        xbc0 = 2 * D_ATT + D_ATT + D_SSM
        tail = CONV_W - 1
        outs[0].append(hT_p.reshape(nbp, H_S, SSM_P, SSM_N))
        outs[1].append(jnp.stack([proj[(b + 1) * seq - tail:(b + 1) * seq, xbc0:] for b in range(nbp)]))
        outs[2].append(hT_s.reshape(nbs, H_S, SSM_P, SSM_N))
        outs[3].append(proj_s[:, dseq - tail:, xbc0:])
        outs[4].append(kf[:n_p].reshape(nbp, seq, H_A, 2, DH_QK))
        outs[5].append(kf[n_p:n_p + n_s].reshape(nbs, dseq, H_A, 2, DH_QK))

    y_prompt = h[:n_p].reshape(nbp, seq, d)
    y_sample = h[n_p:n_p + n_s].reshape(nbs, dseq, d)
    ssm_p, conv_p, ssm_s, conv_s, k_p, k_s = (jnp.stack(o) for o in outs)
    return (y_prompt, y_sample,
            k_p, vp_all.reshape(depth, nbp, seq, H_A, DH_V), ssm_p, conv_p,
            k_s, vs_all.reshape(depth, nbs, dseq, H_A, DH_V), ssm_s, conv_s)
```

```python
import functools
import math

import jax
import jax.numpy as jnp
from jax import lax
from jax.experimental import pallas as pl
from jax.experimental.pallas import tpu as pltpu

F32 = jnp.float32
BF16 = jnp.bfloat16

EPS = 1e-6
CHUNK = 64
CHUNK_SHIFT = 6
H_A = 8
DH_QK = 64
DH_V = 128
D_ATT = H_A * DH_V
ATT_SCALE = DH_QK ** -0.5
SSM_P = 64
H_S = 16
SSM_N = 128
SSM_G = 2
D_SSM = H_S * SSM_P
CONV_W = 4
CONV_DIM = D_SSM + 2 * SSM_G * SSM_N
MAIN_DIM = 2 * D_ATT + D_ATT + D_SSM + CONV_DIM
N_KEYS = 128
PEER_HEADS = 8
PEER_TOPK = 16
TOPK_SHIFT = 4

LANE = 128
TM = 512
MM_ROWS = 1152
SSD_Q = 128
VMEM_LIMIT = 56 * 1024 * 1024

_NT = (((1,), (1,)), ((), ()))
_HI = lax.Precision.HIGHEST


def _cparams(sem):
    return pltpu.CompilerParams(dimension_semantics=sem, vmem_limit_bytes=VMEM_LIMIT)


def _row_tile(t):
    for k in range(1, t // 16 + 1):
        if t % k == 0 and (t // k) % 16 == 0 and t // k <= MM_ROWS:
            return t // k
    return TM


def _rms(x, g):
    ms = jnp.mean(x * x, axis=-1, keepdims=True)
    return x * lax.rsqrt(ms + EPS) * g


def _norm_proj_kernel(x_ref, g_ref, w_ref, *rest, with_dt, with_xn):
    rest = list(rest)
    wdt_ref = rest.pop(0) if with_dt else None
    o_ref = rest.pop(0)
    odt_ref = rest.pop(0) if with_dt else None
    oxn_ref = rest.pop(0) if with_xn else None
    xn_ref = rest.pop(0)

    @pl.when(pl.program_id(1) == 0)
    def _():
        xn = _rms(x_ref[...], g_ref[...]).astype(BF16)
        xn_ref[...] = xn
        if with_xn:
            oxn_ref[...] = xn
        if with_dt:
            odt_ref[...] = jnp.dot(xn, wdt_ref[...], preferred_element_type=F32)

    o_ref[...] = jnp.dot(xn_ref[...], w_ref[...], preferred_element_type=F32)


def _norm_proj(x, g, w, layer, n, wdt=None, with_xn=False, tn=512):
    t, d = x.shape
    tm = _row_tile(t)
    with_dt = wdt is not None
    in_specs = [
        pl.BlockSpec((tm, d), lambda i, j: (i, 0)),
        pl.BlockSpec((1, d), lambda i, j: (0, 0)),
        pl.BlockSpec((None, d, tn), lambda i, j: (layer, 0, j)),
    ]
    args = [x, g.reshape(1, d), w]
    out_shape = [jax.ShapeDtypeStruct((t, n), F32)]
    out_specs = [pl.BlockSpec((tm, tn), lambda i, j: (i, j))]
    if with_dt:
        in_specs.append(pl.BlockSpec((None, d, LANE), lambda i, j: (layer, 0, 0)))
        args.append(wdt)
        out_shape.append(jax.ShapeDtypeStruct((t, LANE), F32))
        out_specs.append(pl.BlockSpec((tm, LANE), lambda i, j: (i, 0)))
    if with_xn:
        out_shape.append(jax.ShapeDtypeStruct((t, d), BF16))
        out_specs.append(pl.BlockSpec((tm, d), lambda i, j: (i, 0)))
    return pl.pallas_call(
        functools.partial(_norm_proj_kernel, with_dt=with_dt, with_xn=with_xn),
        grid=(t // tm, n // tn),
        in_specs=in_specs,
        out_specs=out_specs,
        out_shape=out_shape,
        scratch_shapes=[pltpu.VMEM((tm, d), BF16)],
        compiler_params=_cparams(("parallel", "arbitrary")),
        name="norm_proj",
    )(*args)


def _qk_norm_kernel(q_ref, k_ref, v_ref, qg_ref, kg_ref, vp_in, vs_in,
                    qn_ref, kb_ref, vb_ref, kf_ref, vp_ref, vs_ref, *, np_tiles, n_s):
    del vp_in, vs_in
    i = pl.program_id(0)
    lane = lax.broadcasted_iota(jnp.int32, (1, DH_V), 1)
    lo = lane < DH_QK

    def norm(x, g):
        xx = x * x
        s_lo = jnp.sum(jnp.where(lo, xx, 0.0), axis=-1, keepdims=True)
        s_all = jnp.sum(xx, axis=-1, keepdims=True)
        ms = jnp.where(lo, s_lo, s_all - s_lo) * (1.0 / DH_QK)
        return x * lax.rsqrt(ms + EPS) * g

    for h in range(H_A):
        sl = slice(h * DH_V, (h + 1) * DH_V)
        qn_ref[:, sl] = (norm(q_ref[:, sl], qg_ref[...]) * ATT_SCALE).astype(BF16)
        kn = norm(k_ref[:, sl], kg_ref[...])
        kf_ref[:, sl] = kn
        kb_ref[:, sl] = kn.astype(BF16)
    vb_ref[...] = v_ref[...].astype(BF16)

    @pl.when(i < np_tiles)
    def _():
        vp_ref[...] = v_ref[...]

    @pl.when(i == np_tiles)
    def _():
        vs_ref[...] = v_ref[0:n_s, :]


def _qk_norm(proj, q_gain, k_gain, layer, vp, vs):
    t = proj.shape[0]
    n_p, n_s = vp.shape[1], vs.shape[1]
    assert n_p % TM == 0 and n_s <= TM and t >= n_p + TM
    np_tiles = n_p // TM
    blk = lambda c: pl.BlockSpec((TM, D_ATT), lambda i, c=c: (i, c))
    gspec = pl.BlockSpec((1, DH_V), lambda i: (0, 0))
    hbm = pl.BlockSpec(memory_space=pl.ANY)
    p_out = pl.BlockSpec((None, TM, D_ATT), lambda i: (layer, jnp.minimum(i, np_tiles - 1), 0))
    s_out = pl.BlockSpec((None, n_s, D_ATT), lambda i: (layer, 0, 0))
    return pl.pallas_call(
        functools.partial(_qk_norm_kernel, np_tiles=np_tiles, n_s=n_s),
        grid=(t // TM,),
        in_specs=[blk(0), blk(1), blk(2), gspec, gspec, hbm, hbm],
        out_specs=[blk(0), blk(0), blk(0), blk(0), p_out, s_out],
        out_shape=[
            jax.ShapeDtypeStruct((t, D_ATT), BF16),
            jax.ShapeDtypeStruct((t, D_ATT), BF16),
            jax.ShapeDtypeStruct((t, D_ATT), BF16),
            jax.ShapeDtypeStruct((t, D_ATT), F32),
            jax.ShapeDtypeStruct(vp.shape, F32),
            jax.ShapeDtypeStruct(vs.shape, F32),
        ],
        input_output_aliases={5: 4, 6: 5},
        compiler_params=_cparams(("arbitrary",)),
        name="qk_norm",
    )(proj, proj, proj, q_gain.reshape(1, DH_V), k_gain.reshape(1, DH_V), vp, vs)


def _sub_norm(o, lam_ref, gsub):
    return _rms(o, gsub) * lam_ref[1]


def _attn_prompt_kernel(lam_ref, q_ref, k_ref, v_ref, gsub_ref, o_ref, *, tq, tk):
    qi = pl.program_id(2)
    q = q_ref[...]
    lane = lax.broadcasted_iota(jnp.int32, (1, DH_V), 1)
    zero = jnp.zeros((), BF16)
    q0 = jnp.where(lane < DH_QK, q, zero)
    q1 = jnp.where(lane >= DH_QK, q, zero)

    def block(start, mask):
        kb = k_ref[pl.ds(start, tk), :]
        s0 = lax.dot_general(q0, kb, _NT, preferred_element_type=F32)
        s1 = lax.dot_general(q1, kb, _NT, preferred_element_type=F32)
        if mask is not None:
            s0 = jnp.where(mask, s0, -jnp.inf)
            s1 = jnp.where(mask, s1, -jnp.inf)
        return s0, s1, v_ref[pl.ds(start, tk), :]

    def upd(s, vb, m, l, a):
        mn = jnp.maximum(m, jnp.max(s, axis=-1, keepdims=True))
        alpha = jnp.exp(m - mn)
        p = jnp.exp(s - mn)
        l = alpha * l + jnp.sum(p, axis=-1, keepdims=True)
        a = alpha * a + jnp.dot(p.astype(BF16), vb, preferred_element_type=F32)
        return mn, l, a

    row = jnp.right_shift(lax.broadcasted_iota(jnp.int32, (tq, tk), 0), CHUNK_SHIFT)
    col = jnp.right_shift(lax.broadcasted_iota(jnp.int32, (tq, tk), 1), CHUNK_SHIFT)
    base = pl.multiple_of(qi * tq, tq)
    s0, s1, vb = block(base, col <= row)
    m0 = jnp.max(s0, axis=-1, keepdims=True)
    m1 = jnp.max(s1, axis=-1, keepdims=True)
    p0 = jnp.exp(s0 - m0)
    p1 = jnp.exp(s1 - m1)
    l0 = jnp.sum(p0, axis=-1, keepdims=True)
    l1 = jnp.sum(p1, axis=-1, keepdims=True)
    a0 = jnp.dot(p0.astype(BF16), vb, preferred_element_type=F32)
    a1 = jnp.dot(p1.astype(BF16), vb, preferred_element_type=F32)
    for dblk in range(1, tq // tk):
        s0, s1, vb = block(base + dblk * tk, col + dblk * (tk // CHUNK) <= row)
        m0, l0, a0 = upd(s0, vb, m0, l0, a0)
        m1, l1, a1 = upd(s1, vb, m1, l1, a1)

    def body(j, carry):
        m0, l0, a0, m1, l1, a1 = carry
        s0, s1, vb = block(pl.multiple_of(j * tk, tk), None)
        m0, l0, a0 = upd(s0, vb, m0, l0, a0)
        m1, l1, a1 = upd(s1, vb, m1, l1, a1)
        return m0, l0, a0, m1, l1, a1

    m0, l0, a0, m1, l1, a1 = lax.fori_loop(0, qi * (tq // tk), body, (m0, l0, a0, m1, l1, a1))
    o = a0 / l0 - lam_ref[0] * (a1 / l1)
    o_ref[...] = _sub_norm(o, lam_ref, gsub_ref[...]).astype(BF16)


def _attn_prompt(lam, qn, kb, vb, g_sub, nb, seq, tq=512, tk=512):
    tq = min(tq, seq)
    tk = min(tk, tq)
    nq = seq // tq
    return pl.pallas_call(
        functools.partial(_attn_prompt_kernel, tq=tq, tk=tk),
        grid=(nb, H_A, nq),
        in_specs=[
            pl.BlockSpec(memory_space=pltpu.SMEM),
            pl.BlockSpec((tq, DH_V), lambda b, h, i: (b * nq + i, h)),
            pl.BlockSpec((seq, DH_V), lambda b, h, i: (b, h)),
            pl.BlockSpec((seq, DH_V), lambda b, h, i: (b, h)),
            pl.BlockSpec((1, DH_V), lambda b, h, i: (0, 0)),
        ],
        out_specs=pl.BlockSpec((tq, DH_V), lambda b, h, i: (b * nq + i, h)),
        out_shape=jax.ShapeDtypeStruct((nb * seq, D_ATT), BF16),
        compiler_params=_cparams(("parallel", "parallel", "arbitrary")),
        name="attn_prompt",
    )(lam, qn, kb, vb, g_sub.reshape(1, DH_V))


def _attn_sample_kernel(lam_ref, q_ref, kn_ref, vn_ref, ckt_ref, cv_ref, gsub_ref, o_ref, m_ref, l_ref, acc_ref, *, tk):
    c = pl.program_id(1)
    nq = q_ref.shape[0]
    lane = lax.broadcasted_iota(jnp.int32, (1, DH_V), 1)
    zero = jnp.zeros((), BF16)

    @pl.when(c == 0)
    def _():
        m_ref[...] = jnp.full(m_ref.shape, -jnp.inf, F32)
        l_ref[...] = jnp.zeros(l_ref.shape, F32)
        acc_ref[...] = jnp.zeros(acc_ref.shape, F32)

    def q01(h):
        q = q_ref[:, h * DH_V:(h + 1) * DH_V]
        return jnp.concatenate([jnp.where(lane < DH_QK, q, zero), jnp.where(lane >= DH_QK, q, zero)], axis=0)

    def update(h, s, v):
        m_old = m_ref[h]
        m_new = jnp.maximum(m_old, jnp.max(s, axis=-1, keepdims=True))
        alpha = jnp.exp(m_old - m_new)
        p = jnp.exp(s - m_new[:, 0:1])
        l_ref[h] = alpha * l_ref[h] + jnp.sum(p, axis=-1, keepdims=True)
        acc_ref[h] = alpha * acc_ref[h] + jnp.dot(p.astype(BF16), v, preferred_element_type=F32)
        m_ref[h] = m_new

    for h in range(H_A):
        kt = ckt_ref[h].reshape(2 * DH_QK, tk).astype(BF16)
        s = jnp.dot(q01(h), kt, preferred_element_type=F32)
        update(h, s, cv_ref[pl.ds(h, tk, stride=H_A), :].astype(BF16))

    @pl.when(c == pl.num_programs(1) - 1)
    def _():
        for h in range(H_A):
            sl = slice(h * DH_V, (h + 1) * DH_V)
            s = lax.dot_general(q01(h), kn_ref[:, sl], _NT, preferred_element_type=F32)
            update(h, s, vn_ref[:, sl])
            a = acc_ref[h] / l_ref[h]
            o = a[0:nq] - lam_ref[0] * a[nq:2 * nq]
            o_ref[:, sl] = _sub_norm(o, lam_ref, gsub_ref[...]).astype(BF16)


def _attn_sample(lam, qn, kb, vb, ckt, cv2, layer, g_sub, row0, nb, nq, tk=1024):
    past = ckt.shape[-1]
    tk = min(tk, past)
    nc = past // tk
    r0 = row0 // nq
    new = lambda: pl.BlockSpec((nq, D_ATT), lambda b, c: (r0 + b, 0))
    return pl.pallas_call(
        functools.partial(_attn_sample_kernel, tk=tk),
        grid=(nb, nc),
        in_specs=[pl.BlockSpec(memory_space=pltpu.SMEM), new(), new(), new(),
                  pl.BlockSpec((None, None, H_A, 2, DH_QK, tk), lambda b, c: (layer, b, 0, 0, 0, c)),
                  pl.BlockSpec((tk * H_A, DH_V), lambda b, c: ((layer * nb + b) * nc + c, 0)),
                  pl.BlockSpec((1, DH_V), lambda b, c: (0, 0))],
        out_specs=pl.BlockSpec((nq, D_ATT), lambda b, c: (b, 0)),
        out_shape=jax.ShapeDtypeStruct((nb * nq, D_ATT), BF16),
        scratch_shapes=[pltpu.VMEM((H_A, 2 * nq, DH_V), F32), pltpu.VMEM((H_A, 2 * nq, DH_V), F32),
                        pltpu.VMEM((H_A, 2 * nq, DH_V), F32)],
        compiler_params=_cparams(("parallel", "arbitrary")),
        name="attn_sample",
    )(lam, qn, kb, vb, ckt, cv2, g_sub.reshape(1, DH_V))


def _ssd_kernel(xs_ref, b_ref, c_ref, z_ref, dt_ref, prev_ref, h0_ref, cw_ref, cb_ref, dtb_ref,
                aneg_ref, dexp_ref, gssd_ref, expand_ref, y_ref, hout_ref, ext_ref, ht_ref, *, valid):
    q = SSD_Q
    c = pl.program_id(1)
    halo = 8

    @pl.when(c == 0)
    def _():
        ext_ref[0:halo, :] = prev_ref[...]
        ht_ref[...] = h0_ref[...].T

    ext_ref[halo:halo + q, 0:D_SSM] = xs_ref[...]
    ext_ref[halo:halo + q, D_SSM:D_SSM + SSM_G * SSM_N] = b_ref[...]
    ext_ref[halo:halo + q, D_SSM + SSM_G * SSM_N:CONV_DIM] = c_ref[...]
    conv = cb_ref[...]
    for j in range(CONV_W):
        r0 = halo - (CONV_W - 1) + j
        conv = conv + ext_ref[r0:r0 + q, :] * cw_ref[j:j + 1, :]
    tail = ext_ref[q:q + halo, :]
    ext_ref[0:halo, :] = tail
    xbc = conv * jax.nn.sigmoid(conv)
    xs = xbc[:, 0:D_SSM]
    bm = xbc[:, D_SSM:D_SSM + SSM_G * SSM_N]
    cm = xbc[:, D_SSM + SSM_G * SSM_N:CONV_DIM]

    dt = jax.nn.softplus(dt_ref[...] + dtb_ref[...])
    if valid < q:
        rowv = lax.broadcasted_iota(jnp.int32, (q, LANE), 0) < valid
        dt = jnp.where(rowv, dt, 0.0)
    a = dt * aneg_ref[...]
    r_i = lax.broadcasted_iota(jnp.int32, (q, q), 0)
    c_i = lax.broadcasted_iota(jnp.int32, (q, q), 1)
    tril = c_i <= r_i
    acum = jnp.dot(tril.astype(F32), a, precision=_HI, preferred_element_type=F32)
    eye = (r_i == c_i).astype(F32)
    acum_t = lax.dot_general(eye, acum, _NT, precision=_HI, preferred_element_type=F32)
    a_last = acum[q - 1:q, :]
    decay_in = jnp.exp(a_last - acum)
    stack = jnp.concatenate(
        [dt, dt * decay_in, jnp.exp(acum), jnp.broadcast_to(jnp.exp(a_last), (8, LANE))], axis=0)
    ex = jnp.dot(stack, expand_ref[...], precision=_HI, preferred_element_type=F32)
    dt_e = ex[0:q]
    dtd_e = ex[q:2 * q]
    eacum_e = ex[2 * q:3 * q]
    cd_e = ex[3 * q:3 * q + 1]

    xdt = (xs * dt_e).astype(BF16)
    xdtd = (xs * dtd_e).astype(BF16)
    lane = lax.broadcasted_iota(jnp.int32, (1, LANE), 1)
    zero = jnp.zeros((), BF16)
    gw = D_SSM // SSM_G
    hpg = H_S // SSM_G
    y_diag_parts = []
    y_off_parts = []
    for g in range(SSM_G):
        gs = slice(g * gw, (g + 1) * gw)
        bg = bm[:, g * SSM_N:(g + 1) * SSM_N]
        cg = cm[:, g * SSM_N:(g + 1) * SSM_N].astype(BF16)
        cb = lax.dot_general(cg, bg.astype(BF16), _NT, preferred_element_type=F32)
        ht_prev = ht_ref[:, gs]
        y_off = jnp.dot(cg, ht_prev.astype(BF16), preferred_element_type=F32)
        ht_ref[:, gs] = cd_e[:, gs] * ht_prev + jnp.dot(
            bg.T.astype(BF16), xdtd[:, gs], preferred_element_type=F32)
        for k in range(hpg // 2):
            ms = []
            for h in (g * hpg + 2 * k, g * hpg + 2 * k + 1):
                seg = acum[:, h:h + 1] - acum_t[h:h + 1, :]
                ms.append((cb * jnp.exp(jnp.where(tril, seg, -jnp.inf))).astype(BF16))
            pair = slice((g * hpg + 2 * k) * SSM_P, (g * hpg + 2 * k + 2) * SSM_P)
            xp = xdt[:, pair]
            rhs = jnp.concatenate([jnp.where(lane < SSM_P, xp, zero), jnp.where(lane >= SSM_P, xp, zero)], axis=0)
            y_diag_parts.append(jnp.dot(jnp.concatenate(ms, axis=1), rhs, preferred_element_type=F32))
        y_off_parts.append(y_off)
    y_diag = jnp.concatenate(y_diag_parts, axis=1)
    y_off = jnp.concatenate(y_off_parts, axis=1)
    y = y_diag + y_off * eacum_e + dexp_ref[...] * xs
    zz = z_ref[...]
    y = y * (zz * jax.nn.sigmoid(zz))
    for g in range(SSM_G):
        gs = slice(g * gw, (g + 1) * gw)
        y_ref[:, gs] = _rms(y[:, gs], gssd_ref[:, gs]).astype(BF16)

    @pl.when(c == pl.num_programs(1) - 1)
    def _():
        hout_ref[...] = ht_ref[...].T


def _ssd(proj, dtp, prev, h0, lw, nb, seq, valid):
    nc = seq // SSD_Q
    row = lambda w, col: pl.BlockSpec((SSD_Q, w), lambda b, c, col=col: (b * nc + c, col))
    const = lambda shp: pl.BlockSpec(shp, lambda b, c: (0,) * len(shp))
    per_b = lambda shp: pl.BlockSpec((None,) + shp, lambda b, c: (b, 0, 0))
    return pl.pallas_call(
        functools.partial(_ssd_kernel, valid=valid),
        grid=(nb, nc),
        in_specs=[
            row(D_SSM, 4), row(SSM_G * SSM_N, 20), row(SSM_G * SSM_N, 21), row(D_SSM, 3), row(LANE, 0),
            per_b((8, CONV_DIM)), per_b((D_SSM, SSM_N)),
            const((CONV_W, CONV_DIM)), const((1, CONV_DIM)), const((1, LANE)), const((1, LANE)),
            const((1, D_SSM)), const((1, D_SSM)), const((LANE, D_SSM)),
        ],
        out_specs=[row(D_SSM, 0), per_b((D_SSM, SSM_N))],
        out_shape=[jax.ShapeDtypeStruct((nb * seq, D_SSM), BF16),
                   jax.ShapeDtypeStruct((nb, D_SSM, SSM_N), F32)],
        scratch_shapes=[pltpu.VMEM((SSD_Q + 8, CONV_DIM), F32), pltpu.VMEM((SSM_N, D_SSM), F32)],
        compiler_params=_cparams(("parallel", "arbitrary")),
        name="ssd",
    )(proj, proj, proj, proj, dtp, prev, h0, lw["conv_w"], lw["conv_b"], lw["dt_bias"], lw["a_neg"],
      lw["d_exp"], lw["g_ssd"], lw["expand"])


def _out_proj_kernel(y_ref, o_ref, w_ref, h_ref, out_ref):
    acc = jnp.dot(y_ref[...], w_ref[0:D_SSM, :], preferred_element_type=F32)
    acc = acc + jnp.dot(o_ref[...], w_ref[D_SSM:D_SSM + D_ATT, :], preferred_element_type=F32)
    out_ref[...] = h_ref[...] + acc


def _out_proj(y, o, w, layer, h, tn=512):
    t, d = h.shape
    tm = _row_tile(t)
    return pl.pallas_call(
        _out_proj_kernel,
        grid=(t // tm, d // tn),
        in_specs=[
            pl.BlockSpec((tm, D_SSM), lambda i, j: (i, 0)),
            pl.BlockSpec((tm, D_ATT), lambda i, j: (i, 0)),
            pl.BlockSpec((None, D_SSM + D_ATT, tn), lambda i, j: (layer, 0, j)),
            pl.BlockSpec((tm, tn), lambda i, j: (i, j)),
        ],
        out_specs=pl.BlockSpec((tm, tn), lambda i, j: (i, j)),
        out_shape=jax.ShapeDtypeStruct((t, d), F32),
        compiler_params=_cparams(("parallel", "parallel")),
        name="out_proj",
    )(y, o, w, h)


def _top16(s, n_out=None):
    n, t = s.shape
    iota = lax.broadcasted_iota(jnp.int32, (n, t), 0)
    row16 = lax.broadcasted_iota(jnp.int32, (PEER_TOPK, t), 0)
    rank = jnp.full((n, t), PEER_TOPK, jnp.int32)
    vals = jnp.zeros((PEER_TOPK, t), F32)
    cnt = jnp.zeros((n_out, t), jnp.int32) if n_out else None
    for r in range(PEER_TOPK):
        m = jnp.max(s, axis=0, keepdims=True)
        idx = jnp.min(jnp.where(s == m, iota, n), axis=0, keepdims=True)
        sel = iota == idx
        rank = jnp.where(sel, r, rank)
        s = jnp.where(sel, -jnp.inf, s)
        vals = jnp.where(row16 == r, m, vals)
        if n_out:
            cnt = cnt + (row16 == jnp.right_shift(idx, TOPK_SHIFT)).astype(jnp.int32)
    return vals, rank, cnt


_MARK = 2.0 ** 126


def _peel16(s):
    t = s.shape[1]
    row16 = lax.broadcasted_iota(jnp.int32, (PEER_TOPK, t), 0)
    vals = jnp.zeros((PEER_TOPK, t), F32)
    for r in range(PEER_TOPK):
        m = jnp.max(s, axis=0, keepdims=True)
        s = jnp.where(s == m, -(1.0 + (r + 1) / 32.0) * _MARK, s)
        vals = jnp.where(row16 == r, m, vals)
    marked = s < -_MARK
    rank = jnp.where(marked, s * (-32.0 / _MARK) - 33.0, float(PEER_TOPK))
    count = jnp.sum(jnp.where(marked, 1.0, 0.0), axis=0, keepdims=True)
    return vals, rank, count


def _route_outputs(s1, s2, v1, v2, rank1, rank2, cnt, zsum):
    n1 = jnp.zeros(s1.shape, F32)
    for j in range(PEER_TOPK):
        n1 = jnp.where(rank1 == j, cnt[j:j + 1, :], n1)
    e1 = jnp.where(rank1 < PEER_TOPK, jnp.exp(s1 - v1[0:1, :]) / zsum, 0.0)
    e2 = jnp.where(rank2 < PEER_TOPK, jnp.exp(s2 - v2[0:1, :]), 0.0)
    return e1, n1, e2.astype(BF16), rank2.astype(BF16)


def _route_exact(s1, s2):
    v1, rank1, _ = _top16(s1)
    v2, rank2, _ = _top16(s2)
    cand = jnp.concatenate([v1[j:j + 1, :] + v2 for j in range(PEER_TOPK)], axis=0)
    sc, _, cnt = _top16(cand, n_out=PEER_TOPK)
    zsum = jnp.sum(jnp.exp(sc - sc[0:1, :]), axis=0, keepdims=True)
    return _route_outputs(s1, s2, v1, v2, rank1.astype(F32), rank2.astype(F32), cnt.astype(F32), zsum)


def _route_fast(s1, s2):
    half = PEER_TOPK // 2
    t = s1.shape[1]
    v1, rank1, c1 = _peel16(s1)
    v2, rank2, c2 = _peel16(s2)
    pieces = [v1[j:j + 1, :] + v2[0:half, :] for j in range(half)]
    pieces.append(v1[half:, :] + v2[0:1, :])
    pieces.append(v1[0:1, :] + v2[half:, :])
    cand = jnp.concatenate(pieces, axis=0)
    sc, rankc, cc = _peel16(cand)
    sel = rankc < PEER_TOPK
    zsum = jnp.sum(jnp.where(sel, jnp.exp(cand - sc[0:1, :]), 0.0), axis=0, keepdims=True)
    mk = jnp.where(sel, 1.0, 0.0)
    row8 = lax.broadcasted_iota(jnp.int32, (half, t), 0)
    n_lo = jnp.zeros((half, t), F32)
    for j in range(half):
        rs = jnp.sum(mk[j * half:(j + 1) * half, :], axis=0, keepdims=True)
        if j == 0:
            rs = rs + jnp.sum(mk[(half + 1) * half:, :], axis=0, keepdims=True)
        n_lo = jnp.where(row8 == j, rs, n_lo)
    cnt = jnp.concatenate([n_lo, mk[half * half:(half + 1) * half, :]], axis=0)
    bad = jnp.abs(c1 - PEER_TOPK) + jnp.abs(c2 - PEER_TOPK) + jnp.abs(cc - PEER_TOPK)
    return _route_outputs(s1, s2, v1, v2, rank1, rank2, cnt, zsum), bad


def _route_kernel(q_ref, k1_ref, k2_ref, e1_ref, n1_ref, e2_ref, r2_ref, *, tm):
    def store(ls, outs):
        e1_ref[:, ls], n1_ref[:, ls], e2_ref[:, ls], r2_ref[:, ls] = outs

    def scores(ls):
        qb = q_ref[ls, :].astype(BF16)
        s1 = lax.dot_general(k1_ref[...], qb[:, 0:N_KEYS], _NT, preferred_element_type=F32)
        s2 = lax.dot_general(k2_ref[...], qb[:, N_KEYS:2 * N_KEYS], _NT, preferred_element_type=F32)
        return s1, s2

    group = 4
    for g0 in range(0, tm // LANE, group):
        tiles = [slice(lt * LANE, (lt + 1) * LANE) for lt in range(g0, min(g0 + group, tm // LANE))]
        flags = []
        for ls in tiles:
            outs, bad = _route_fast(*scores(ls))
            store(ls, outs)
            flags.append(jnp.max(bad) > 0.0)
        for ls, flag in zip(tiles, flags):
            @pl.when(flag)
            def _(ls=ls):
                store(ls, _route_exact(*scores(ls)))


def _route(q, k1, k2, layer, tm=TM):
    t = q.shape[0]
    kd = k1.shape[-1]
    out = lambda: pl.BlockSpec((None, N_KEYS, tm), lambda i, h: (h, 0, i))
    key = lambda: pl.BlockSpec((None, None, N_KEYS, kd), lambda i, h: (layer, h, 0, 0))
    return pl.pallas_call(
        functools.partial(_route_kernel, tm=tm),
        grid=(t // tm, PEER_HEADS),
        in_specs=[pl.BlockSpec((tm, 2 * kd), lambda i, h: (i, h)), key(), key()],
        out_specs=[out(), out(), out(), out()],
        out_shape=[jax.ShapeDtypeStruct((PEER_HEADS, N_KEYS, t), F32),
                   jax.ShapeDtypeStruct((PEER_HEADS, N_KEYS, t), F32),
                   jax.ShapeDtypeStruct((PEER_HEADS, N_KEYS, t), BF16),
                   jax.ShapeDtypeStruct((PEER_HEADS, N_KEYS, t), BF16)],
        compiler_params=_cparams(("parallel", "parallel")),
        name="peer_route",
    )(q, k1, k2)


def _peer_kernel(x_ref, u_ref, vt_ref, e1_ref, n1_ref, e2_ref, r2_ref, h_ref, out_ref, acc_ref, *, te):
    j = pl.program_id(1)

    @pl.when(j == 0)
    def _():
        acc_ref[...] = jnp.zeros_like(acc_ref)

    at = lax.dot_general(u_ref[...], x_ref[...], _NT, preferred_element_type=F32)
    act = (0.5 * at * (1.0 + lax.erf(at * (2.0 ** -0.5)))).astype(BF16)
    zero = jnp.zeros((), BF16)
    parts = []
    for c in range(te // N_KEYS):
        w = None
        for h in range(PEER_HEADS):
            n1 = n1_ref[h, c:c + 1, :].astype(BF16)
            e1 = e1_ref[h, c:c + 1, :].astype(BF16)
            wh = jnp.where(r2_ref[h] < n1, e1 * e2_ref[h], zero)
            w = wh if w is None else w + wh
        parts.append(act[c * N_KEYS:(c + 1) * N_KEYS, :] * w)
    ht = jnp.concatenate(parts, axis=0)
    acc_ref[...] += jnp.dot(vt_ref[...], ht, preferred_element_type=F32)

    @pl.when(j == pl.num_programs(1) - 1)
    def _():
        out_ref[...] = h_ref[...] + acc_ref[...].T


def _peer(x, u, vt, layer, e1, n1, e2, r2, h, te=1024):
    t, d = h.shape
    ne = u.shape[1]
    rows = te // N_KEYS
    return pl.pallas_call(
        functools.partial(_peer_kernel, te=te),
        grid=(t // TM, ne // te),
        in_specs=[
            pl.BlockSpec((TM, d), lambda i, j: (i, 0)),
            pl.BlockSpec((None, te, d), lambda i, j: (layer, j, 0)),
            pl.BlockSpec((None, d, te), lambda i, j: (layer, 0, j)),
            pl.BlockSpec((PEER_HEADS, rows, TM), lambda i, j: (0, j, i)),
            pl.BlockSpec((PEER_HEADS, rows, TM), lambda i, j: (0, j, i)),
            pl.BlockSpec((PEER_HEADS, N_KEYS, TM), lambda i, j: (0, 0, i)),
            pl.BlockSpec((PEER_HEADS, N_KEYS, TM), lambda i, j: (0, 0, i)),
            pl.BlockSpec((TM, d), lambda i, j: (i, 0)),
        ],
        out_specs=pl.BlockSpec((TM, d), lambda i, j: (i, 0)),
        out_shape=jax.ShapeDtypeStruct((t, d), F32),
        scratch_shapes=[pltpu.VMEM((d, TM), F32)],
        compiler_params=_cparams(("parallel", "arbitrary")),
        name="peer_experts",
    )(x, u, vt, e1, n1, e2, r2, h)


def _ple_kernel(h_ref, g_ref, p_ref, wg_ref, wp_ref, hc_ref, out_ref, xn_ref):
    @pl.when(pl.program_id(1) == 0)
    def _():
        xn_ref[...] = _rms(h_ref[...], g_ref[...]).astype(BF16)

    gate = jax.nn.sigmoid(jnp.dot(xn_ref[...], wg_ref[...], preferred_element_type=F32))
    emb = jnp.dot(p_ref[...].astype(BF16), wp_ref[...], preferred_element_type=F32)
    out_ref[...] = hc_ref[...] + emb * gate


def _ple(h, g, p, wg, wp, layer, tn=512):
    t, d = h.shape
    pd = p.shape[2]
    tm = _row_tile(t)
    return pl.pallas_call(
        _ple_kernel,
        grid=(t // tm, d // tn),
        in_specs=[
            pl.BlockSpec((tm, d), lambda i, j: (i, 0)),
            pl.BlockSpec((1, d), lambda i, j: (0, 0)),
            pl.BlockSpec((None, tm, pd), lambda i, j: (layer, i, 0)),
            pl.BlockSpec((None, d, tn), lambda i, j: (layer, 0, j)),
            pl.BlockSpec((None, pd, tn), lambda i, j: (layer, 0, j)),
            pl.BlockSpec((tm, tn), lambda i, j: (i, j)),
        ],
        out_specs=pl.BlockSpec((tm, tn), lambda i, j: (i, j)),
        out_shape=jax.ShapeDtypeStruct((t, d), F32),
        scratch_shapes=[pltpu.VMEM((tm, d), BF16)],
        compiler_params=_cparams(("parallel", "arbitrary")),
        name="ple_gate",
    )(h, g.reshape(1, d), p, wg, wp, h)


def _lambda_init(layer_idx):
    return 0.8 - 0.6 * math.exp(-0.3 * layer_idx)


def _pad_rows(x, rows):
    return jnp.pad(x, ((0, rows - x.shape[0]),) + ((0, 0),) * (x.ndim - 1))


def kernel(x_prompt, x_sample, cache_k, cache_v, state_ssm, state_conv, p_prompt, p_sample, g_mix, w_in,
           conv_w, conv_b, dt_bias, a_log, d_skip, g_ssd, q_gain, k_gain, lam_q1, lam_k1, lam_q2, lam_k2,
           g_sub, w_out, g_ffn, peer_wq, peer_k1, peer_k2, peer_u, peer_v, g_ple, w_ple, w_pgate):
    nbp, seq, d = x_prompt.shape
    nbs, dseq, _ = x_sample.shape
    depth = w_in.shape[0]
    past = cache_k.shape[2]
    n_p = nbp * seq
    n_s = nbs * dseq
    t_pad = -(-(n_p + n_s) // TM) * TM

    h = _pad_rows(jnp.concatenate([x_prompt.reshape(n_p, d), x_sample.reshape(n_s, d)], axis=0), t_pad)
    ple_dim = p_prompt.shape[-1]
    p_all = jnp.concatenate([p_prompt.reshape(depth, n_p, ple_dim), p_sample.reshape(depth, n_s, ple_dim)], axis=1)
    p_all = jnp.pad(p_all, ((0, 0), (0, t_pad - n_p - n_s), (0, 0)))
    ck = jnp.transpose(cache_k, (0, 1, 3, 4, 5, 2))
    cv = cache_v.reshape(depth * nbs * past * H_A, DH_V)
    expand = (jnp.arange(LANE)[:, None] == jnp.arange(D_SSM)[None, :] // SSM_P).astype(F32)
    prev_zero = jnp.zeros((nbp, 8, CONV_DIM), F32)
    h0_zero = jnp.zeros((nbp, D_SSM, SSM_N), F32)

    w_main = w_in.astype(BF16)
    w_dt = jnp.pad(w_in[:, :, MAIN_DIM:], ((0, 0), (0, 0), (0, LANE - H_S))).astype(BF16)
    w_out_b = w_out.astype(BF16)
    wq_b = peer_wq.astype(BF16)
    k1_b = peer_k1.astype(BF16)
    k2_b = peer_k2.astype(BF16)
    u_b = peer_u.astype(BF16)
    vt_b = jnp.swapaxes(peer_v, 1, 2).astype(BF16)
    wg_b = w_pgate.astype(BF16)
    wp_b = w_ple.astype(BF16)

    vp_all = jnp.zeros((depth, n_p, D_ATT), F32)
    vs_all = jnp.zeros((depth, n_s, D_ATT), F32)

    outs = [[] for _ in range(6)]
    for l in range(depth):
        lam0 = _lambda_init(l)
        lam = (jnp.exp(jnp.sum(lam_q1[l] * lam_k1[l])) - jnp.exp(jnp.sum(lam_q2[l] * lam_k2[l])) + lam0)
        lam_s = jnp.stack([lam, jnp.asarray(1.0 - lam0, F32)]).astype(F32)
        lw = dict(
            conv_w=conv_w[l], conv_b=conv_b[l].reshape(1, CONV_DIM),
            dt_bias=_pad_rows(dt_bias[l], LANE).reshape(1, LANE),
            a_neg=_pad_rows(-jnp.exp(a_log[l]), LANE).reshape(1, LANE),
            d_exp=jnp.repeat(d_skip[l], SSM_P).reshape(1, D_SSM),
            g_ssd=g_ssd[l].reshape(1, D_SSM), expand=expand)

        proj, dtp = _norm_proj(h, g_mix[l], w_main, l, MAIN_DIM, wdt=w_dt)
        qn, kb, vb, kf, vp_all, vs_all = _qk_norm(proj, q_gain[l], k_gain[l], l, vp_all, vs_all)
        o_p = _attn_prompt(lam_s, qn, kb, vb, g_sub[l], nbp, seq)
        o_s = _attn_sample(lam_s, qn, kb, vb, ck, cv, l, g_sub[l], n_p, nbs, dseq)
        y_p, hT_p = _ssd(proj, dtp, prev_zero, h0_zero, lw, nbp, seq, SSD_Q)
        proj_s = proj[n_p:n_p + n_s].reshape(nbs, dseq, MAIN_DIM)
        pad_seq = lambda a: jnp.pad(a, ((0, 0), (0, SSD_Q - dseq), (0, 0))).reshape(nbs * SSD_Q, a.shape[-1])
        prev_s = jnp.pad(state_conv[l], ((0, 0), (8 - (CONV_W - 1), 0), (0, 0)))
        y_s, hT_s = _ssd(pad_seq(proj_s), pad_seq(dtp[n_p:n_p + n_s].reshape(nbs, dseq, LANE)), prev_s,
                         state_ssm[l].reshape(nbs, D_SSM, SSM_N), lw, nbs, SSD_Q, dseq)
        y_s = y_s.reshape(nbs, SSD_Q, D_SSM)[:, :dseq].reshape(n_s, D_SSM)
        y_all = _pad_rows(jnp.concatenate([y_p, y_s], axis=0), t_pad)
        o_all = _pad_rows(jnp.concatenate([o_p, o_s], axis=0), t_pad)
        h = _out_proj(y_all, o_all, w_out_b, l, h)

        q_peer, c = _norm_proj(h, g_ffn[l], wq_b, l, d, with_xn=True)
        e1, n1, e2, r2 = _route(q_peer, k1_b, k2_b, l)
        h = _peer(c, u_b, vt_b, l, e1, n1, e2, r2, h)

        h = _ple(h, g_ple[l], p_all, wg_b, wp_b, l)

        xbc0 = 2 * D_ATT + D_ATT + D_SSM
        tail = CONV_W - 1
        outs[0].append(hT_p.reshape(nbp, H_S, SSM_P, SSM_N))
        outs[1].append(jnp.stack([proj[(b + 1) * seq - tail:(b + 1) * seq, xbc0:] for b in range(nbp)]))
        outs[2].append(hT_s.reshape(nbs, H_S, SSM_P, SSM_N))
        outs[3].append(proj_s[:, dseq - tail:, xbc0:])
        outs[4].append(kf[:n_p].reshape(nbp, seq, H_A, 2, DH_QK))
        outs[5].append(kf[n_p:n_p + n_s].reshape(nbs, dseq, H_A, 2, DH_QK))

    y_prompt = h[:n_p].reshape(nbp, seq, d)
    y_sample = h[n_p:n_p + n_s].reshape(nbs, dseq, d)
    ssm_p, conv_p, ssm_s, conv_s, k_p, k_s = (jnp.stack(o) for o in outs)
    return (y_prompt, y_sample,
            k_p, vp_all.reshape(depth, nbp, seq, H_A, DH_V), ssm_p, conv_p,
            k_s, vs_all.reshape(depth, nbs, dseq, H_A, DH_V), ssm_s, conv_s)
```

```python
import functools
import math

import jax
import jax.numpy as jnp
from jax import lax
from jax.experimental import pallas as pl
from jax.experimental.pallas import tpu as pltpu

F32 = jnp.float32
BF16 = jnp.bfloat16

EPS = 1e-6
CHUNK = 64
CHUNK_SHIFT = 6
H_A = 8
DH_QK = 64
DH_V = 128
D_ATT = H_A * DH_V
ATT_SCALE = DH_QK ** -0.5
SSM_P = 64
H_S = 16
SSM_N = 128
SSM_G = 2
D_SSM = H_S * SSM_P
CONV_W = 4
CONV_DIM = D_SSM + 2 * SSM_G * SSM_N
MAIN_DIM = 2 * D_ATT + D_ATT + D_SSM + CONV_DIM
N_KEYS = 128
PEER_HEADS = 8
PEER_TOPK = 16
TOPK_SHIFT = 4

LANE = 128
TM = 512
MM_ROWS = 1152
SSD_Q = 128
VMEM_LIMIT = 56 * 1024 * 1024

_NT = (((1,), (1,)), ((), ()))


def _cparams(sem):
    return pltpu.CompilerParams(dimension_semantics=sem, vmem_limit_bytes=VMEM_LIMIT)


def _row_tile(t):
    for k in range(1, t // 16 + 1):
        if t % k == 0 and (t // k) % 16 == 0 and t // k <= MM_ROWS:
            return t // k
    return TM


def _rms(x, g):
    ms = jnp.mean(x * x, axis=-1, keepdims=True)
    return x * lax.rsqrt(ms + EPS) * g


def _norm_proj_kernel(x_ref, g_ref, w_ref, *rest, with_dt, with_xn):
    rest = list(rest)
    wdt_ref = rest.pop(0) if with_dt else None
    o_ref = rest.pop(0)
    odt_ref = rest.pop(0) if with_dt else None
    oxn_ref = rest.pop(0) if with_xn else None
    xn_ref = rest.pop(0)

    @pl.when(pl.program_id(1) == 0)
    def _():
        xn = _rms(x_ref[...], g_ref[...]).astype(BF16)
        xn_ref[...] = xn
        if with_xn:
            oxn_ref[...] = xn
        if with_dt:
            odt_ref[...] = jnp.dot(xn, wdt_ref[...].astype(BF16), preferred_element_type=F32)

    o_ref[...] = jnp.dot(xn_ref[...], w_ref[...].astype(BF16), preferred_element_type=F32)


def _norm_proj(x, g, w, layer, n, wdt=None, with_xn=False, tn=512):
    t, d = x.shape
    tm = _row_tile(t)
    with_dt = wdt is not None
    in_specs = [
        pl.BlockSpec((tm, d), lambda i, j: (i, 0)),
        pl.BlockSpec((1, d), lambda i, j: (0, 0)),
        pl.BlockSpec((None, d, tn), lambda i, j: (layer, 0, j)),
    ]
    args = [x, g.reshape(1, d), w]
    out_shape = [jax.ShapeDtypeStruct((t, n), F32)]
    out_specs = [pl.BlockSpec((tm, tn), lambda i, j: (i, j))]
    if with_dt:
        in_specs.append(pl.BlockSpec((None, d, LANE), lambda i, j: (layer, 0, 0)))
        args.append(wdt)
        out_shape.append(jax.ShapeDtypeStruct((t, LANE), F32))
        out_specs.append(pl.BlockSpec((tm, LANE), lambda i, j: (i, 0)))
    if with_xn:
        out_shape.append(jax.ShapeDtypeStruct((t, d), BF16))
        out_specs.append(pl.BlockSpec((tm, d), lambda i, j: (i, 0)))
    return pl.pallas_call(
        functools.partial(_norm_proj_kernel, with_dt=with_dt, with_xn=with_xn),
        grid=(t // tm, n // tn),
        in_specs=in_specs,
        out_specs=out_specs,
        out_shape=out_shape,
        scratch_shapes=[pltpu.VMEM((tm, d), BF16)],
        compiler_params=_cparams(("parallel", "arbitrary")),
        name="norm_proj",
    )(*args)


def _qk_norm_kernel(q_ref, k_ref, v_ref, qg_ref, kg_ref, vp_in, vs_in,
                    qn_ref, kb_ref, vb_ref, kf_ref, vp_ref, vs_ref, *, np_tiles, n_s):
    del vp_in, vs_in
    i = pl.program_id(0)
    lane = lax.broadcasted_iota(jnp.int32, (1, DH_V), 1)
    lo = lane < DH_QK

    def norm(x, g):
        xx = x * x
        s_lo = jnp.sum(jnp.where(lo, xx, 0.0), axis=-1, keepdims=True)
        s_all = jnp.sum(xx, axis=-1, keepdims=True)
        ms = jnp.where(lo, s_lo, s_all - s_lo) * (1.0 / DH_QK)
        return x * lax.rsqrt(ms + EPS) * g

    for h in range(H_A):
        sl = slice(h * DH_V, (h + 1) * DH_V)
        qn_ref[:, sl] = (norm(q_ref[:, sl], qg_ref[...]) * ATT_SCALE).astype(BF16)
        kn = norm(k_ref[:, sl], kg_ref[...])
        kf_ref[:, sl] = kn
        kb_ref[:, sl] = kn.astype(BF16)
    vb_ref[...] = v_ref[...].astype(BF16)

    @pl.when(i < np_tiles)
    def _():
        vp_ref[...] = v_ref[...]

    @pl.when(i == np_tiles)
    def _():
        vs_ref[...] = v_ref[0:n_s, :]


def _qk_norm(proj, q_gain, k_gain, layer, vp, vs):
    t = proj.shape[0]
    n_p, n_s = vp.shape[1], vs.shape[1]
    assert n_p % TM == 0 and n_s <= TM and t >= n_p + TM
    np_tiles = n_p // TM
    blk = lambda c: pl.BlockSpec((TM, D_ATT), lambda i, c=c: (i, c))
    gspec = pl.BlockSpec((1, DH_V), lambda i: (0, 0))
    hbm = pl.BlockSpec(memory_space=pl.ANY)
    p_out = pl.BlockSpec((None, TM, D_ATT), lambda i: (layer, jnp.minimum(i, np_tiles - 1), 0))
    s_out = pl.BlockSpec((None, n_s, D_ATT), lambda i: (layer, 0, 0))
    return pl.pallas_call(
        functools.partial(_qk_norm_kernel, np_tiles=np_tiles, n_s=n_s),
        grid=(t // TM,),
        in_specs=[blk(0), blk(1), blk(2), gspec, gspec, hbm, hbm],
        out_specs=[blk(0), blk(0), blk(0), blk(0), p_out, s_out],
        out_shape=[
            jax.ShapeDtypeStruct((t, D_ATT), BF16),
            jax.ShapeDtypeStruct((t, D_ATT), BF16),
            jax.ShapeDtypeStruct((t, D_ATT), BF16),
            jax.ShapeDtypeStruct((t, D_ATT), F32),
            jax.ShapeDtypeStruct(vp.shape, F32),
            jax.ShapeDtypeStruct(vs.shape, F32),
        ],
        input_output_aliases={5: 4, 6: 5},
        compiler_params=_cparams(("arbitrary",)),
        name="qk_norm",
    )(proj, proj, proj, q_gain.reshape(1, DH_V), k_gain.reshape(1, DH_V), vp, vs)


def _sub_norm(o, lam_ref, gsub):
    return _rms(o, gsub) * lam_ref[1]


def _attn_prompt_kernel(lam_ref, q_ref, k_ref, v_ref, gsub_ref, o_ref, *, tq, tk):
    qi = pl.program_id(2)
    q = q_ref[...]
    lane = lax.broadcasted_iota(jnp.int32, (1, DH_V), 1)
    zero = jnp.zeros((), BF16)
    q0 = jnp.where(lane < DH_QK, q, zero)
    q1 = jnp.where(lane >= DH_QK, q, zero)

    def block(start, mask):
        kb = k_ref[pl.ds(start, tk), :]
        s0 = lax.dot_general(q0, kb, _NT, preferred_element_type=F32)
        s1 = lax.dot_general(q1, kb, _NT, preferred_element_type=F32)
        if mask is not None:
            s0 = jnp.where(mask, s0, -jnp.inf)
            s1 = jnp.where(mask, s1, -jnp.inf)
        return s0, s1, v_ref[pl.ds(start, tk), :]

    def upd(s, vb, m, l, a):
        mn = jnp.maximum(m, jnp.max(s, axis=-1, keepdims=True))
        alpha = jnp.exp(m - mn)
        p = jnp.exp(s - mn)
        l = alpha * l + jnp.sum(p, axis=-1, keepdims=True)
        a = alpha * a + jnp.dot(p.astype(BF16), vb, preferred_element_type=F32)
        return mn, l, a

    row = jnp.right_shift(lax.broadcasted_iota(jnp.int32, (tq, tk), 0), CHUNK_SHIFT)
    col = jnp.right_shift(lax.broadcasted_iota(jnp.int32, (tq, tk), 1), CHUNK_SHIFT)
    base = pl.multiple_of(qi * tq, tq)
    s0, s1, vb = block(base, col <= row)
    m0 = jnp.max(s0, axis=-1, keepdims=True)
    m1 = jnp.max(s1, axis=-1, keepdims=True)
    p0 = jnp.exp(s0 - m0)
    p1 = jnp.exp(s1 - m1)
    l0 = jnp.sum(p0, axis=-1, keepdims=True)
    l1 = jnp.sum(p1, axis=-1, keepdims=True)
    a0 = jnp.dot(p0.astype(BF16), vb, preferred_element_type=F32)
    a1 = jnp.dot(p1.astype(BF16), vb, preferred_element_type=F32)
    for dblk in range(1, tq // tk):
        s0, s1, vb = block(base + dblk * tk, col + dblk * (tk // CHUNK) <= row)
        m0, l0, a0 = upd(s0, vb, m0, l0, a0)
        m1, l1, a1 = upd(s1, vb, m1, l1, a1)

    def body(j, carry):
        m0, l0, a0, m1, l1, a1 = carry
        s0, s1, vb = block(pl.multiple_of(j * tk, tk), None)
        m0, l0, a0 = upd(s0, vb, m0, l0, a0)
        m1, l1, a1 = upd(s1, vb, m1, l1, a1)
        return m0, l0, a0, m1, l1, a1

    m0, l0, a0, m1, l1, a1 = lax.fori_loop(0, qi * (tq // tk), body, (m0, l0, a0, m1, l1, a1))
    o = a0 / l0 - lam_ref[0] * (a1 / l1)
    o_ref[...] = _sub_norm(o, lam_ref, gsub_ref[...]).astype(BF16)


def _attn_prompt(lam, qn, kb, vb, g_sub, nb, seq, tq=512, tk=512):
    tq = min(tq, seq)
    tk = min(tk, tq)
    nq = seq // tq
    return pl.pallas_call(
        functools.partial(_attn_prompt_kernel, tq=tq, tk=tk),
        grid=(nb, H_A, nq),
        in_specs=[
            pl.BlockSpec(memory_space=pltpu.SMEM),
            pl.BlockSpec((tq, DH_V), lambda b, h, i: (b * nq + i, h)),
            pl.BlockSpec((seq, DH_V), lambda b, h, i: (b, h)),
            pl.BlockSpec((seq, DH_V), lambda b, h, i: (b, h)),
            pl.BlockSpec((1, DH_V), lambda b, h, i: (0, 0)),
        ],
        out_specs=pl.BlockSpec((tq, DH_V), lambda b, h, i: (b * nq + i, h)),
        out_shape=jax.ShapeDtypeStruct((nb * seq, D_ATT), BF16),
        compiler_params=_cparams(("parallel", "parallel", "arbitrary")),
        name="attn_prompt",
    )(lam, qn, kb, vb, g_sub.reshape(1, DH_V))


def _attn_sample_kernel(lam_ref, q_ref, kn_ref, vn_ref, ckt_ref, cv_ref, gsub_ref, o_ref, m_ref, l_ref, acc_ref, *, tk):
    c = pl.program_id(1)
    nq = q_ref.shape[0]
    lane = lax.broadcasted_iota(jnp.int32, (1, DH_V), 1)
    zero = jnp.zeros((), BF16)

    @pl.when(c == 0)
    def _():
        m_ref[...] = jnp.full(m_ref.shape, -jnp.inf, F32)
        l_ref[...] = jnp.zeros(l_ref.shape, F32)
        acc_ref[...] = jnp.zeros(acc_ref.shape, F32)

    def q01(h):
        q = q_ref[:, h * DH_V:(h + 1) * DH_V]
        return jnp.concatenate([jnp.where(lane < DH_QK, q, zero), jnp.where(lane >= DH_QK, q, zero)], axis=0)

    def update(h, s, v):
        m_old = m_ref[h]
        m_new = jnp.maximum(m_old, jnp.max(s, axis=-1, keepdims=True))
        alpha = jnp.exp(m_old - m_new)
        p = jnp.exp(s - m_new[:, 0:1])
        l_ref[h] = alpha * l_ref[h] + jnp.sum(p, axis=-1, keepdims=True)
        acc_ref[h] = alpha * acc_ref[h] + jnp.dot(p.astype(BF16), v, preferred_element_type=F32)
        m_ref[h] = m_new

    for h in range(H_A):
        kt = ckt_ref[h].reshape(2 * DH_QK, tk).astype(BF16)
        s = jnp.dot(q01(h), kt, preferred_element_type=F32)
        update(h, s, cv_ref[pl.ds(h, tk, stride=H_A), :].astype(BF16))

    @pl.when(c == pl.num_programs(1) - 1)
    def _():
        for h in range(H_A):
            sl = slice(h * DH_V, (h + 1) * DH_V)
            s = lax.dot_general(q01(h), kn_ref[:, sl], _NT, preferred_element_type=F32)
            update(h, s, vn_ref[:, sl])
            a = acc_ref[h] / l_ref[h]
            o = a[0:nq] - lam_ref[0] * a[nq:2 * nq]
            o_ref[:, sl] = _sub_norm(o, lam_ref, gsub_ref[...]).astype(BF16)


def _attn_sample(lam, qn, kb, vb, ckt, cv2, layer, g_sub, row0, nb, nq, tk=1024):
    past = ckt.shape[-1]
    tk = min(tk, past)
    nc = past // tk
    r0 = row0 // nq
    new = lambda: pl.BlockSpec((nq, D_ATT), lambda b, c: (r0 + b, 0))
    return pl.pallas_call(
        functools.partial(_attn_sample_kernel, tk=tk),
        grid=(nb, nc),
        in_specs=[pl.BlockSpec(memory_space=pltpu.SMEM), new(), new(), new(),
                  pl.BlockSpec((None, None, H_A, 2, DH_QK, tk), lambda b, c: (layer, b, 0, 0, 0, c)),
                  pl.BlockSpec((tk * H_A, DH_V), lambda b, c: ((layer * nb + b) * nc + c, 0)),
                  pl.BlockSpec((1, DH_V), lambda b, c: (0, 0))],
        out_specs=pl.BlockSpec((nq, D_ATT), lambda b, c: (b, 0)),
        out_shape=jax.ShapeDtypeStruct((nb * nq, D_ATT), BF16),
        scratch_shapes=[pltpu.VMEM((H_A, 2 * nq, DH_V), F32), pltpu.VMEM((H_A, 2 * nq, DH_V), F32),
                        pltpu.VMEM((H_A, 2 * nq, DH_V), F32)],
        compiler_params=_cparams(("parallel", "arbitrary")),
        name="attn_sample",
    )(lam, qn, kb, vb, ckt, cv2, g_sub.reshape(1, DH_V))


def _split3(x, axis):
    hi = x.astype(BF16)
    r = x - hi.astype(F32)
    mid = r.astype(BF16)
    lo = (r - mid.astype(F32)).astype(BF16)
    return jnp.concatenate([hi, mid, lo], axis=axis)


def _ssd_kernel(xs_ref, b_ref, c_ref, z_ref, dt_ref, prev_ref, h0_ref, cw_ref, cb_ref, dtb_ref,
                aneg_ref, dexp_ref, gssd_ref, expand_ref, y_ref, hout_ref, ext_ref, ht_ref, *, valid):
    q = SSD_Q
    c = pl.program_id(1)
    halo = 8

    @pl.when(c == 0)
    def _():
        ext_ref[0:halo, :] = prev_ref[...]
        ht_ref[...] = h0_ref[...].T

    ext_ref[halo:halo + q, 0:D_SSM] = xs_ref[...]
    ext_ref[halo:halo + q, D_SSM:D_SSM + SSM_G * SSM_N] = b_ref[...]
    ext_ref[halo:halo + q, D_SSM + SSM_G * SSM_N:CONV_DIM] = c_ref[...]
    conv = cb_ref[...]
    for j in range(CONV_W):
        r0 = halo - (CONV_W - 1) + j
        conv = conv + ext_ref[r0:r0 + q, :] * cw_ref[j:j + 1, :]
    tail = ext_ref[q:q + halo, :]
    ext_ref[0:halo, :] = tail
    xbc = conv * jax.nn.sigmoid(conv)
    xs = xbc[:, 0:D_SSM]
    bm = xbc[:, D_SSM:D_SSM + SSM_G * SSM_N]
    cm = xbc[:, D_SSM + SSM_G * SSM_N:CONV_DIM]

    dt = jax.nn.softplus(dt_ref[...] + dtb_ref[...])
    if valid < q:
        rowv = lax.broadcasted_iota(jnp.int32, (q, LANE), 0) < valid
        dt = jnp.where(rowv, dt, 0.0)
    a = dt * aneg_ref[...]
    r_i = lax.broadcasted_iota(jnp.int32, (q, q), 0)
    c_i = lax.broadcasted_iota(jnp.int32, (q, q), 1)
    tril = c_i <= r_i
    tril3 = jnp.tile(tril.astype(BF16), (1, 3))
    acum = jnp.dot(tril3, _split3(a, 0), preferred_element_type=F32)
    eye3 = jnp.tile((lax.broadcasted_iota(jnp.int32, (LANE, LANE), 0)
                     == lax.broadcasted_iota(jnp.int32, (LANE, LANE), 1)).astype(BF16), (1, 3))
    acum_t = lax.dot_general(eye3, _split3(acum, 1), _NT, preferred_element_type=F32)
    a_last = acum[q - 1:q, :]
    decay_in = jnp.exp(a_last - acum)
    stack = jnp.concatenate(
        [dt, dt * decay_in, jnp.exp(acum), jnp.broadcast_to(jnp.exp(a_last), (8, LANE))], axis=0)
    ex = jnp.dot(_split3(stack, 1), expand_ref[...], preferred_element_type=F32)
    dt_e = ex[0:q]
    dtd_e = ex[q:2 * q]
    eacum_e = ex[2 * q:3 * q]
    cd_e = ex[3 * q:3 * q + 1]

    xdt = (xs * dt_e).astype(BF16)
    xdtd = (xs * dtd_e).astype(BF16)
    lane = lax.broadcasted_iota(jnp.int32, (1, LANE), 1)
    zero = jnp.zeros((), BF16)
    gw = D_SSM // SSM_G
    hpg = H_S // SSM_G
    y_diag_parts = []
    y_off_parts = []
    for g in range(SSM_G):
        gs = slice(g * gw, (g + 1) * gw)
        bg = bm[:, g * SSM_N:(g + 1) * SSM_N]
        cg = cm[:, g * SSM_N:(g + 1) * SSM_N].astype(BF16)
        cb = lax.dot_general(cg, bg.astype(BF16), _NT, preferred_element_type=F32)
        ht_prev = ht_ref[:, gs]
        y_off = jnp.dot(cg, ht_prev.astype(BF16), preferred_element_type=F32)
        ht_ref[:, gs] = cd_e[:, gs] * ht_prev + jnp.dot(
            bg.T.astype(BF16), xdtd[:, gs], preferred_element_type=F32)
        for k in range(hpg // 2):
            ms = []
            for h in (g * hpg + 2 * k, g * hpg + 2 * k + 1):
                seg = acum[:, h:h + 1] - acum_t[h:h + 1, :]
                ms.append((cb * jnp.exp(jnp.where(tril, seg, -jnp.inf))).astype(BF16))
            pair = slice((g * hpg + 2 * k) * SSM_P, (g * hpg + 2 * k + 2) * SSM_P)
            xp = xdt[:, pair]
            rhs = jnp.concatenate([jnp.where(lane < SSM_P, xp, zero), jnp.where(lane >= SSM_P, xp, zero)], axis=0)
            y_diag_parts.append(jnp.dot(jnp.concatenate(ms, axis=1), rhs, preferred_element_type=F32))
        y_off_parts.append(y_off)
    y_diag = jnp.concatenate(y_diag_parts, axis=1)
    y_off = jnp.concatenate(y_off_parts, axis=1)
    y = y_diag + y_off * eacum_e + dexp_ref[...] * xs
    zz = z_ref[...]
    y = y * (zz * jax.nn.sigmoid(zz))
    for g in range(SSM_G):
        gs = slice(g * gw, (g + 1) * gw)
        y_ref[:, gs] = _rms(y[:, gs], gssd_ref[:, gs]).astype(BF16)

    @pl.when(c == pl.num_programs(1) - 1)
    def _():
        hout_ref[...] = ht_ref[...].T


def _ssd(proj, dtp, prev, h0, lw, nb, seq, valid):
    nc = seq // SSD_Q
    row = lambda w, col: pl.BlockSpec((SSD_Q, w), lambda b, c, col=col: (b * nc + c, col))
    const = lambda shp: pl.BlockSpec(shp, lambda b, c: (0,) * len(shp))
    per_b = lambda shp: pl.BlockSpec((None,) + shp, lambda b, c: (b, 0, 0))
    return pl.pallas_call(
        functools.partial(_ssd_kernel, valid=valid),
        grid=(nb, nc),
        in_specs=[
            row(D_SSM, 4), row(SSM_G * SSM_N, 20), row(SSM_G * SSM_N, 21), row(D_SSM, 3), row(LANE, 0),
            per_b((8, CONV_DIM)), per_b((D_SSM, SSM_N)),
            const((CONV_W, CONV_DIM)), const((1, CONV_DIM)), const((1, LANE)), const((1, LANE)),
            const((1, D_SSM)), const((1, D_SSM)), const((3 * LANE, D_SSM)),
        ],
        out_specs=[row(D_SSM, 0), per_b((D_SSM, SSM_N))],
        out_shape=[jax.ShapeDtypeStruct((nb * seq, D_SSM), BF16),
                   jax.ShapeDtypeStruct((nb, D_SSM, SSM_N), F32)],
        scratch_shapes=[pltpu.VMEM((SSD_Q + 8, CONV_DIM), F32), pltpu.VMEM((SSM_N, D_SSM), F32)],
        compiler_params=_cparams(("parallel", "arbitrary")),
        name="ssd",
    )(proj, proj, proj, proj, dtp, prev, h0, lw["conv_w"], lw["conv_b"], lw["dt_bias"], lw["a_neg"],
      lw["d_exp"], lw["g_ssd"], lw["expand"])


def _out_proj_kernel(y_ref, o_ref, w_ref, h_ref, out_ref):
    acc = jnp.dot(y_ref[...], w_ref[0:D_SSM, :].astype(BF16), preferred_element_type=F32)
    acc = acc + jnp.dot(o_ref[...], w_ref[D_SSM:D_SSM + D_ATT, :].astype(BF16), preferred_element_type=F32)
    out_ref[...] = h_ref[...] + acc


def _out_proj(y, o, w, layer, h, tn=512):
    t, d = h.shape
    tm = _row_tile(t)
    return pl.pallas_call(
        _out_proj_kernel,
        grid=(t // tm, d // tn),
        in_specs=[
            pl.BlockSpec((tm, D_SSM), lambda i, j: (i, 0)),
            pl.BlockSpec((tm, D_ATT), lambda i, j: (i, 0)),
            pl.BlockSpec((None, D_SSM + D_ATT, tn), lambda i, j: (layer, 0, j)),
            pl.BlockSpec((tm, tn), lambda i, j: (i, j)),
        ],
        out_specs=pl.BlockSpec((tm, tn), lambda i, j: (i, j)),
        out_shape=jax.ShapeDtypeStruct((t, d), F32),
        compiler_params=_cparams(("parallel", "parallel")),
        name="out_proj",
    )(y, o, w, h)


def _top16(s, n_out=None):
    n, t = s.shape
    iota = lax.broadcasted_iota(jnp.int32, (n, t), 0)
    row16 = lax.broadcasted_iota(jnp.int32, (PEER_TOPK, t), 0)
    rank = jnp.full((n, t), PEER_TOPK, jnp.int32)
    vals = jnp.zeros((PEER_TOPK, t), F32)
    cnt = jnp.zeros((n_out, t), jnp.int32) if n_out else None
    for r in range(PEER_TOPK):
        m = jnp.max(s, axis=0, keepdims=True)
        idx = jnp.min(jnp.where(s == m, iota, n), axis=0, keepdims=True)
        sel = iota == idx
        rank = jnp.where(sel, r, rank)
        s = jnp.where(sel, -jnp.inf, s)
        vals = jnp.where(row16 == r, m, vals)
        if n_out:
            cnt = cnt + (row16 == jnp.right_shift(idx, TOPK_SHIFT)).astype(jnp.int32)
    return vals, rank, cnt


_MARK = 2.0 ** 126


def _peel16(s):
    t = s.shape[1]
    row16 = lax.broadcasted_iota(jnp.int32, (PEER_TOPK, t), 0)
    vals = jnp.zeros((PEER_TOPK, t), F32)
    for r in range(PEER_TOPK):
        m = jnp.max(s, axis=0, keepdims=True)
        s = jnp.where(s == m, -(1.0 + (r + 1) / 32.0) * _MARK, s)
        vals = jnp.where(row16 == r, m, vals)
    marked = s < -_MARK
    rank = jnp.where(marked, s * (-32.0 / _MARK) - 33.0, float(PEER_TOPK))
    count = jnp.sum(jnp.where(marked, 1.0, 0.0), axis=0, keepdims=True)
    return vals, rank, count


def _route_outputs(s1, s2, v1, v2, rank1, rank2, cnt, zsum):
    n1 = jnp.zeros(s1.shape, F32)
    for j in range(PEER_TOPK):
        n1 = jnp.where(rank1 == j, cnt[j:j + 1, :], n1)
    e1 = jnp.where(rank1 < PEER_TOPK, jnp.exp(s1 - v1[0:1, :]) / zsum, 0.0)
    e2 = jnp.where(rank2 < PEER_TOPK, jnp.exp(s2 - v2[0:1, :]), 0.0)
    return e1, n1, e2.astype(BF16), rank2.astype(BF16)


def _route_exact(s1, s2):
    v1, rank1, _ = _top16(s1)
    v2, rank2, _ = _top16(s2)
    cand = jnp.concatenate([v1[j:j + 1, :] + v2 for j in range(PEER_TOPK)], axis=0)
    sc, _, cnt = _top16(cand, n_out=PEER_TOPK)
    zsum = jnp.sum(jnp.exp(sc - sc[0:1, :]), axis=0, keepdims=True)
    return _route_outputs(s1, s2, v1, v2, rank1.astype(F32), rank2.astype(F32), cnt.astype(F32), zsum)


def _route_fast(s1, s2):
    half = PEER_TOPK // 2
    t = s1.shape[1]
    v1, rank1, c1 = _peel16(s1)
    v2, rank2, c2 = _peel16(s2)
    pieces = [v1[j:j + 1, :] + v2[0:half, :] for j in range(half)]
    pieces.append(v1[half:, :] + v2[0:1, :])
    pieces.append(v1[0:1, :] + v2[half:, :])
    cand = jnp.concatenate(pieces, axis=0)
    sc, rankc, cc = _peel16(cand)
    sel = rankc < PEER_TOPK
    zsum = jnp.sum(jnp.where(sel, jnp.exp(cand - sc[0:1, :]), 0.0), axis=0, keepdims=True)
    mk = jnp.where(sel, 1.0, 0.0)
    row8 = lax.broadcasted_iota(jnp.int32, (half, t), 0)
    n_lo = jnp.zeros((half, t), F32)
    for j in range(half):
        rs = jnp.sum(mk[j * half:(j + 1) * half, :], axis=0, keepdims=True)
        if j == 0:
            rs = rs + jnp.sum(mk[(half + 1) * half:, :], axis=0, keepdims=True)
        n_lo = jnp.where(row8 == j, rs, n_lo)
    cnt = jnp.concatenate([n_lo, mk[half * half:(half + 1) * half, :]], axis=0)
    bad = jnp.abs(c1 - PEER_TOPK) + jnp.abs(c2 - PEER_TOPK) + jnp.abs(cc - PEER_TOPK)
    return _route_outputs(s1, s2, v1, v2, rank1, rank2, cnt, zsum), bad


def _route_kernel(q_ref, k1_ref, k2_ref, e1_ref, n1_ref, e2_ref, r2_ref, *, tm):
    def store(ls, outs):
        e1_ref[:, ls], n1_ref[:, ls], e2_ref[:, ls], r2_ref[:, ls] = outs

    def scores(ls):
        qb = q_ref[ls, :].astype(BF16)
        s1 = lax.dot_general(k1_ref[...], qb[:, 0:N_KEYS], _NT, preferred_element_type=F32)
        s2 = lax.dot_general(k2_ref[...], qb[:, N_KEYS:2 * N_KEYS], _NT, preferred_element_type=F32)
        return s1, s2

    group = 4
    for g0 in range(0, tm // LANE, group):
        tiles = [slice(lt * LANE, (lt + 1) * LANE) for lt in range(g0, min(g0 + group, tm // LANE))]
        flags = []
        for ls in tiles:
            outs, bad = _route_fast(*scores(ls))
            store(ls, outs)
            flags.append(jnp.max(bad) > 0.0)
        for ls, flag in zip(tiles, flags):
            @pl.when(flag)
            def _(ls=ls):
                store(ls, _route_exact(*scores(ls)))


def _route(q, k1, k2, layer, tm=TM):
    t = q.shape[0]
    kd = k1.shape[-1]
    out = lambda: pl.BlockSpec((None, N_KEYS, tm), lambda i, h: (h, 0, i))
    key = lambda: pl.BlockSpec((None, None, N_KEYS, kd), lambda i, h: (layer, h, 0, 0))
    return pl.pallas_call(
        functools.partial(_route_kernel, tm=tm),
        grid=(t // tm, PEER_HEADS),
        in_specs=[pl.BlockSpec((tm, 2 * kd), lambda i, h: (i, h)), key(), key()],
        out_specs=[out(), out(), out(), out()],
        out_shape=[jax.ShapeDtypeStruct((PEER_HEADS, N_KEYS, t), F32),
                   jax.ShapeDtypeStruct((PEER_HEADS, N_KEYS, t), F32),
                   jax.ShapeDtypeStruct((PEER_HEADS, N_KEYS, t), BF16),
                   jax.ShapeDtypeStruct((PEER_HEADS, N_KEYS, t), BF16)],
        compiler_params=_cparams(("parallel", "parallel")),
        name="peer_route",
    )(q, k1, k2)


def _peer_kernel(x_ref, u_ref, vt_ref, e1_ref, n1_ref, e2_ref, r2_ref, h_ref, out_ref, acc_ref, *, te):
    j = pl.program_id(1)

    @pl.when(j == 0)
    def _():
        acc_ref[...] = jnp.zeros_like(acc_ref)

    at = lax.dot_general(u_ref[...], x_ref[...], _NT, preferred_element_type=F32)
    act = (0.5 * at * (1.0 + lax.erf(at * (2.0 ** -0.5)))).astype(BF16)
    zero = jnp.zeros((), BF16)
    parts = []
    for c in range(te // N_KEYS):
        w = None
        for h in range(PEER_HEADS):
            n1 = n1_ref[h, c:c + 1, :].astype(BF16)
            e1 = e1_ref[h, c:c + 1, :].astype(BF16)
            wh = jnp.where(r2_ref[h] < n1, e1 * e2_ref[h], zero)
            w = wh if w is None else w + wh
        parts.append(act[c * N_KEYS:(c + 1) * N_KEYS, :] * w)
    ht = jnp.concatenate(parts, axis=0)
    acc_ref[...] += jnp.dot(vt_ref[...], ht, preferred_element_type=F32)

    @pl.when(j == pl.num_programs(1) - 1)
    def _():
        out_ref[...] = h_ref[...] + acc_ref[...].T


def _peer(x, u, vt, layer, e1, n1, e2, r2, h, te=1024):
    t, d = h.shape
    ne = u.shape[1]
    rows = te // N_KEYS
    return pl.pallas_call(
        functools.partial(_peer_kernel, te=te),
        grid=(t // TM, ne // te),
        in_specs=[
            pl.BlockSpec((TM, d), lambda i, j: (i, 0)),
            pl.BlockSpec((None, te, d), lambda i, j: (layer, j, 0)),
            pl.BlockSpec((None, d, te), lambda i, j: (layer, 0, j)),
            pl.BlockSpec((PEER_HEADS, rows, TM), lambda i, j: (0, j, i)),
            pl.BlockSpec((PEER_HEADS, rows, TM), lambda i, j: (0, j, i)),
            pl.BlockSpec((PEER_HEADS, N_KEYS, TM), lambda i, j: (0, 0, i)),
            pl.BlockSpec((PEER_HEADS, N_KEYS, TM), lambda i, j: (0, 0, i)),
            pl.BlockSpec((TM, d), lambda i, j: (i, 0)),
        ],
        out_specs=pl.BlockSpec((TM, d), lambda i, j: (i, 0)),
        out_shape=jax.ShapeDtypeStruct((t, d), F32),
        scratch_shapes=[pltpu.VMEM((d, TM), F32)],
        compiler_params=_cparams(("parallel", "arbitrary")),
        name="peer_experts",
    )(x, u, vt, e1, n1, e2, r2, h)


def _ple_kernel(h_ref, g_ref, p_ref, wg_ref, wp_ref, hc_ref, out_ref, xn_ref):
    @pl.when(pl.program_id(1) == 0)
    def _():
        xn_ref[...] = _rms(h_ref[...], g_ref[...]).astype(BF16)

    gate = jax.nn.sigmoid(jnp.dot(xn_ref[...], wg_ref[...].astype(BF16), preferred_element_type=F32))
    emb = jnp.dot(p_ref[...].astype(BF16), wp_ref[...].astype(BF16), preferred_element_type=F32)
    out_ref[...] = hc_ref[...] + emb * gate


def _ple(h, g, p, wg, wp, layer, tn=512):
    t, d = h.shape
    pd = p.shape[2]
    tm = _row_tile(t)
    return pl.pallas_call(
        _ple_kernel,
        grid=(t // tm, d // tn),
        in_specs=[
            pl.BlockSpec((tm, d), lambda i, j: (i, 0)),
            pl.BlockSpec((1, d), lambda i, j: (0, 0)),
            pl.BlockSpec((None, tm, pd), lambda i, j: (layer, i, 0)),
            pl.BlockSpec((None, d, tn), lambda i, j: (layer, 0, j)),
            pl.BlockSpec((None, pd, tn), lambda i, j: (layer, 0, j)),
            pl.BlockSpec((tm, tn), lambda i, j: (i, j)),
        ],
        out_specs=pl.BlockSpec((tm, tn), lambda i, j: (i, j)),
        out_shape=jax.ShapeDtypeStruct((t, d), F32),
        scratch_shapes=[pltpu.VMEM((tm, d), BF16)],
        compiler_params=_cparams(("parallel", "arbitrary")),
        name="ple_gate",
    )(h, g.reshape(1, d), p, wg, wp, h)


def _lambda_init(layer_idx):
    return 0.8 - 0.6 * math.exp(-0.3 * layer_idx)


def _pad_rows(x, rows):
    return jnp.pad(x, ((0, rows - x.shape[0]),) + ((0, 0),) * (x.ndim - 1))


def kernel(x_prompt, x_sample, cache_k, cache_v, state_ssm, state_conv, p_prompt, p_sample, g_mix, w_in,
           conv_w, conv_b, dt_bias, a_log, d_skip, g_ssd, q_gain, k_gain, lam_q1, lam_k1, lam_q2, lam_k2,
           g_sub, w_out, g_ffn, peer_wq, peer_k1, peer_k2, peer_u, peer_v, g_ple, w_ple, w_pgate):
    nbp, seq, d = x_prompt.shape
    nbs, dseq, _ = x_sample.shape
    depth = w_in.shape[0]
    past = cache_k.shape[2]
    n_p = nbp * seq
    n_s = nbs * dseq
    t_pad = -(-(n_p + n_s) // TM) * TM

    h = _pad_rows(jnp.concatenate([x_prompt.reshape(n_p, d), x_sample.reshape(n_s, d)], axis=0), t_pad)
    ple_dim = p_prompt.shape[-1]
    p_all = jnp.concatenate([p_prompt.reshape(depth, n_p, ple_dim), p_sample.reshape(depth, n_s, ple_dim)], axis=1)
    p_all = jnp.pad(p_all, ((0, 0), (0, t_pad - n_p - n_s), (0, 0)))
    ck = jnp.transpose(cache_k, (0, 1, 3, 4, 5, 2))
    cv = cache_v.reshape(depth * nbs * past * H_A, DH_V)
    expand = jnp.tile((jnp.arange(LANE)[:, None] == jnp.arange(D_SSM)[None, :] // SSM_P).astype(BF16), (3, 1))
    prev_zero = jnp.zeros((nbp, 8, CONV_DIM), F32)
    h0_zero = jnp.zeros((nbp, D_SSM, SSM_N), F32)

    w_main = w_in
    w_dt = jnp.pad(w_in[:, :, MAIN_DIM:], ((0, 0), (0, 0), (0, LANE - H_S)))
    w_out_b = w_out
    wq_b = peer_wq
    k1_b = peer_k1.astype(BF16)
    k2_b = peer_k2.astype(BF16)
    u_b = peer_u.astype(BF16)
    vt_b = jnp.swapaxes(peer_v, 1, 2).astype(BF16)
    wg_b = w_pgate
    wp_b = w_ple

    vp_all = jnp.zeros((depth, n_p, D_ATT), F32)
    vs_all = jnp.zeros((depth, n_s, D_ATT), F32)

    outs = [[] for _ in range(6)]
    for l in range(depth):
        lam0 = _lambda_init(l)
        lam = (jnp.exp(jnp.sum(lam_q1[l] * lam_k1[l])) - jnp.exp(jnp.sum(lam_q2[l] * lam_k2[l])) + lam0)
        lam_s = jnp.stack([lam, jnp.asarray(1.0 - lam0, F32)]).astype(F32)
        lw = dict(
            conv_w=conv_w[l], conv_b=conv_b[l].reshape(1, CONV_DIM),
            dt_bias=_pad_rows(dt_bias[l], LANE).reshape(1, LANE),
            a_neg=_pad_rows(-jnp.exp(a_log[l]), LANE).reshape(1, LANE),
            d_exp=jnp.repeat(d_skip[l], SSM_P).reshape(1, D_SSM),
            g_ssd=g_ssd[l].reshape(1, D_SSM), expand=expand)

        proj, dtp = _norm_proj(h, g_mix[l], w_main, l, MAIN_DIM, wdt=w_dt)
        qn, kb, vb, kf, vp_all, vs_all = _qk_norm(proj, q_gain[l], k_gain[l], l, vp_all, vs_all)
        o_p = _attn_prompt(lam_s, qn, kb, vb, g_sub[l], nbp, seq)
        o_s = _attn_sample(lam_s, qn, kb, vb, ck, cv, l, g_sub[l], n_p, nbs, dseq)
        y_p, hT_p = _ssd(proj, dtp, prev_zero, h0_zero, lw, nbp, seq, SSD_Q)
        proj_s = proj[n_p:n_p + n_s].reshape(nbs, dseq, MAIN_DIM)
        pad_seq = lambda a: jnp.pad(a, ((0, 0), (0, SSD_Q - dseq), (0, 0))).reshape(nbs * SSD_Q, a.shape[-1])
        prev_s = jnp.pad(state_conv[l], ((0, 0), (8 - (CONV_W - 1), 0), (0, 0)))
        y_s, hT_s = _ssd(pad_seq(proj_s), pad_seq(dtp[n_p:n_p + n_s].reshape(nbs, dseq, LANE)), prev_s,
                         state_ssm[l].reshape(nbs, D_SSM, SSM_N), lw, nbs, SSD_Q, dseq)
        y_s = y_s.reshape(nbs, SSD_Q, D_SSM)[:, :dseq].reshape(n_s, D_SSM)
        y_all = _pad_rows(jnp.concatenate([y_p, y_s], axis=0), t_pad)
        o_all = _pad_rows(jnp.concatenate([o_p, o_s], axis=0), t_pad)
        h = _out_proj(y_all, o_all, w_out_b, l, h)

        q_peer, c = _norm_proj(h, g_ffn[l], wq_b, l, d, with_xn=True)
        e1, n1, e2, r2 = _route(q_peer, k1_b, k2_b, l)
        h = _peer(c, u_b, vt_b, l, e1, n1, e2, r2, h)

        h = _ple(h, g_ple[l], p_all, wg_b, wp_b, l)

        xbc0 = 2 * D_ATT + D_ATT + D_SSM
        tail = CONV_W - 1
        outs[0].append(hT_p.reshape(nbp, H_S, SSM_P, SSM_N))
        outs[1].append(jnp.stack([proj[(b + 1) * seq - tail:(b + 1) * seq, xbc0:] for b in range(nbp)]))
        outs[2].append(hT_s.reshape(nbs, H_S, SSM_P, SSM_N))
        outs[3].append(proj_s[:, dseq - tail:, xbc0:])
        outs[4].append(kf[:n_p].reshape(nbp, seq, H_A, 2, DH_QK))
        outs[5].append(kf[n_p:n_p + n_s].reshape(nbs, dseq, H_A, 2, DH_QK))

    y_prompt = h[:n_p].reshape(nbp, seq, d)
    y_sample = h[n_p:n_p + n_s].reshape(nbs, dseq, d)
    ssm_p, conv_p, ssm_s, conv_s, k_p, k_s = (jnp.stack(o) for o in outs)
    return (y_prompt, y_sample,
            k_p, vp_all.reshape(depth, nbp, seq, H_A, DH_V), ssm_p, conv_p,
            k_s, vs_all.reshape(depth, nbs, dseq, H_A, DH_V), ssm_s, conv_s)
```

```python
import functools
import math

import jax
import jax.numpy as jnp
from jax import lax
from jax.experimental import pallas as pl
from jax.experimental.pallas import tpu as pltpu

F32 = jnp.float32
BF16 = jnp.bfloat16

EPS = 1e-6
CHUNK = 64
CHUNK_SHIFT = 6
H_A = 8
DH_QK = 64
DH_V = 128
D_ATT = H_A * DH_V
ATT_SCALE = DH_QK ** -0.5
SSM_P = 64
H_S = 16
SSM_N = 128
SSM_G = 2
D_SSM = H_S * SSM_P
CONV_W = 4
CONV_DIM = D_SSM + 2 * SSM_G * SSM_N
MAIN_DIM = 2 * D_ATT + D_ATT + D_SSM + CONV_DIM
N_KEYS = 128
PEER_HEADS = 8
PEER_TOPK = 16
TOPK_SHIFT = 4

LANE = 128
TM = 512
MM_ROWS = 1152
SSD_Q = 128
VMEM_LIMIT = 56 * 1024 * 1024

_NT = (((1,), (1,)), ((), ()))


def _cparams(sem):
    return pltpu.CompilerParams(dimension_semantics=sem, vmem_limit_bytes=VMEM_LIMIT)


def _row_tile(t):
    for k in range(1, t // 16 + 1):
        if t % k == 0 and (t // k) % 16 == 0 and t // k <= MM_ROWS:
            return t // k
    return TM


def _rms(x, g):
    ms = jnp.mean(x * x, axis=-1, keepdims=True)
    return x * lax.rsqrt(ms + EPS) * g


def _norm_proj_kernel(x_ref, g_ref, w_ref, *rest, with_dt, with_xn):
    rest = list(rest)
    wdt_ref = rest.pop(0) if with_dt else None
    o_ref = rest.pop(0)
    odt_ref = rest.pop(0) if with_dt else None
    oxn_ref = rest.pop(0) if with_xn else None
    xn_ref = rest.pop(0)

    @pl.when(pl.program_id(1) == 0)
    def _():
        xn = _rms(x_ref[...], g_ref[...]).astype(BF16)
        xn_ref[...] = xn
        if with_xn:
            oxn_ref[...] = xn
        if with_dt:
            odt_ref[...] = jnp.dot(xn, wdt_ref[...], preferred_element_type=F32)

    o_ref[...] = jnp.dot(xn_ref[...], w_ref[...], preferred_element_type=F32)


def _norm_proj(x, g, w, layer, n, wdt=None, with_xn=False, tn=512):
    t, d = x.shape
    tm = _row_tile(t)
    with_dt = wdt is not None
    in_specs = [
        pl.BlockSpec((tm, d), lambda i, j: (i, 0)),
        pl.BlockSpec((1, d), lambda i, j: (0, 0)),
        pl.BlockSpec((None, d, tn), lambda i, j: (layer, 0, j)),
    ]
    args = [x, g.reshape(1, d), w]
    out_shape = [jax.ShapeDtypeStruct((t, n), F32)]
    out_specs = [pl.BlockSpec((tm, tn), lambda i, j: (i, j))]
    if with_dt:
        in_specs.append(pl.BlockSpec((None, d, LANE), lambda i, j: (layer, 0, 0)))
        args.append(wdt)
        out_shape.append(jax.ShapeDtypeStruct((t, LANE), F32))
        out_specs.append(pl.BlockSpec((tm, LANE), lambda i, j: (i, 0)))
    if with_xn:
        out_shape.append(jax.ShapeDtypeStruct((t, d), BF16))
        out_specs.append(pl.BlockSpec((tm, d), lambda i, j: (i, 0)))
    return pl.pallas_call(
        functools.partial(_norm_proj_kernel, with_dt=with_dt, with_xn=with_xn),
        grid=(t // tm, n // tn),
        in_specs=in_specs,
        out_specs=out_specs,
        out_shape=out_shape,
        scratch_shapes=[pltpu.VMEM((tm, d), BF16)],
        compiler_params=_cparams(("parallel", "arbitrary")),
        name="norm_proj",
    )(*args)


def _qk_norm_kernel(q_ref, k_ref, v_ref, qg_ref, kg_ref, vp_in, vs_in,
                    qn_ref, kb_ref, vb_ref, kf_ref, vp_ref, vs_ref, *, np_tiles, n_s):
    del vp_in, vs_in
    i = pl.program_id(0)
    lane = lax.broadcasted_iota(jnp.int32, (1, DH_V), 1)
    lo = lane < DH_QK

    def norm(x, g):
        xx = x * x
        s_lo = jnp.sum(jnp.where(lo, xx, 0.0), axis=-1, keepdims=True)
        s_all = jnp.sum(xx, axis=-1, keepdims=True)
        ms = jnp.where(lo, s_lo, s_all - s_lo) * (1.0 / DH_QK)
        return x * lax.rsqrt(ms + EPS) * g

    for h in range(H_A):
        sl = slice(h * DH_V, (h + 1) * DH_V)
        qn_ref[:, sl] = (norm(q_ref[:, sl], qg_ref[...]) * ATT_SCALE).astype(BF16)
        kn = norm(k_ref[:, sl], kg_ref[...])
        kf_ref[:, sl] = kn
        kb_ref[:, sl] = kn.astype(BF16)
    vb_ref[...] = v_ref[...].astype(BF16)

    @pl.when(i < np_tiles)
    def _():
        vp_ref[...] = v_ref[...]

    @pl.when(i == np_tiles)
    def _():
        vs_ref[...] = v_ref[0:n_s, :]


def _qk_norm(proj, q_gain, k_gain, layer, vp, vs):
    t = proj.shape[0]
    n_p, n_s = vp.shape[1], vs.shape[1]
    assert n_p % TM == 0 and n_s <= TM and t >= n_p + TM
    np_tiles = n_p // TM
    blk = lambda c: pl.BlockSpec((TM, D_ATT), lambda i, c=c: (i, c))
    gspec = pl.BlockSpec((1, DH_V), lambda i: (0, 0))
    hbm = pl.BlockSpec(memory_space=pl.ANY)
    p_out = pl.BlockSpec((None, TM, D_ATT), lambda i: (layer, jnp.minimum(i, np_tiles - 1), 0))
    s_out = pl.BlockSpec((None, n_s, D_ATT), lambda i: (layer, 0, 0))
    return pl.pallas_call(
        functools.partial(_qk_norm_kernel, np_tiles=np_tiles, n_s=n_s),
        grid=(t // TM,),
        in_specs=[blk(0), blk(1), blk(2), gspec, gspec, hbm, hbm],
        out_specs=[blk(0), blk(0), blk(0), blk(0), p_out, s_out],
        out_shape=[
            jax.ShapeDtypeStruct((t, D_ATT), BF16),
            jax.ShapeDtypeStruct((t, D_ATT), BF16),
            jax.ShapeDtypeStruct((t, D_ATT), BF16),
            jax.ShapeDtypeStruct((t, D_ATT), F32),
            jax.ShapeDtypeStruct(vp.shape, F32),
            jax.ShapeDtypeStruct(vs.shape, F32),
        ],
        input_output_aliases={5: 4, 6: 5},
        compiler_params=_cparams(("arbitrary",)),
        name="qk_norm",
    )(proj, proj, proj, q_gain.reshape(1, DH_V), k_gain.reshape(1, DH_V), vp, vs)


def _sub_norm(o, lam_ref, gsub):
    return _rms(o, gsub) * lam_ref[1]


def _attn_prompt_kernel(lam_ref, q_ref, k_ref, v_ref, gsub_ref, o_ref, *, tq, tk):
    qi = pl.program_id(2)
    q = q_ref[...]
    lane = lax.broadcasted_iota(jnp.int32, (1, DH_V), 1)
    zero = jnp.zeros((), BF16)
    q0 = jnp.where(lane < DH_QK, q, zero)
    q1 = jnp.where(lane >= DH_QK, q, zero)

    def block(start, mask):
        kb = k_ref[pl.ds(start, tk), :]
        s0 = lax.dot_general(q0, kb, _NT, preferred_element_type=F32)
        s1 = lax.dot_general(q1, kb, _NT, preferred_element_type=F32)
        if mask is not None:
            s0 = jnp.where(mask, s0, -jnp.inf)
            s1 = jnp.where(mask, s1, -jnp.inf)
        return s0, s1, v_ref[pl.ds(start, tk), :]

    def upd(s, vb, m, l, a):
        mn = jnp.maximum(m, jnp.max(s, axis=-1, keepdims=True))
        alpha = jnp.exp(m - mn)
        p = jnp.exp(s - mn)
        l = alpha * l + jnp.sum(p, axis=-1, keepdims=True)
        a = alpha * a + jnp.dot(p.astype(BF16), vb, preferred_element_type=F32)
        return mn, l, a

    row = jnp.right_shift(lax.broadcasted_iota(jnp.int32, (tq, tk), 0), CHUNK_SHIFT)
    col = jnp.right_shift(lax.broadcasted_iota(jnp.int32, (tq, tk), 1), CHUNK_SHIFT)
    base = pl.multiple_of(qi * tq, tq)
    s0, s1, vb = block(base, col <= row)
    m0 = jnp.max(s0, axis=-1, keepdims=True)
    m1 = jnp.max(s1, axis=-1, keepdims=True)
    p0 = jnp.exp(s0 - m0)
    p1 = jnp.exp(s1 - m1)
    l0 = jnp.sum(p0, axis=-1, keepdims=True)
    l1 = jnp.sum(p1, axis=-1, keepdims=True)
    a0 = jnp.dot(p0.astype(BF16), vb, preferred_element_type=F32)
    a1 = jnp.dot(p1.astype(BF16), vb, preferred_element_type=F32)
    for dblk in range(1, tq // tk):
        s0, s1, vb = block(base + dblk * tk, col + dblk * (tk // CHUNK) <= row)
        m0, l0, a0 = upd(s0, vb, m0, l0, a0)
        m1, l1, a1 = upd(s1, vb, m1, l1, a1)

    def body(j, carry):
        m0, l0, a0, m1, l1, a1 = carry
        s0, s1, vb = block(pl.multiple_of(j * tk, tk), None)
        m0, l0, a0 = upd(s0, vb, m0, l0, a0)
        m1, l1, a1 = upd(s1, vb, m1, l1, a1)
        return m0, l0, a0, m1, l1, a1

    m0, l0, a0, m1, l1, a1 = lax.fori_loop(0, qi * (tq // tk), body, (m0, l0, a0, m1, l1, a1))
    o = a0 / l0 - lam_ref[0] * (a1 / l1)
    o_ref[...] = _sub_norm(o, lam_ref, gsub_ref[...]).astype(BF16)


def _attn_prompt(lam, qn, kb, vb, g_sub, nb, seq, tq=512, tk=512):
    tq = min(tq, seq)
    tk = min(tk, tq)
    nq = seq // tq
    return pl.pallas_call(
        functools.partial(_attn_prompt_kernel, tq=tq, tk=tk),
        grid=(nb, H_A, nq),
        in_specs=[
            pl.BlockSpec(memory_space=pltpu.SMEM),
            pl.BlockSpec((tq, DH_V), lambda b, h, i: (b * nq + i, h)),
            pl.BlockSpec((seq, DH_V), lambda b, h, i: (b, h)),
            pl.BlockSpec((seq, DH_V), lambda b, h, i: (b, h)),
            pl.BlockSpec((1, DH_V), lambda b, h, i: (0, 0)),
        ],
        out_specs=pl.BlockSpec((tq, DH_V), lambda b, h, i: (b * nq + i, h)),
        out_shape=jax.ShapeDtypeStruct((nb * seq, D_ATT), BF16),
        compiler_params=_cparams(("parallel", "parallel", "arbitrary")),
        name="attn_prompt",
    )(lam, qn, kb, vb, g_sub.reshape(1, DH_V))


def _attn_sample_kernel(lam_ref, q_ref, kn_ref, vn_ref, ckt_ref, cv_ref, gsub_ref, o_ref, m_ref, l_ref, acc_ref, *, tk):
    c = pl.program_id(1)
    nq = q_ref.shape[0]
    lane = lax.broadcasted_iota(jnp.int32, (1, DH_V), 1)
    zero = jnp.zeros((), BF16)

    @pl.when(c == 0)
    def _():
        m_ref[...] = jnp.full(m_ref.shape, -jnp.inf, F32)
        l_ref[...] = jnp.zeros(l_ref.shape, F32)
        acc_ref[...] = jnp.zeros(acc_ref.shape, F32)

    def q01(h):
        q = q_ref[:, h * DH_V:(h + 1) * DH_V]
        return jnp.concatenate([jnp.where(lane < DH_QK, q, zero), jnp.where(lane >= DH_QK, q, zero)], axis=0)

    def update(h, s, v):
        m_old = m_ref[h]
        m_new = jnp.maximum(m_old, jnp.max(s, axis=-1, keepdims=True))
        alpha = jnp.exp(m_old - m_new)
        p = jnp.exp(s - m_new[:, 0:1])
        l_ref[h] = alpha * l_ref[h] + jnp.sum(p, axis=-1, keepdims=True)
        acc_ref[h] = alpha * acc_ref[h] + jnp.dot(p.astype(BF16), v, preferred_element_type=F32)
        m_ref[h] = m_new

    for h in range(H_A):
        kt = ckt_ref[h].reshape(2 * DH_QK, tk).astype(BF16)
        s = jnp.dot(q01(h), kt, preferred_element_type=F32)
        update(h, s, cv_ref[pl.ds(h, tk, stride=H_A), :].astype(BF16))

    @pl.when(c == pl.num_programs(1) - 1)
    def _():
        for h in range(H_A):
            sl = slice(h * DH_V, (h + 1) * DH_V)
            s = lax.dot_general(q01(h), kn_ref[:, sl], _NT, preferred_element_type=F32)
            update(h, s, vn_ref[:, sl])
            a = acc_ref[h] / l_ref[h]
            o = a[0:nq] - lam_ref[0] * a[nq:2 * nq]
            o_ref[:, sl] = _sub_norm(o, lam_ref, gsub_ref[...]).astype(BF16)


def _attn_sample(lam, qn, kb, vb, ckt, cv2, layer, g_sub, row0, nb, nq, tk=1024):
    past = ckt.shape[-1]
    tk = min(tk, past)
    nc = past // tk
    r0 = row0 // nq
    new = lambda: pl.BlockSpec((nq, D_ATT), lambda b, c: (r0 + b, 0))
    return pl.pallas_call(
        functools.partial(_attn_sample_kernel, tk=tk),
        grid=(nb, nc),
        in_specs=[pl.BlockSpec(memory_space=pltpu.SMEM), new(), new(), new(),
                  pl.BlockSpec((None, None, H_A, 2, DH_QK, tk), lambda b, c: (layer, b, 0, 0, 0, c)),
                  pl.BlockSpec((tk * H_A, DH_V), lambda b, c: ((layer * nb + b) * nc + c, 0)),
                  pl.BlockSpec((1, DH_V), lambda b, c: (0, 0))],
        out_specs=pl.BlockSpec((nq, D_ATT), lambda b, c: (b, 0)),
        out_shape=jax.ShapeDtypeStruct((nb * nq, D_ATT), BF16),
        scratch_shapes=[pltpu.VMEM((H_A, 2 * nq, DH_V), F32), pltpu.VMEM((H_A, 2 * nq, DH_V), F32),
                        pltpu.VMEM((H_A, 2 * nq, DH_V), F32)],
        compiler_params=_cparams(("parallel", "arbitrary")),
        name="attn_sample",
    )(lam, qn, kb, vb, ckt, cv2, g_sub.reshape(1, DH_V))


def _split3(x, axis):
    hi = x.astype(BF16)
    r = x - hi.astype(F32)
    mid = r.astype(BF16)
    lo = (r - mid.astype(F32)).astype(BF16)
    return jnp.concatenate([hi, mid, lo], axis=axis)


def _ssd_kernel(xs_ref, b_ref, c_ref, z_ref, dt_ref, prev_ref, h0_ref, cw_ref, cb_ref, dtb_ref,
                aneg_ref, dexp_ref, gssd_ref, expand_ref, y_ref, hout_ref, ext_ref, ht_ref, *, valid):
    q = SSD_Q
    c = pl.program_id(1)
    halo = 8

    @pl.when(c == 0)
    def _():
        ext_ref[0:halo, :] = prev_ref[...]
        ht_ref[...] = h0_ref[...].T

    ext_ref[halo:halo + q, 0:D_SSM] = xs_ref[...]
    ext_ref[halo:halo + q, D_SSM:D_SSM + SSM_G * SSM_N] = b_ref[...]
    ext_ref[halo:halo + q, D_SSM + SSM_G * SSM_N:CONV_DIM] = c_ref[...]
    conv = cb_ref[...]
    for j in range(CONV_W):
        r0 = halo - (CONV_W - 1) + j
        conv = conv + ext_ref[r0:r0 + q, :] * cw_ref[j:j + 1, :]
    tail = ext_ref[q:q + halo, :]
    ext_ref[0:halo, :] = tail
    xbc = conv * jax.nn.sigmoid(conv)
    xs = xbc[:, 0:D_SSM]
    bm = xbc[:, D_SSM:D_SSM + SSM_G * SSM_N]
    cm = xbc[:, D_SSM + SSM_G * SSM_N:CONV_DIM]

    dt = jax.nn.softplus(dt_ref[...] + dtb_ref[...])
    if valid < q:
        rowv = lax.broadcasted_iota(jnp.int32, (q, LANE), 0) < valid
        dt = jnp.where(rowv, dt, 0.0)
    a = dt * aneg_ref[...]
    r_i = lax.broadcasted_iota(jnp.int32, (q, q), 0)
    c_i = lax.broadcasted_iota(jnp.int32, (q, q), 1)
    tril = c_i <= r_i
    tril3 = jnp.tile(tril.astype(BF16), (1, 3))
    acum = jnp.dot(tril3, _split3(a, 0), preferred_element_type=F32)
    eye3 = jnp.tile((lax.broadcasted_iota(jnp.int32, (LANE, LANE), 0)
                     == lax.broadcasted_iota(jnp.int32, (LANE, LANE), 1)).astype(BF16), (1, 3))
    acum_t = lax.dot_general(eye3, _split3(acum, 1), _NT, preferred_element_type=F32)
    a_last = acum[q - 1:q, :]
    decay_in = jnp.exp(a_last - acum)
    stack = jnp.concatenate(
        [dt, dt * decay_in, jnp.exp(acum), jnp.broadcast_to(jnp.exp(a_last), (8, LANE))], axis=0)
    ex = jnp.dot(_split3(stack, 1), expand_ref[...], preferred_element_type=F32)
    dt_e = ex[0:q]
    dtd_e = ex[q:2 * q]
    eacum_e = ex[2 * q:3 * q]
    cd_e = ex[3 * q:3 * q + 1]

    xdt = (xs * dt_e).astype(BF16)
    xdtd = (xs * dtd_e).astype(BF16)
    lane = lax.broadcasted_iota(jnp.int32, (1, LANE), 1)
    zero = jnp.zeros((), BF16)
    gw = D_SSM // SSM_G
    hpg = H_S // SSM_G
    y_diag_parts = []
    y_off_parts = []
    for g in range(SSM_G):
        gs = slice(g * gw, (g + 1) * gw)
        bg = bm[:, g * SSM_N:(g + 1) * SSM_N]
        cg = cm[:, g * SSM_N:(g + 1) * SSM_N].astype(BF16)
        cb = lax.dot_general(cg, bg.astype(BF16), _NT, preferred_element_type=F32)
        ht_prev = ht_ref[:, gs]
        y_off = jnp.dot(cg, ht_prev.astype(BF16), preferred_element_type=F32)
        ht_ref[:, gs] = cd_e[:, gs] * ht_prev + jnp.dot(
            bg.T.astype(BF16), xdtd[:, gs], preferred_element_type=F32)
        for k in range(hpg // 2):
            ms = []
            for h in (g * hpg + 2 * k, g * hpg + 2 * k + 1):
                seg = acum[:, h:h + 1] - acum_t[h:h + 1, :]
                ms.append((cb * jnp.exp(jnp.where(tril, seg, -jnp.inf))).astype(BF16))
            pair = slice((g * hpg + 2 * k) * SSM_P, (g * hpg + 2 * k + 2) * SSM_P)
            xp = xdt[:, pair]
            rhs = jnp.concatenate([jnp.where(lane < SSM_P, xp, zero), jnp.where(lane >= SSM_P, xp, zero)], axis=0)
            y_diag_parts.append(jnp.dot(jnp.concatenate(ms, axis=1), rhs, preferred_element_type=F32))
        y_off_parts.append(y_off)
    y_diag = jnp.concatenate(y_diag_parts, axis=1)
    y_off = jnp.concatenate(y_off_parts, axis=1)
    y = y_diag + y_off * eacum_e + dexp_ref[...] * xs
    zz = z_ref[...]
    y = y * (zz * jax.nn.sigmoid(zz))
    for g in range(SSM_G):
        gs = slice(g * gw, (g + 1) * gw)
        y_ref[:, gs] = _rms(y[:, gs], gssd_ref[:, gs]).astype(BF16)

    @pl.when(c == pl.num_programs(1) - 1)
    def _():
        hout_ref[...] = ht_ref[...].T


def _ssd(proj, dtp, prev, h0, lw, nb, seq, valid):
    nc = seq // SSD_Q
    row = lambda w, col: pl.BlockSpec((SSD_Q, w), lambda b, c, col=col: (b * nc + c, col))
    const = lambda shp: pl.BlockSpec(shp, lambda b, c: (0,) * len(shp))
    per_b = lambda shp: pl.BlockSpec((None,) + shp, lambda b, c: (b, 0, 0))
    return pl.pallas_call(
        functools.partial(_ssd_kernel, valid=valid),
        grid=(nb, nc),
        in_specs=[
            row(D_SSM, 4), row(SSM_G * SSM_N, 20), row(SSM_G * SSM_N, 21), row(D_SSM, 3), row(LANE, 0),
            per_b((8, CONV_DIM)), per_b((D_SSM, SSM_N)),
            const((CONV_W, CONV_DIM)), const((1, CONV_DIM)), const((1, LANE)), const((1, LANE)),
            const((1, D_SSM)), const((1, D_SSM)), const((3 * LANE, D_SSM)),
        ],
        out_specs=[row(D_SSM, 0), per_b((D_SSM, SSM_N))],
        out_shape=[jax.ShapeDtypeStruct((nb * seq, D_SSM), BF16),
                   jax.ShapeDtypeStruct((nb, D_SSM, SSM_N), F32)],
        scratch_shapes=[pltpu.VMEM((SSD_Q + 8, CONV_DIM), F32), pltpu.VMEM((SSM_N, D_SSM), F32)],
        compiler_params=_cparams(("parallel", "arbitrary")),
        name="ssd",
    )(proj, proj, proj, proj, dtp, prev, h0, lw["conv_w"], lw["conv_b"], lw["dt_bias"], lw["a_neg"],
      lw["d_exp"], lw["g_ssd"], lw["expand"])


def _out_proj_kernel(y_ref, o_ref, w_ref, h_ref, out_ref):
    acc = jnp.dot(y_ref[...], w_ref[0:D_SSM, :], preferred_element_type=F32)
    acc = acc + jnp.dot(o_ref[...], w_ref[D_SSM:D_SSM + D_ATT, :], preferred_element_type=F32)
    out_ref[...] = h_ref[...] + acc


def _out_proj(y, o, w, layer, h, tn=512):
    t, d = h.shape
    tm = _row_tile(t)
    return pl.pallas_call(
        _out_proj_kernel,
        grid=(t // tm, d // tn),
        in_specs=[
            pl.BlockSpec((tm, D_SSM), lambda i, j: (i, 0)),
            pl.BlockSpec((tm, D_ATT), lambda i, j: (i, 0)),
            pl.BlockSpec((None, D_SSM + D_ATT, tn), lambda i, j: (layer, 0, j)),
            pl.BlockSpec((tm, tn), lambda i, j: (i, j)),
        ],
        out_specs=pl.BlockSpec((tm, tn), lambda i, j: (i, j)),
        out_shape=jax.ShapeDtypeStruct((t, d), F32),
        compiler_params=_cparams(("parallel", "parallel")),
        name="out_proj",
    )(y, o, w, h)


def _top16(s, n_out=None):
    n, t = s.shape
    iota = lax.broadcasted_iota(jnp.int32, (n, t), 0)
    row16 = lax.broadcasted_iota(jnp.int32, (PEER_TOPK, t), 0)
    rank = jnp.full((n, t), PEER_TOPK, jnp.int32)
    vals = jnp.zeros((PEER_TOPK, t), F32)
    cnt = jnp.zeros((n_out, t), jnp.int32) if n_out else None
    for r in range(PEER_TOPK):
        m = jnp.max(s, axis=0, keepdims=True)
        idx = jnp.min(jnp.where(s == m, iota, n), axis=0, keepdims=True)
        sel = iota == idx
        rank = jnp.where(sel, r, rank)
        s = jnp.where(sel, -jnp.inf, s)
        vals = jnp.where(row16 == r, m, vals)
        if n_out:
            cnt = cnt + (row16 == jnp.right_shift(idx, TOPK_SHIFT)).astype(jnp.int32)
    return vals, rank, cnt


_MARK = 2.0 ** 126


def _peel16(s):
    t = s.shape[1]
    row16 = lax.broadcasted_iota(jnp.int32, (PEER_TOPK, t), 0)
    vals = jnp.zeros((PEER_TOPK, t), F32)
    for r in range(PEER_TOPK):
        m = jnp.max(s, axis=0, keepdims=True)
        s = jnp.where(s == m, -(1.0 + (r + 1) / 32.0) * _MARK, s)
        vals = jnp.where(row16 == r, m, vals)
    marked = s < -_MARK
    rank = jnp.where(marked, s * (-32.0 / _MARK) - 33.0, float(PEER_TOPK))
    count = jnp.sum(jnp.where(marked, 1.0, 0.0), axis=0, keepdims=True)
    return vals, rank, count


def _route_outputs(s1, s2, v1, v2, rank1, rank2, cnt, zsum):
    n1 = jnp.zeros(s1.shape, F32)
    for j in range(PEER_TOPK):
        n1 = jnp.where(rank1 == j, cnt[j:j + 1, :], n1)
    e1 = jnp.where(rank1 < PEER_TOPK, jnp.exp(s1 - v1[0:1, :]) / zsum, 0.0)
    e2 = jnp.where(rank2 < PEER_TOPK, jnp.exp(s2 - v2[0:1, :]), 0.0)
    return e1, n1, e2.astype(BF16), rank2.astype(BF16)


def _route_exact(s1, s2):
    v1, rank1, _ = _top16(s1)
    v2, rank2, _ = _top16(s2)
    cand = jnp.concatenate([v1[j:j + 1, :] + v2 for j in range(PEER_TOPK)], axis=0)
    sc, _, cnt = _top16(cand, n_out=PEER_TOPK)
    zsum = jnp.sum(jnp.exp(sc - sc[0:1, :]), axis=0, keepdims=True)
    return _route_outputs(s1, s2, v1, v2, rank1.astype(F32), rank2.astype(F32), cnt.astype(F32), zsum)


def _route_fast(s1, s2):
    half = PEER_TOPK // 2
    t = s1.shape[1]
    v1, rank1, c1 = _peel16(s1)
    v2, rank2, c2 = _peel16(s2)
    pieces = [v1[j:j + 1, :] + v2[0:half, :] for j in range(half)]
    pieces.append(v1[half:, :] + v2[0:1, :])
    pieces.append(v1[0:1, :] + v2[half:, :])
    cand = jnp.concatenate(pieces, axis=0)
    sc, rankc, cc = _peel16(cand)
    sel = rankc < PEER_TOPK
    zsum = jnp.sum(jnp.where(sel, jnp.exp(cand - sc[0:1, :]), 0.0), axis=0, keepdims=True)
    mk = jnp.where(sel, 1.0, 0.0)
    row8 = lax.broadcasted_iota(jnp.int32, (half, t), 0)
    n_lo = jnp.zeros((half, t), F32)
    for j in range(half):
        rs = jnp.sum(mk[j * half:(j + 1) * half, :], axis=0, keepdims=True)
        if j == 0:
            rs = rs + jnp.sum(mk[(half + 1) * half:, :], axis=0, keepdims=True)
        n_lo = jnp.where(row8 == j, rs, n_lo)
    cnt = jnp.concatenate([n_lo, mk[half * half:(half + 1) * half, :]], axis=0)
    bad = jnp.abs(c1 - PEER_TOPK) + jnp.abs(c2 - PEER_TOPK) + jnp.abs(cc - PEER_TOPK)
    return _route_outputs(s1, s2, v1, v2, rank1, rank2, cnt, zsum), bad


def _route_kernel(q_ref, k1_ref, k2_ref, e1_ref, n1_ref, e2_ref, r2_ref, *, tm):
    def store(ls, outs):
        e1_ref[:, ls], n1_ref[:, ls], e2_ref[:, ls], r2_ref[:, ls] = outs

    def scores(ls):
        qb = q_ref[ls, :].astype(BF16)
        s1 = lax.dot_general(k1_ref[...], qb[:, 0:N_KEYS], _NT, preferred_element_type=F32)
        s2 = lax.dot_general(k2_ref[...], qb[:, N_KEYS:2 * N_KEYS], _NT, preferred_element_type=F32)
        return s1, s2

    group = 4
    for g0 in range(0, tm // LANE, group):
        tiles = [slice(lt * LANE, (lt + 1) * LANE) for lt in range(g0, min(g0 + group, tm // LANE))]
        flags = []
        for ls in tiles:
            outs, bad = _route_fast(*scores(ls))
            store(ls, outs)
            flags.append(jnp.max(bad) > 0.0)
        for ls, flag in zip(tiles, flags):
            @pl.when(flag)
            def _(ls=ls):
                store(ls, _route_exact(*scores(ls)))


def _route(q, k1, k2, layer, tm=TM):
    t = q.shape[0]
    kd = k1.shape[-1]
    out = lambda: pl.BlockSpec((None, N_KEYS, tm), lambda i, h: (h, 0, i))
    key = lambda: pl.BlockSpec((None, None, N_KEYS, kd), lambda i, h: (layer, h, 0, 0))
    return pl.pallas_call(
        functools.partial(_route_kernel, tm=tm),
        grid=(t // tm, PEER_HEADS),
        in_specs=[pl.BlockSpec((tm, 2 * kd), lambda i, h: (i, h)), key(), key()],
        out_specs=[out(), out(), out(), out()],
        out_shape=[jax.ShapeDtypeStruct((PEER_HEADS, N_KEYS, t), F32),
                   jax.ShapeDtypeStruct((PEER_HEADS, N_KEYS, t), F32),
                   jax.ShapeDtypeStruct((PEER_HEADS, N_KEYS, t), BF16),
                   jax.ShapeDtypeStruct((PEER_HEADS, N_KEYS, t), BF16)],
        compiler_params=_cparams(("parallel", "parallel")),
        name="peer_route",
    )(q, k1, k2)


def _peer_kernel(x_ref, u_ref, vt_ref, e1_ref, n1_ref, e2_ref, r2_ref, h_ref, out_ref, acc_ref, *, te):
    j = pl.program_id(1)

    @pl.when(j == 0)
    def _():
        acc_ref[...] = jnp.zeros_like(acc_ref)

    at = lax.dot_general(u_ref[...], x_ref[...], _NT, preferred_element_type=F32)
    act = (0.5 * at * (1.0 + lax.erf(at * (2.0 ** -0.5)))).astype(BF16)
    zero = jnp.zeros((), BF16)
    parts = []
    for c in range(te // N_KEYS):
        w = None
        for h in range(PEER_HEADS):
            n1 = n1_ref[h, c:c + 1, :].astype(BF16)
            e1 = e1_ref[h, c:c + 1, :].astype(BF16)
            wh = jnp.where(r2_ref[h] < n1, e1 * e2_ref[h], zero)
            w = wh if w is None else w + wh
        parts.append(act[c * N_KEYS:(c + 1) * N_KEYS, :] * w)
    ht = jnp.concatenate(parts, axis=0)
    acc_ref[...] += jnp.dot(vt_ref[...], ht, preferred_element_type=F32)

    @pl.when(j == pl.num_programs(1) - 1)
    def _():
        out_ref[...] = h_ref[...] + acc_ref[...].T


def _peer(x, u, vt, layer, e1, n1, e2, r2, h, te=1024):
    t, d = h.shape
    ne = u.shape[1]
    rows = te // N_KEYS
    return pl.pallas_call(
        functools.partial(_peer_kernel, te=te),
        grid=(t // TM, ne // te),
        in_specs=[
            pl.BlockSpec((TM, d), lambda i, j: (i, 0)),
            pl.BlockSpec((None, te, d), lambda i, j: (layer, j, 0)),
            pl.BlockSpec((None, d, te), lambda i, j: (layer, 0, j)),
            pl.BlockSpec((PEER_HEADS, rows, TM), lambda i, j: (0, j, i)),
            pl.BlockSpec((PEER_HEADS, rows, TM), lambda i, j: (0, j, i)),
            pl.BlockSpec((PEER_HEADS, N_KEYS, TM), lambda i, j: (0, 0, i)),
            pl.BlockSpec((PEER_HEADS, N_KEYS, TM), lambda i, j: (0, 0, i)),
            pl.BlockSpec((TM, d), lambda i, j: (i, 0)),
        ],
        out_specs=pl.BlockSpec((TM, d), lambda i, j: (i, 0)),
        out_shape=jax.ShapeDtypeStruct((t, d), F32),
        scratch_shapes=[pltpu.VMEM((d, TM), F32)],
        compiler_params=_cparams(("parallel", "arbitrary")),
        name="peer_experts",
    )(x, u, vt, e1, n1, e2, r2, h)


def _ple_kernel(h_ref, g_ref, p_ref, wg_ref, wp_ref, hc_ref, out_ref, xn_ref):
    @pl.when(pl.program_id(1) == 0)
    def _():
        xn_ref[...] = _rms(h_ref[...], g_ref[...]).astype(BF16)

    gate = jax.nn.sigmoid(jnp.dot(xn_ref[...], wg_ref[...], preferred_element_type=F32))
    emb = jnp.dot(p_ref[...].astype(BF16), wp_ref[...], preferred_element_type=F32)
    out_ref[...] = hc_ref[...] + emb * gate


def _ple(h, g, p, wg, wp, layer, tn=512):
    t, d = h.shape
    pd = p.shape[2]
    tm = _row_tile(t)
    return pl.pallas_call(
        _ple_kernel,
        grid=(t // tm, d // tn),
        in_specs=[
            pl.BlockSpec((tm, d), lambda i, j: (i, 0)),
            pl.BlockSpec((1, d), lambda i, j: (0, 0)),
            pl.BlockSpec((None, tm, pd), lambda i, j: (layer, i, 0)),
            pl.BlockSpec((None, d, tn), lambda i, j: (layer, 0, j)),
            pl.BlockSpec((None, pd, tn), lambda i, j: (layer, 0, j)),
            pl.BlockSpec((tm, tn), lambda i, j: (i, j)),
        ],
        out_specs=pl.BlockSpec((tm, tn), lambda i, j: (i, j)),
        out_shape=jax.ShapeDtypeStruct((t, d), F32),
        scratch_shapes=[pltpu.VMEM((tm, d), BF16)],
        compiler_params=_cparams(("parallel", "arbitrary")),
        name="ple_gate",
    )(h, g.reshape(1, d), p, wg, wp, h)


def _lambda_init(layer_idx):
    return 0.8 - 0.6 * math.exp(-0.3 * layer_idx)


def _pad_rows(x, rows):
    return jnp.pad(x, ((0, rows - x.shape[0]),) + ((0, 0),) * (x.ndim - 1))


def kernel(x_prompt, x_sample, cache_k, cache_v, state_ssm, state_conv, p_prompt, p_sample, g_mix, w_in,
           conv_w, conv_b, dt_bias, a_log, d_skip, g_ssd, q_gain, k_gain, lam_q1, lam_k1, lam_q2, lam_k2,
           g_sub, w_out, g_ffn, peer_wq, peer_k1, peer_k2, peer_u, peer_v, g_ple, w_ple, w_pgate):
    nbp, seq, d = x_prompt.shape
    nbs, dseq, _ = x_sample.shape
    depth = w_in.shape[0]
    past = cache_k.shape[2]
    n_p = nbp * seq
    n_s = nbs * dseq
    t_pad = -(-(n_p + n_s) // TM) * TM

    xp = x_prompt.reshape(n_p, d)
    h = jnp.concatenate([xp, x_sample.reshape(n_s, d), xp[:t_pad - n_p - n_s]], axis=0)
    ple_dim = p_prompt.shape[-1]
    p_all = jnp.concatenate([p_prompt.reshape(depth, n_p, ple_dim), p_sample.reshape(depth, n_s, ple_dim)], axis=1)
    p_all = jnp.pad(p_all, ((0, 0), (0, t_pad - n_p - n_s), (0, 0)))
    ck = jnp.transpose(cache_k, (0, 1, 3, 4, 5, 2))
    cv = cache_v.reshape(depth * nbs * past * H_A, DH_V)
    expand = jnp.tile((jnp.arange(LANE)[:, None] == jnp.arange(D_SSM)[None, :] // SSM_P).astype(BF16), (3, 1))
    prev_zero = jnp.zeros((nbp, 8, CONV_DIM), F32)
    h0_zero = jnp.zeros((nbp, D_SSM, SSM_N), F32)

    w_main = w_in.astype(BF16)
    w_dt = jnp.pad(w_in[:, :, MAIN_DIM:], ((0, 0), (0, 0), (0, LANE - H_S))).astype(BF16)
    w_out_b = w_out.astype(BF16)
    wq_b = peer_wq.astype(BF16)
    k1_b = peer_k1.astype(BF16)
    k2_b = peer_k2.astype(BF16)
    u_b = peer_u.astype(BF16)
    vt_b = jnp.swapaxes(peer_v, 1, 2).astype(BF16)
    wg_b = w_pgate.astype(BF16)
    wp_b = w_ple.astype(BF16)

    vp_all = jnp.zeros((depth, n_p, D_ATT), F32)
    vs_all = jnp.zeros((depth, n_s, D_ATT), F32)

    outs = [[] for _ in range(6)]
    for l in range(depth):
        lam0 = _lambda_init(l)
        lam = (jnp.exp(jnp.sum(lam_q1[l] * lam_k1[l])) - jnp.exp(jnp.sum(lam_q2[l] * lam_k2[l])) + lam0)
        lam_s = jnp.stack([lam, jnp.asarray(1.0 - lam0, F32)]).astype(F32)
        lw = dict(
            conv_w=conv_w[l], conv_b=conv_b[l].reshape(1, CONV_DIM),
            dt_bias=_pad_rows(dt_bias[l], LANE).reshape(1, LANE),
            a_neg=_pad_rows(-jnp.exp(a_log[l]), LANE).reshape(1, LANE),
            d_exp=jnp.repeat(d_skip[l], SSM_P).reshape(1, D_SSM),
            g_ssd=g_ssd[l].reshape(1, D_SSM), expand=expand)

        proj, dtp = _norm_proj(h, g_mix[l], w_main, l, MAIN_DIM, wdt=w_dt)
        qn, kb, vb, kf, vp_all, vs_all = _qk_norm(proj, q_gain[l], k_gain[l], l, vp_all, vs_all)
        o_p = _attn_prompt(lam_s, qn, kb, vb, g_sub[l], nbp, seq)
        o_s = _attn_sample(lam_s, qn, kb, vb, ck, cv, l, g_sub[l], n_p, nbs, dseq)
        y_p, hT_p = _ssd(proj, dtp, prev_zero, h0_zero, lw, nbp, seq, SSD_Q)
        proj_s = proj[n_p:n_p + n_s].reshape(nbs, dseq, MAIN_DIM)
        pad_seq = lambda a: jnp.pad(a, ((0, 0), (0, SSD_Q - dseq), (0, 0))).reshape(nbs * SSD_Q, a.shape[-1])
        prev_s = jnp.pad(state_conv[l], ((0, 0), (8 - (CONV_W - 1), 0), (0, 0)))
        y_s, hT_s = _ssd(pad_seq(proj_s), pad_seq(dtp[n_p:n_p + n_s].reshape(nbs, dseq, LANE)), prev_s,
                         state_ssm[l].reshape(nbs, D_SSM, SSM_N), lw, nbs, SSD_Q, dseq)
        y_s = y_s.reshape(nbs, SSD_Q, D_SSM)[:, :dseq].reshape(n_s, D_SSM)
        y_all = _pad_rows(jnp.concatenate([y_p, y_s], axis=0), t_pad)
        o_all = _pad_rows(jnp.concatenate([o_p, o_s], axis=0), t_pad)
        h = _out_proj(y_all, o_all, w_out_b, l, h)

        q_peer, c = _norm_proj(h, g_ffn[l], wq_b, l, d, with_xn=True)
        e1, n1, e2, r2 = _route(q_peer, k1_b, k2_b, l)
        h = _peer(c, u_b, vt_b, l, e1, n1, e2, r2, h)

        h = _ple(h, g_ple[l], p_all, wg_b, wp_b, l)

        xbc0 = 2 * D_ATT + D_ATT + D_SSM
        tail = CONV_W - 1
        outs[0].append(hT_p.reshape(nbp, H_S, SSM_P, SSM_N))
        outs[1].append(jnp.stack([proj[(b + 1) * seq - tail:(b + 1) * seq, xbc0:] for b in range(nbp)]))
        outs[2].append(hT_s.reshape(nbs, H_S, SSM_P, SSM_N))
        outs[3].append(proj_s[:, dseq - tail:, xbc0:])
        outs[4].append(kf[:n_p].reshape(nbp, seq, H_A, 2, DH_QK))
        outs[5].append(kf[n_p:n_p + n_s].reshape(nbs, dseq, H_A, 2, DH_QK))

    y_prompt = h[:n_p].reshape(nbp, seq, d)
    y_sample = h[n_p:n_p + n_s].reshape(nbs, dseq, d)
    ssm_p, conv_p, ssm_s, conv_s, k_p, k_s = (jnp.stack(o) for o in outs)
    return (y_prompt, y_sample,
            k_p, vp_all.reshape(depth, nbp, seq, H_A, DH_V), ssm_p, conv_p,
            k_s, vs_all.reshape(depth, nbs, dseq, H_A, DH_V), ssm_s, conv_s)
```

```python
import functools
import math

import jax
import jax.numpy as jnp
from jax import lax
from jax.experimental import pallas as pl
from jax.experimental.pallas import tpu as pltpu

F32 = jnp.float32
BF16 = jnp.bfloat16

EPS = 1e-6
CHUNK = 64
CHUNK_SHIFT = 6
H_A = 8
DH_QK = 64
DH_V = 128
D_ATT = H_A * DH_V
ATT_SCALE = DH_QK ** -0.5
SSM_P = 64
H_S = 16
SSM_N = 128
SSM_G = 2
D_SSM = H_S * SSM_P
CONV_W = 4
CONV_DIM = D_SSM + 2 * SSM_G * SSM_N
MAIN_DIM = 2 * D_ATT + D_ATT + D_SSM + CONV_DIM
N_KEYS = 128
PEER_HEADS = 8
PEER_TOPK = 16
TOPK_SHIFT = 4

LANE = 128
TM = 512
PEER_TM = 1024
MM_ROWS = 1152
SSD_Q = 128
VMEM_LIMIT = 56 * 1024 * 1024

_NT = (((1,), (1,)), ((), ()))


def _cparams(sem):
    return pltpu.CompilerParams(dimension_semantics=sem, vmem_limit_bytes=VMEM_LIMIT)


def _row_tile(t):
    for k in range(1, t // 16 + 1):
        if t % k == 0 and (t // k) % 16 == 0 and t // k <= MM_ROWS:
            return t // k
    return TM


def _rms(x, g):
    ms = jnp.mean(x * x, axis=-1, keepdims=True)
    return x * lax.rsqrt(ms + EPS) * g


def _norm_proj_kernel(x_ref, g_ref, w_ref, *rest, with_dt, with_xn):
    rest = list(rest)
    wdt_ref = rest.pop(0) if with_dt else None
    o_ref = rest.pop(0)
    odt_ref = rest.pop(0) if with_dt else None
    oxn_ref = rest.pop(0) if with_xn else None
    xn_ref = rest.pop(0)

    @pl.when(pl.program_id(1) == 0)
    def _():
        xn = _rms(x_ref[...], g_ref[...]).astype(BF16)
        xn_ref[...] = xn
        if with_xn:
            oxn_ref[...] = xn
        if with_dt:
            odt_ref[...] = jnp.dot(xn, wdt_ref[...], preferred_element_type=F32)

    o_ref[...] = jnp.dot(xn_ref[...], w_ref[...], preferred_element_type=F32)


def _norm_proj(x, g, w, layer, n, wdt=None, with_xn=False, tn=512):
    t, d = x.shape
    tm = _row_tile(t)
    with_dt = wdt is not None
    in_specs = [
        pl.BlockSpec((tm, d), lambda i, j: (i, 0)),
        pl.BlockSpec((1, d), lambda i, j: (0, 0)),
        pl.BlockSpec((None, d, tn), lambda i, j: (layer, 0, j)),
    ]
    args = [x, g.reshape(1, d), w]
    out_shape = [jax.ShapeDtypeStruct((t, n), F32)]
    out_specs = [pl.BlockSpec((tm, tn), lambda i, j: (i, j))]
    if with_dt:
        in_specs.append(pl.BlockSpec((None, d, LANE), lambda i, j: (layer, 0, 0)))
        args.append(wdt)
        out_shape.append(jax.ShapeDtypeStruct((t, LANE), F32))
        out_specs.append(pl.BlockSpec((tm, LANE), lambda i, j: (i, 0)))
    if with_xn:
        out_shape.append(jax.ShapeDtypeStruct((t, d), BF16))
        out_specs.append(pl.BlockSpec((tm, d), lambda i, j: (i, 0)))
    return pl.pallas_call(
        functools.partial(_norm_proj_kernel, with_dt=with_dt, with_xn=with_xn),
        grid=(t // tm, n // tn),
        in_specs=in_specs,
        out_specs=out_specs,
        out_shape=out_shape,
        scratch_shapes=[pltpu.VMEM((tm, d), BF16)],
        compiler_params=_cparams(("parallel", "arbitrary")),
        name="norm_proj",
    )(*args)


def _qk_norm_kernel(q_ref, k_ref, v_ref, qg_ref, kg_ref, vp_in, vs_in,
                    qn_ref, kb_ref, vb_ref, kf_ref, vp_ref, vs_ref, *, np_tiles, n_s):
    del vp_in, vs_in
    i = pl.program_id(0)
    lane = lax.broadcasted_iota(jnp.int32, (1, DH_V), 1)
    lo = lane < DH_QK

    def norm(x, g):
        xx = x * x
        s_lo = jnp.sum(jnp.where(lo, xx, 0.0), axis=-1, keepdims=True)
        s_all = jnp.sum(xx, axis=-1, keepdims=True)
        ms = jnp.where(lo, s_lo, s_all - s_lo) * (1.0 / DH_QK)
        return x * lax.rsqrt(ms + EPS) * g

    for h in range(H_A):
        sl = slice(h * DH_V, (h + 1) * DH_V)
        qn_ref[:, sl] = (norm(q_ref[:, sl], qg_ref[...]) * ATT_SCALE).astype(BF16)
        kn = norm(k_ref[:, sl], kg_ref[...])
        kf_ref[:, sl] = kn
        kb_ref[:, sl] = kn.astype(BF16)
    vb_ref[...] = v_ref[...].astype(BF16)

    @pl.when(i < np_tiles)
    def _():
        vp_ref[...] = v_ref[...]

    @pl.when(i == np_tiles)
    def _():
        vs_ref[...] = v_ref[0:n_s, :]


def _qk_norm(proj, q_gain, k_gain, layer, vp, vs):
    t = proj.shape[0]
    n_p, n_s = vp.shape[1], vs.shape[1]
    assert n_p % TM == 0 and n_s <= TM and t >= n_p + TM
    np_tiles = n_p // TM
    blk = lambda c: pl.BlockSpec((TM, D_ATT), lambda i, c=c: (i, c))
    gspec = pl.BlockSpec((1, DH_V), lambda i: (0, 0))
    hbm = pl.BlockSpec(memory_space=pl.ANY)
    p_out = pl.BlockSpec((None, TM, D_ATT), lambda i: (layer, jnp.minimum(i, np_tiles - 1), 0))
    s_out = pl.BlockSpec((None, n_s, D_ATT), lambda i: (layer, 0, 0))
    return pl.pallas_call(
        functools.partial(_qk_norm_kernel, np_tiles=np_tiles, n_s=n_s),
        grid=(t // TM,),
        in_specs=[blk(0), blk(1), blk(2), gspec, gspec, hbm, hbm],
        out_specs=[blk(0), blk(0), blk(0), blk(0), p_out, s_out],
        out_shape=[
            jax.ShapeDtypeStruct((t, D_ATT), BF16),
            jax.ShapeDtypeStruct((t, D_ATT), BF16),
            jax.ShapeDtypeStruct((t, D_ATT), BF16),
            jax.ShapeDtypeStruct((t, D_ATT), F32),
            jax.ShapeDtypeStruct(vp.shape, F32),
            jax.ShapeDtypeStruct(vs.shape, F32),
        ],
        input_output_aliases={5: 4, 6: 5},
        compiler_params=_cparams(("arbitrary",)),
        name="qk_norm",
    )(proj, proj, proj, q_gain.reshape(1, DH_V), k_gain.reshape(1, DH_V), vp, vs)


def _sub_norm(o, lam_ref, gsub):
    return _rms(o, gsub) * lam_ref[1]


def _attn_prompt_kernel(lam_ref, q_ref, k_ref, v_ref, gsub_ref, o_ref, *, tq, tk):
    qi = pl.program_id(2)
    q = q_ref[...]
    lane = lax.broadcasted_iota(jnp.int32, (1, DH_V), 1)
    zero = jnp.zeros((), BF16)
    q0 = jnp.where(lane < DH_QK, q, zero)
    q1 = jnp.where(lane >= DH_QK, q, zero)

    def block(start, mask):
        kb = k_ref[pl.ds(start, tk), :]
        s0 = lax.dot_general(q0, kb, _NT, preferred_element_type=F32)
        s1 = lax.dot_general(q1, kb, _NT, preferred_element_type=F32)
        if mask is not None:
            s0 = jnp.where(mask, s0, -jnp.inf)
            s1 = jnp.where(mask, s1, -jnp.inf)
        return s0, s1, v_ref[pl.ds(start, tk), :]

    def upd(s, vb, m, l, a):
        mn = jnp.maximum(m, jnp.max(s, axis=-1, keepdims=True))
        alpha = jnp.exp(m - mn)
        p = jnp.exp(s - mn)
        l = alpha * l + jnp.sum(p, axis=-1, keepdims=True)
        a = alpha * a + jnp.dot(p.astype(BF16), vb, preferred_element_type=F32)
        return mn, l, a

    row = jnp.right_shift(lax.broadcasted_iota(jnp.int32, (tq, tk), 0), CHUNK_SHIFT)
    col = jnp.right_shift(lax.broadcasted_iota(jnp.int32, (tq, tk), 1), CHUNK_SHIFT)
    base = pl.multiple_of(qi * tq, tq)
    s0, s1, vb = block(base, col <= row)
    m0 = jnp.max(s0, axis=-1, keepdims=True)
    m1 = jnp.max(s1, axis=-1, keepdims=True)
    p0 = jnp.exp(s0 - m0)
    p1 = jnp.exp(s1 - m1)
    l0 = jnp.sum(p0, axis=-1, keepdims=True)
    l1 = jnp.sum(p1, axis=-1, keepdims=True)
    a0 = jnp.dot(p0.astype(BF16), vb, preferred_element_type=F32)
    a1 = jnp.dot(p1.astype(BF16), vb, preferred_element_type=F32)
    for dblk in range(1, tq // tk):
        s0, s1, vb = block(base + dblk * tk, col + dblk * (tk // CHUNK) <= row)
        m0, l0, a0 = upd(s0, vb, m0, l0, a0)
        m1, l1, a1 = upd(s1, vb, m1, l1, a1)

    def body(j, carry):
        m0, l0, a0, m1, l1, a1 = carry
        s0, s1, vb = block(pl.multiple_of(j * tk, tk), None)
        m0, l0, a0 = upd(s0, vb, m0, l0, a0)
        m1, l1, a1 = upd(s1, vb, m1, l1, a1)
        return m0, l0, a0, m1, l1, a1

    m0, l0, a0, m1, l1, a1 = lax.fori_loop(0, qi * (tq // tk), body, (m0, l0, a0, m1, l1, a1))
    o = a0 / l0 - lam_ref[0] * (a1 / l1)
    o_ref[...] = _sub_norm(o, lam_ref, gsub_ref[...]).astype(BF16)


def _attn_prompt(lam, qn, kb, vb, g_sub, nb, seq, tq=512, tk=512):
    tq = min(tq, seq)
    tk = min(tk, tq)
    nq = seq // tq
    return pl.pallas_call(
        functools.partial(_attn_prompt_kernel, tq=tq, tk=tk),
        grid=(nb, H_A, nq),
        in_specs=[
            pl.BlockSpec(memory_space=pltpu.SMEM),
            pl.BlockSpec((tq, DH_V), lambda b, h, i: (b * nq + i, h)),
            pl.BlockSpec((seq, DH_V), lambda b, h, i: (b, h)),
            pl.BlockSpec((seq, DH_V), lambda b, h, i: (b, h)),
            pl.BlockSpec((1, DH_V), lambda b, h, i: (0, 0)),
        ],
        out_specs=pl.BlockSpec((tq, DH_V), lambda b, h, i: (b * nq + i, h)),
        out_shape=jax.ShapeDtypeStruct((nb * seq, D_ATT), BF16),
        compiler_params=_cparams(("parallel", "parallel", "arbitrary")),
        name="attn_prompt",
    )(lam, qn, kb, vb, g_sub.reshape(1, DH_V))


def _attn_sample_kernel(lam_ref, q_ref, kn_ref, vn_ref, ckt_ref, cv_ref, gsub_ref, o_ref, m_ref, l_ref, acc_ref, *, tk):
    c = pl.program_id(1)
    nq = q_ref.shape[0]
    lane = lax.broadcasted_iota(jnp.int32, (1, DH_V), 1)
    zero = jnp.zeros((), BF16)

    @pl.when(c == 0)
    def _():
        m_ref[...] = jnp.full(m_ref.shape, -jnp.inf, F32)
        l_ref[...] = jnp.zeros(l_ref.shape, F32)
        acc_ref[...] = jnp.zeros(acc_ref.shape, F32)

    def q01(h):
        q = q_ref[:, h * DH_V:(h + 1) * DH_V]
        return jnp.concatenate([jnp.where(lane < DH_QK, q, zero), jnp.where(lane >= DH_QK, q, zero)], axis=0)

    def update(h, s, v):
        m_old = m_ref[h]
        m_new = jnp.maximum(m_old, jnp.max(s, axis=-1, keepdims=True))
        alpha = jnp.exp(m_old - m_new)
        p = jnp.exp(s - m_new[:, 0:1])
        l_ref[h] = alpha * l_ref[h] + jnp.sum(p, axis=-1, keepdims=True)
        acc_ref[h] = alpha * acc_ref[h] + jnp.dot(p.astype(BF16), v, preferred_element_type=F32)
        m_ref[h] = m_new

    for h in range(H_A):
        kt = ckt_ref[h].reshape(2 * DH_QK, tk).astype(BF16)
        s = jnp.dot(q01(h), kt, preferred_element_type=F32)
        update(h, s, cv_ref[pl.ds(h, tk, stride=H_A), :].astype(BF16))

    @pl.when(c == pl.num_programs(1) - 1)
    def _():
        for h in range(H_A):
            sl = slice(h * DH_V, (h + 1) * DH_V)
            s = lax.dot_general(q01(h), kn_ref[:, sl], _NT, preferred_element_type=F32)
            update(h, s, vn_ref[:, sl])
            a = acc_ref[h] / l_ref[h]
            o = a[0:nq] - lam_ref[0] * a[nq:2 * nq]
            o_ref[:, sl] = _sub_norm(o, lam_ref, gsub_ref[...]).astype(BF16)


def _attn_sample(lam, qn, kb, vb, ckt, cv2, layer, g_sub, row0, nb, nq, tk=1024):
    past = ckt.shape[-1]
    tk = min(tk, past)
    nc = past // tk
    r0 = row0 // nq
    new = lambda: pl.BlockSpec((nq, D_ATT), lambda b, c: (r0 + b, 0))
    return pl.pallas_call(
        functools.partial(_attn_sample_kernel, tk=tk),
        grid=(nb, nc),
        in_specs=[pl.BlockSpec(memory_space=pltpu.SMEM), new(), new(), new(),
                  pl.BlockSpec((None, None, H_A, 2, DH_QK, tk), lambda b, c: (layer, b, 0, 0, 0, c)),
                  pl.BlockSpec((tk * H_A, DH_V), lambda b, c: ((layer * nb + b) * nc + c, 0)),
                  pl.BlockSpec((1, DH_V), lambda b, c: (0, 0))],
        out_specs=pl.BlockSpec((nq, D_ATT), lambda b, c: (b, 0)),
        out_shape=jax.ShapeDtypeStruct((nb * nq, D_ATT), BF16),
        scratch_shapes=[pltpu.VMEM((H_A, 2 * nq, DH_V), F32), pltpu.VMEM((H_A, 2 * nq, DH_V), F32),
                        pltpu.VMEM((H_A, 2 * nq, DH_V), F32)],
        compiler_params=_cparams(("parallel", "arbitrary")),
        name="attn_sample",
    )(lam, qn, kb, vb, ckt, cv2, g_sub.reshape(1, DH_V))


def _split3(x, axis):
    hi = x.astype(BF16)
    r = x - hi.astype(F32)
    mid = r.astype(BF16)
    lo = (r - mid.astype(F32)).astype(BF16)
    return jnp.concatenate([hi, mid, lo], axis=axis)


def _ssd_kernel(xs_ref, b_ref, c_ref, z_ref, dt_ref, prev_ref, h0_ref, cw_ref, cb_ref, dtb_ref,
                aneg_ref, dexp_ref, gssd_ref, expand_ref, y_ref, hout_ref, ext_ref, ht_ref, *, valid):
    q = SSD_Q
    c = pl.program_id(1)
    halo = 8

    @pl.when(c == 0)
    def _():
        ext_ref[0:halo, :] = prev_ref[...]
        ht_ref[...] = h0_ref[...].T

    ext_ref[halo:halo + q, 0:D_SSM] = xs_ref[...]
    ext_ref[halo:halo + q, D_SSM:D_SSM + SSM_G * SSM_N] = b_ref[...]
    ext_ref[halo:halo + q, D_SSM + SSM_G * SSM_N:CONV_DIM] = c_ref[...]
    conv = cb_ref[...]
    for j in range(CONV_W):
        r0 = halo - (CONV_W - 1) + j
        conv = conv + ext_ref[r0:r0 + q, :] * cw_ref[j:j + 1, :]
    tail = ext_ref[q:q + halo, :]
    ext_ref[0:halo, :] = tail
    xbc = conv * jax.nn.sigmoid(conv)
    xs = xbc[:, 0:D_SSM]
    bm = xbc[:, D_SSM:D_SSM + SSM_G * SSM_N]
    cm = xbc[:, D_SSM + SSM_G * SSM_N:CONV_DIM]

    dt = jax.nn.softplus(dt_ref[...] + dtb_ref[...])
    if valid < q:
        rowv = lax.broadcasted_iota(jnp.int32, (q, LANE), 0) < valid
        dt = jnp.where(rowv, dt, 0.0)
    a = dt * aneg_ref[...]
    r_i = lax.broadcasted_iota(jnp.int32, (q, q), 0)
    c_i = lax.broadcasted_iota(jnp.int32, (q, q), 1)
    tril = c_i <= r_i
    tril3 = jnp.tile(tril.astype(BF16), (1, 3))
    acum = jnp.dot(tril3, _split3(a, 0), preferred_element_type=F32)
    eye3 = jnp.tile((lax.broadcasted_iota(jnp.int32, (LANE, LANE), 0)
                     == lax.broadcasted_iota(jnp.int32, (LANE, LANE), 1)).astype(BF16), (1, 3))
    acum_t = lax.dot_general(eye3, _split3(acum, 1), _NT, preferred_element_type=F32)
    a_last = acum[q - 1:q, :]
    decay_in = jnp.exp(a_last - acum)
    stack = jnp.concatenate(
        [dt, dt * decay_in, jnp.exp(acum), jnp.broadcast_to(jnp.exp(a_last), (8, LANE))], axis=0)
    ex = jnp.dot(_split3(stack, 1), expand_ref[...], preferred_element_type=F32)
    dt_e = ex[0:q]
    dtd_e = ex[q:2 * q]
    eacum_e = ex[2 * q:3 * q]
    cd_e = ex[3 * q:3 * q + 1]

    xdt = (xs * dt_e).astype(BF16)
    xdtd = (xs * dtd_e).astype(BF16)
    lane = lax.broadcasted_iota(jnp.int32, (1, LANE), 1)
    zero = jnp.zeros((), BF16)
    gw = D_SSM // SSM_G
    hpg = H_S // SSM_G
    y_diag_parts = []
    y_off_parts = []
    for g in range(SSM_G):
        gs = slice(g * gw, (g + 1) * gw)
        bg = bm[:, g * SSM_N:(g + 1) * SSM_N]
        cg = cm[:, g * SSM_N:(g + 1) * SSM_N].astype(BF16)
        cb = lax.dot_general(cg, bg.astype(BF16), _NT, preferred_element_type=F32)
        ht_prev = ht_ref[:, gs]
        y_off = jnp.dot(cg, ht_prev.astype(BF16), preferred_element_type=F32)
        ht_ref[:, gs] = cd_e[:, gs] * ht_prev + jnp.dot(
            bg.T.astype(BF16), xdtd[:, gs], preferred_element_type=F32)
        for k in range(hpg // 2):
            ms = []
            for h in (g * hpg + 2 * k, g * hpg + 2 * k + 1):
                seg = acum[:, h:h + 1] - acum_t[h:h + 1, :]
                ms.append((cb * jnp.exp(jnp.where(tril, seg, -jnp.inf))).astype(BF16))
            pair = slice((g * hpg + 2 * k) * SSM_P, (g * hpg + 2 * k + 2) * SSM_P)
            xp = xdt[:, pair]
            rhs = jnp.concatenate([jnp.where(lane < SSM_P, xp, zero), jnp.where(lane >= SSM_P, xp, zero)], axis=0)
            y_diag_parts.append(jnp.dot(jnp.concatenate(ms, axis=1), rhs, preferred_element_type=F32))
        y_off_parts.append(y_off)
    y_diag = jnp.concatenate(y_diag_parts, axis=1)
    y_off = jnp.concatenate(y_off_parts, axis=1)
    y = y_diag + y_off * eacum_e + dexp_ref[...] * xs
    zz = z_ref[...]
    y = y * (zz * jax.nn.sigmoid(zz))
    for g in range(SSM_G):
        gs = slice(g * gw, (g + 1) * gw)
        y_ref[:, gs] = _rms(y[:, gs], gssd_ref[:, gs]).astype(BF16)

    @pl.when(c == pl.num_programs(1) - 1)
    def _():
        hout_ref[...] = ht_ref[...].T


def _ssd(proj, dtp, prev, h0, lw, nb, seq, valid):
    nc = seq // SSD_Q
    row = lambda w, col: pl.BlockSpec((SSD_Q, w), lambda b, c, col=col: (b * nc + c, col))
    const = lambda shp: pl.BlockSpec(shp, lambda b, c: (0,) * len(shp))
    per_b = lambda shp: pl.BlockSpec((None,) + shp, lambda b, c: (b, 0, 0))
    return pl.pallas_call(
        functools.partial(_ssd_kernel, valid=valid),
        grid=(nb, nc),
        in_specs=[
            row(D_SSM, 4), row(SSM_G * SSM_N, 20), row(SSM_G * SSM_N, 21), row(D_SSM, 3), row(LANE, 0),
            per_b((8, CONV_DIM)), per_b((D_SSM, SSM_N)),
            const((CONV_W, CONV_DIM)), const((1, CONV_DIM)), const((1, LANE)), const((1, LANE)),
            const((1, D_SSM)), const((1, D_SSM)), const((3 * LANE, D_SSM)),
        ],
        out_specs=[row(D_SSM, 0), per_b((D_SSM, SSM_N))],
        out_shape=[jax.ShapeDtypeStruct((nb * seq, D_SSM), BF16),
                   jax.ShapeDtypeStruct((nb, D_SSM, SSM_N), F32)],
        scratch_shapes=[pltpu.VMEM((SSD_Q + 8, CONV_DIM), F32), pltpu.VMEM((SSM_N, D_SSM), F32)],
        compiler_params=_cparams(("parallel", "arbitrary")),
        name="ssd",
    )(proj, proj, proj, proj, dtp, prev, h0, lw["conv_w"], lw["conv_b"], lw["dt_bias"], lw["a_neg"],
      lw["d_exp"], lw["g_ssd"], lw["expand"])


def _out_proj_kernel(y_ref, o_ref, w_ref, h_ref, out_ref):
    acc = jnp.dot(y_ref[...], w_ref[0:D_SSM, :], preferred_element_type=F32)
    acc = acc + jnp.dot(o_ref[...], w_ref[D_SSM:D_SSM + D_ATT, :], preferred_element_type=F32)
    out_ref[...] = h_ref[...] + acc


def _out_proj(y, o, w, layer, h, tn=512):
    t, d = h.shape
    tm = _row_tile(t)
    return pl.pallas_call(
        _out_proj_kernel,
        grid=(t // tm, d // tn),
        in_specs=[
            pl.BlockSpec((tm, D_SSM), lambda i, j: (i, 0)),
            pl.BlockSpec((tm, D_ATT), lambda i, j: (i, 0)),
            pl.BlockSpec((None, D_SSM + D_ATT, tn), lambda i, j: (layer, 0, j)),
            pl.BlockSpec((tm, tn), lambda i, j: (i, j)),
        ],
        out_specs=pl.BlockSpec((tm, tn), lambda i, j: (i, j)),
        out_shape=jax.ShapeDtypeStruct((t, d), F32),
        compiler_params=_cparams(("parallel", "parallel")),
        name="out_proj",
    )(y, o, w, h)


def _top16(s, n_out=None):
    n, t = s.shape
    iota = lax.broadcasted_iota(jnp.int32, (n, t), 0)
    row16 = lax.broadcasted_iota(jnp.int32, (PEER_TOPK, t), 0)
    rank = jnp.full((n, t), PEER_TOPK, jnp.int32)
    vals = jnp.zeros((PEER_TOPK, t), F32)
    cnt = jnp.zeros((n_out, t), jnp.int32) if n_out else None
    for r in range(PEER_TOPK):
        m = jnp.max(s, axis=0, keepdims=True)
        idx = jnp.min(jnp.where(s == m, iota, n), axis=0, keepdims=True)
        sel = iota == idx
        rank = jnp.where(sel, r, rank)
        s = jnp.where(sel, -jnp.inf, s)
        vals = jnp.where(row16 == r, m, vals)
        if n_out:
            cnt = cnt + (row16 == jnp.right_shift(idx, TOPK_SHIFT)).astype(jnp.int32)
    return vals, rank, cnt


_MARK = 2.0 ** 126


def _peel16(s):
    t = s.shape[1]
    row16 = lax.broadcasted_iota(jnp.int32, (PEER_TOPK, t), 0)
    vals = jnp.zeros((PEER_TOPK, t), F32)
    for r in range(PEER_TOPK):
        m = jnp.max(s, axis=0, keepdims=True)
        s = jnp.where(s == m, -(1.0 + (r + 1) / 32.0) * _MARK, s)
        vals = jnp.where(row16 == r, m, vals)
    marked = s < -_MARK
    rank = jnp.where(marked, s * (-32.0 / _MARK) - 33.0, float(PEER_TOPK))
    count = jnp.sum(jnp.where(marked, 1.0, 0.0), axis=0, keepdims=True)
    return vals, rank, count


def _route_outputs(s1, s2, v1, v2, rank1, rank2, cnt, zsum):
    n1 = jnp.zeros(s1.shape, F32)
    for j in range(PEER_TOPK):
        n1 = jnp.where(rank1 == j, cnt[j:j + 1, :], n1)
    e1 = jnp.where(rank1 < PEER_TOPK, jnp.exp(s1 - v1[0:1, :]) / zsum, 0.0)
    e2 = jnp.where(rank2 < PEER_TOPK, jnp.exp(s2 - v2[0:1, :]), 0.0)
    return e1, n1, e2.astype(BF16), rank2.astype(BF16)


def _route_exact(s1, s2):
    v1, rank1, _ = _top16(s1)
    v2, rank2, _ = _top16(s2)
    cand = jnp.concatenate([v1[j:j + 1, :] + v2 for j in range(PEER_TOPK)], axis=0)
    sc, _, cnt = _top16(cand, n_out=PEER_TOPK)
    zsum = jnp.sum(jnp.exp(sc - sc[0:1, :]), axis=0, keepdims=True)
    return _route_outputs(s1, s2, v1, v2, rank1.astype(F32), rank2.astype(F32), cnt.astype(F32), zsum)


def _route_fast(s1, s2):
    half = PEER_TOPK // 2
    t = s1.shape[1]
    v1, rank1, c1 = _peel16(s1)
    v2, rank2, c2 = _peel16(s2)
    pieces = [v1[j:j + 1, :] + v2[0:half, :] for j in range(half)]
    pieces.append(v1[half:, :] + v2[0:1, :])
    pieces.append(v1[0:1, :] + v2[half:, :])
    cand = jnp.concatenate(pieces, axis=0)
    sc, rankc, cc = _peel16(cand)
    sel = rankc < PEER_TOPK
    zsum = jnp.sum(jnp.where(sel, jnp.exp(cand - sc[0:1, :]), 0.0), axis=0, keepdims=True)
    mk = jnp.where(sel, 1.0, 0.0)
    row8 = lax.broadcasted_iota(jnp.int32, (half, t), 0)
    n_lo = jnp.zeros((half, t), F32)
    for j in range(half):
        rs = jnp.sum(mk[j * half:(j + 1) * half, :], axis=0, keepdims=True)
        if j == 0:
            rs = rs + jnp.sum(mk[(half + 1) * half:, :], axis=0, keepdims=True)
        n_lo = jnp.where(row8 == j, rs, n_lo)
    cnt = jnp.concatenate([n_lo, mk[half * half:(half + 1) * half, :]], axis=0)
    bad = jnp.abs(c1 - PEER_TOPK) + jnp.abs(c2 - PEER_TOPK) + jnp.abs(cc - PEER_TOPK)
    return _route_outputs(s1, s2, v1, v2, rank1, rank2, cnt, zsum), bad


def _route_kernel(q_ref, k1_ref, k2_ref, e1_ref, n1_ref, e2_ref, r2_ref, *, tm):
    def store(ls, outs):
        e1_ref[:, ls], n1_ref[:, ls], e2_ref[:, ls], r2_ref[:, ls] = outs

    def scores(ls):
        qb = q_ref[ls, :].astype(BF16)
        s1 = lax.dot_general(k1_ref[...], qb[:, 0:N_KEYS], _NT, preferred_element_type=F32)
        s2 = lax.dot_general(k2_ref[...], qb[:, N_KEYS:2 * N_KEYS], _NT, preferred_element_type=F32)
        return s1, s2

    group = 4
    for g0 in range(0, tm // LANE, group):
        tiles = [slice(lt * LANE, (lt + 1) * LANE) for lt in range(g0, min(g0 + group, tm // LANE))]
        flags = []
        for ls in tiles:
            outs, bad = _route_fast(*scores(ls))
            store(ls, outs)
            flags.append(jnp.max(bad) > 0.0)
        for ls, flag in zip(tiles, flags):
            @pl.when(flag)
            def _(ls=ls):
                store(ls, _route_exact(*scores(ls)))


def _route(q, k1, k2, layer, tm=TM):
    t = q.shape[0]
    kd = k1.shape[-1]
    out = lambda: pl.BlockSpec((None, N_KEYS, tm), lambda i, h: (h, 0, i))
    key = lambda: pl.BlockSpec((None, None, N_KEYS, kd), lambda i, h: (layer, h, 0, 0))
    return pl.pallas_call(
        functools.partial(_route_kernel, tm=tm),
        grid=(t // tm, PEER_HEADS),
        in_specs=[pl.BlockSpec((tm, 2 * kd), lambda i, h: (i, h)), key(), key()],
        out_specs=[out(), out(), out(), out()],
        out_shape=[jax.ShapeDtypeStruct((PEER_HEADS, N_KEYS, t), F32),
                   jax.ShapeDtypeStruct((PEER_HEADS, N_KEYS, t), F32),
                   jax.ShapeDtypeStruct((PEER_HEADS, N_KEYS, t), BF16),
                   jax.ShapeDtypeStruct((PEER_HEADS, N_KEYS, t), BF16)],
        compiler_params=_cparams(("parallel", "parallel")),
        name="peer_route",
    )(q, k1, k2)


def _peer_kernel(x_ref, u_ref, vt_ref, e1_ref, n1_ref, e2_ref, r2_ref, *rest, te):
    out_ref = rest[-1]

    @pl.when(pl.program_id(1) == 0)
    def _():
        out_ref[...] = jnp.zeros_like(out_ref)

    at = lax.dot_general(u_ref[...], x_ref[...], _NT, preferred_element_type=F32)
    act = (0.5 * at * (1.0 + lax.erf(at * (2.0 ** -0.5)))).astype(BF16)
    zero = jnp.zeros((), BF16)
    parts = []
    for c in range(te // N_KEYS):
        w = None
        for h in range(PEER_HEADS):
            n1 = n1_ref[h, c:c + 1, :].astype(BF16)
            e1 = e1_ref[h, c:c + 1, :].astype(BF16)
            wh = jnp.where(r2_ref[h] < n1, e1 * e2_ref[h], zero)
            w = wh if w is None else w + wh
        parts.append(act[c * N_KEYS:(c + 1) * N_KEYS, :] * w)
    ht = jnp.concatenate(parts, axis=0)
    out_ref[...] += jnp.dot(vt_ref[...], ht, preferred_element_type=F32)


def _peer_call(x, u, vt, layer, e1, n1, e2, r2, prev, tm, first, count, te=1024):
    t, d = x.shape
    ne = u.shape[1]
    rows = te // N_KEYS
    once = pl.Buffered(1)
    in_specs = [
        pl.BlockSpec((tm, d), lambda i, j: (first + i, 0), pipeline_mode=once),
        pl.BlockSpec((None, te, d), lambda i, j: (layer, j, 0)),
        pl.BlockSpec((None, d, te), lambda i, j: (layer, 0, j)),
        pl.BlockSpec((PEER_HEADS, rows, tm), lambda i, j: (0, j, first + i)),
        pl.BlockSpec((PEER_HEADS, rows, tm), lambda i, j: (0, j, first + i)),
        pl.BlockSpec((PEER_HEADS, N_KEYS, tm), lambda i, j: (0, 0, first + i), pipeline_mode=once),
        pl.BlockSpec((PEER_HEADS, N_KEYS, tm), lambda i, j: (0, 0, first + i), pipeline_mode=once),
    ]
    args = [x, u, vt, e1, n1, e2, r2]
    aliases = {}
    if prev is not None:
        in_specs.append(pl.BlockSpec(memory_space=pl.ANY))
        args.append(prev)
        aliases = {len(args) - 1: 0}
    return pl.pallas_call(
        functools.partial(_peer_kernel, te=te),
        grid=(count, ne // te),
        in_specs=in_specs,
        out_specs=pl.BlockSpec((d, tm), lambda i, j: (0, first + i)),
        out_shape=jax.ShapeDtypeStruct((d, t), F32),
        input_output_aliases=aliases,
        compiler_params=_cparams(("parallel", "arbitrary")),
        name="peer_experts",
    )(*args)


def _peer(x, u, vt, layer, e1, n1, e2, r2):
    t = x.shape[0]
    n_wide = t // PEER_TM
    out = None
    if n_wide:
        out = _peer_call(x, u, vt, layer, e1, n1, e2, r2, None, PEER_TM, 0, n_wide)
    n_rest = (t - n_wide * PEER_TM) // TM
    if n_rest:
        out = _peer_call(x, u, vt, layer, e1, n1, e2, r2, out, TM, n_wide * (PEER_TM // TM), n_rest)
    return out


def _ple_kernel(h_ref, dt_ref, g_ref, p_ref, wg_ref, wp_ref, out_ref, h2_ref, xn_ref, *, tn):
    j = pl.program_id(1)

    @pl.when(j == 0)
    def _():
        h2 = h_ref[...] + dt_ref[...].T
        h2_ref[...] = h2
        xn_ref[...] = _rms(h2, g_ref[...]).astype(BF16)

    gate = jax.nn.sigmoid(jnp.dot(xn_ref[...], wg_ref[...], preferred_element_type=F32))
    emb = jnp.dot(p_ref[...].astype(BF16), wp_ref[...], preferred_element_type=F32)
    out_ref[...] = h2_ref[:, pl.ds(pl.multiple_of(j * tn, tn), tn)] + emb * gate


def _ple(h, delta_t, g, p, wg, wp, layer, tn=512):
    t, d = h.shape
    pd = p.shape[2]
    tm = TM
    return pl.pallas_call(
        functools.partial(_ple_kernel, tn=tn),
        grid=(t // tm, d // tn),
        in_specs=[
            pl.BlockSpec((tm, d), lambda i, j: (i, 0)),
            pl.BlockSpec((d, tm), lambda i, j: (0, i)),
            pl.BlockSpec((1, d), lambda i, j: (0, 0)),
            pl.BlockSpec((None, tm, pd), lambda i, j: (layer, i, 0)),
            pl.BlockSpec((None, d, tn), lambda i, j: (layer, 0, j)),
            pl.BlockSpec((None, pd, tn), lambda i, j: (layer, 0, j)),
        ],
        out_specs=pl.BlockSpec((tm, tn), lambda i, j: (i, j)),
        out_shape=jax.ShapeDtypeStruct((t, d), F32),
        scratch_shapes=[pltpu.VMEM((tm, d), F32), pltpu.VMEM((tm, d), BF16)],
        compiler_params=_cparams(("parallel", "arbitrary")),
        name="ple_gate",
    )(h, delta_t, g.reshape(1, d), p, wg, wp)


def _lambda_init(layer_idx):
    return 0.8 - 0.6 * math.exp(-0.3 * layer_idx)


def _pad_rows(x, rows):
    return jnp.pad(x, ((0, rows - x.shape[0]),) + ((0, 0),) * (x.ndim - 1))


def kernel(x_prompt, x_sample, cache_k, cache_v, state_ssm, state_conv, p_prompt, p_sample, g_mix, w_in,
           conv_w, conv_b, dt_bias, a_log, d_skip, g_ssd, q_gain, k_gain, lam_q1, lam_k1, lam_q2, lam_k2,
           g_sub, w_out, g_ffn, peer_wq, peer_k1, peer_k2, peer_u, peer_v, g_ple, w_ple, w_pgate):
    nbp, seq, d = x_prompt.shape
    nbs, dseq, _ = x_sample.shape
    depth = w_in.shape[0]
    past = cache_k.shape[2]
    n_p = nbp * seq
    n_s = nbs * dseq
    t_pad = -(-(n_p + n_s) // TM) * TM

    xp = x_prompt.reshape(n_p, d)
    h = jnp.concatenate([xp, x_sample.reshape(n_s, d), xp[:t_pad - n_p - n_s]], axis=0)
    ple_dim = p_prompt.shape[-1]
    p_all = jnp.concatenate([p_prompt.reshape(depth, n_p, ple_dim), p_sample.reshape(depth, n_s, ple_dim)], axis=1)
    p_all = jnp.pad(p_all, ((0, 0), (0, t_pad - n_p - n_s), (0, 0)))
    ck = jnp.transpose(cache_k, (0, 1, 3, 4, 5, 2))
    cv = cache_v.reshape(depth * nbs * past * H_A, DH_V)
    expand = jnp.tile((jnp.arange(LANE)[:, None] == jnp.arange(D_SSM)[None, :] // SSM_P).astype(BF16), (3, 1))
    prev_zero = jnp.zeros((nbp, 8, CONV_DIM), F32)
    h0_zero = jnp.zeros((nbp, D_SSM, SSM_N), F32)

    w_main = w_in.astype(BF16)
    w_dt = jnp.pad(w_in[:, :, MAIN_DIM:], ((0, 0), (0, 0), (0, LANE - H_S))).astype(BF16)
    w_out_b = w_out.astype(BF16)
    wq_b = peer_wq.astype(BF16)
    k1_b = peer_k1.astype(BF16)
    k2_b = peer_k2.astype(BF16)
    u_b = peer_u.astype(BF16)
    vt_b = jnp.swapaxes(peer_v, 1, 2).astype(BF16)
    wg_b = w_pgate.astype(BF16)
    wp_b = w_ple.astype(BF16)

    vp_all = jnp.zeros((depth, n_p, D_ATT), F32)
    vs_all = jnp.zeros((depth, n_s, D_ATT), F32)

    outs = [[] for _ in range(6)]
    for l in range(depth):
        lam0 = _lambda_init(l)
        lam = (jnp.exp(jnp.sum(lam_q1[l] * lam_k1[l])) - jnp.exp(jnp.sum(lam_q2[l] * lam_k2[l])) + lam0)
        lam_s = jnp.stack([lam, jnp.asarray(1.0 - lam0, F32)]).astype(F32)
        lw = dict(
            conv_w=conv_w[l], conv_b=conv_b[l].reshape(1, CONV_DIM),
            dt_bias=_pad_rows(dt_bias[l], LANE).reshape(1, LANE),
            a_neg=_pad_rows(-jnp.exp(a_log[l]), LANE).reshape(1, LANE),
            d_exp=jnp.repeat(d_skip[l], SSM_P).reshape(1, D_SSM),
            g_ssd=g_ssd[l].reshape(1, D_SSM), expand=expand)

        proj, dtp = _norm_proj(h, g_mix[l], w_main, l, MAIN_DIM, wdt=w_dt)
        qn, kb, vb, kf, vp_all, vs_all = _qk_norm(proj, q_gain[l], k_gain[l], l, vp_all, vs_all)
        o_p = _attn_prompt(lam_s, qn, kb, vb, g_sub[l], nbp, seq)
        o_s = _attn_sample(lam_s, qn, kb, vb, ck, cv, l, g_sub[l], n_p, nbs, dseq)
        y_p, hT_p = _ssd(proj, dtp, prev_zero, h0_zero, lw, nbp, seq, SSD_Q)
        proj_s = proj[n_p:n_p + n_s].reshape(nbs, dseq, MAIN_DIM)
        pad_seq = lambda a: jnp.pad(a, ((0, 0), (0, SSD_Q - dseq), (0, 0))).reshape(nbs * SSD_Q, a.shape[-1])
        prev_s = jnp.pad(state_conv[l], ((0, 0), (8 - (CONV_W - 1), 0), (0, 0)))
        y_s, hT_s = _ssd(pad_seq(proj_s), pad_seq(dtp[n_p:n_p + n_s].reshape(nbs, dseq, LANE)), prev_s,
                         state_ssm[l].reshape(nbs, D_SSM, SSM_N), lw, nbs, SSD_Q, dseq)
        y_s = y_s.reshape(nbs, SSD_Q, D_SSM)[:, :dseq].reshape(n_s, D_SSM)
        y_all = _pad_rows(jnp.concatenate([y_p, y_s], axis=0), t_pad)
        o_all = _pad_rows(jnp.concatenate([o_p, o_s], axis=0), t_pad)
        h = _out_proj(y_all, o_all, w_out_b, l, h)

        q_peer, c = _norm_proj(h, g_ffn[l], wq_b, l, d, with_xn=True)
        e1, n1, e2, r2 = _route(q_peer, k1_b, k2_b, l)
        delta_t = _peer(c, u_b, vt_b, l, e1, n1, e2, r2)

        h = _ple(h, delta_t, g_ple[l], p_all, wg_b, wp_b, l)

        xbc0 = 2 * D_ATT + D_ATT + D_SSM
        tail = CONV_W - 1
        outs[0].append(hT_p.reshape(nbp, H_S, SSM_P, SSM_N))
        outs[1].append(jnp.stack([proj[(b + 1) * seq - tail:(b + 1) * seq, xbc0:] for b in range(nbp)]))
        outs[2].append(hT_s.reshape(nbs, H_S, SSM_P, SSM_N))
        outs[3].append(proj_s[:, dseq - tail:, xbc0:])
        outs[4].append(kf[:n_p].reshape(nbp, seq, H_A, 2, DH_QK))
        outs[5].append(kf[n_p:n_p + n_s].reshape(nbs, dseq, H_A, 2, DH_QK))

    y_prompt = h[:n_p].reshape(nbp, seq, d)
    y_sample = h[n_p:n_p + n_s].reshape(nbs, dseq, d)
    ssm_p, conv_p, ssm_s, conv_s, k_p, k_s = (jnp.stack(o) for o in outs)
    return (y_prompt, y_sample,
            k_p, vp_all.reshape(depth, nbp, seq, H_A, DH_V), ssm_p, conv_p,
            k_s, vs_all.reshape(depth, nbs, dseq, H_A, DH_V), ssm_s, conv_s)
```

```python
import functools
import math

import jax
import jax.numpy as jnp
from jax import lax
from jax.experimental import pallas as pl
from jax.experimental.pallas import tpu as pltpu

F32 = jnp.float32
BF16 = jnp.bfloat16

EPS = 1e-6
CHUNK = 64
CHUNK_SHIFT = 6
H_A = 8
DH_QK = 64
DH_V = 128
D_ATT = H_A * DH_V
ATT_SCALE = DH_QK ** -0.5
SSM_P = 64
H_S = 16
SSM_N = 128
SSM_G = 2
D_SSM = H_S * SSM_P
CONV_W = 4
CONV_DIM = D_SSM + 2 * SSM_G * SSM_N
MAIN_DIM = 2 * D_ATT + D_ATT + D_SSM + CONV_DIM
N_KEYS = 128
PEER_HEADS = 8
PEER_TOPK = 16
TOPK_SHIFT = 4

LANE = 128
TM = 512
PEER_TM = 1024
MM_ROWS = 1152
SSD_Q = 128
VMEM_LIMIT = 56 * 1024 * 1024

_NT = (((1,), (1,)), ((), ()))


def _cparams(sem):
    return pltpu.CompilerParams(dimension_semantics=sem, vmem_limit_bytes=VMEM_LIMIT)


def _row_tile(t):
    for k in range(1, t // 16 + 1):
        if t % k == 0 and (t // k) % 16 == 0 and t // k <= MM_ROWS:
            return t // k
    return TM


def _rms(x, g):
    ms = jnp.mean(x * x, axis=-1, keepdims=True)
    return x * lax.rsqrt(ms + EPS) * g


def _norm_proj_kernel(x_ref, g_ref, w_ref, *rest, with_dt, with_xn):
    rest = list(rest)
    wdt_ref = rest.pop(0) if with_dt else None
    o_ref = rest.pop(0)
    odt_ref = rest.pop(0) if with_dt else None
    oxn_ref = rest.pop(0) if with_xn else None
    xn_ref = rest.pop(0)

    @pl.when(pl.program_id(1) == 0)
    def _():
        xn = _rms(x_ref[...], g_ref[...]).astype(BF16)
        xn_ref[...] = xn
        if with_xn:
            oxn_ref[...] = xn
        if with_dt:
            odt_ref[...] = jnp.dot(xn, wdt_ref[...], preferred_element_type=F32)

    o_ref[...] = jnp.dot(xn_ref[...], w_ref[...], preferred_element_type=F32)


def _norm_proj(x, g, w, layer, n, wdt=None, with_xn=False, tn=512):
    t, d = x.shape
    tm = _row_tile(t)
    with_dt = wdt is not None
    in_specs = [
        pl.BlockSpec((tm, d), lambda i, j: (i, 0)),
        pl.BlockSpec((1, d), lambda i, j: (0, 0)),
        pl.BlockSpec((None, d, tn), lambda i, j: (layer, 0, j)),
    ]
    args = [x, g.reshape(1, d), w]
    out_shape = [jax.ShapeDtypeStruct((t, n), F32)]
    out_specs = [pl.BlockSpec((tm, tn), lambda i, j: (i, j))]
    if with_dt:
        in_specs.append(pl.BlockSpec((None, d, LANE), lambda i, j: (layer, 0, 0)))
        args.append(wdt)
        out_shape.append(jax.ShapeDtypeStruct((t, LANE), F32))
        out_specs.append(pl.BlockSpec((tm, LANE), lambda i, j: (i, 0)))
    if with_xn:
        out_shape.append(jax.ShapeDtypeStruct((t, d), BF16))
        out_specs.append(pl.BlockSpec((tm, d), lambda i, j: (i, 0)))
    return pl.pallas_call(
        functools.partial(_norm_proj_kernel, with_dt=with_dt, with_xn=with_xn),
        grid=(t // tm, n // tn),
        in_specs=in_specs,
        out_specs=out_specs,
        out_shape=out_shape,
        scratch_shapes=[pltpu.VMEM((tm, d), BF16)],
        compiler_params=_cparams(("parallel", "arbitrary")),
        name="norm_proj",
    )(*args)


def _qk_norm_kernel(q_ref, k_ref, v_ref, qg_ref, kg_ref, vp_in, vs_in,
                    qn_ref, kb_ref, vb_ref, kf_ref, vp_ref, vs_ref, *, np_tiles, n_s):
    del vp_in, vs_in
    i = pl.program_id(0)
    lane = lax.broadcasted_iota(jnp.int32, (1, DH_V), 1)
    lo = lane < DH_QK

    def norm(x, g):
        xx = x * x
        s_lo = jnp.sum(jnp.where(lo, xx, 0.0), axis=-1, keepdims=True)
        s_all = jnp.sum(xx, axis=-1, keepdims=True)
        ms = jnp.where(lo, s_lo, s_all - s_lo) * (1.0 / DH_QK)
        return x * lax.rsqrt(ms + EPS) * g

    for h in range(H_A):
        sl = slice(h * DH_V, (h + 1) * DH_V)
        qn_ref[:, sl] = (norm(q_ref[:, sl], qg_ref[...]) * ATT_SCALE).astype(BF16)
        kn = norm(k_ref[:, sl], kg_ref[...])
        kf_ref[:, sl] = kn
        kb_ref[:, sl] = kn.astype(BF16)
    vb_ref[...] = v_ref[...].astype(BF16)

    @pl.when(i < np_tiles)
    def _():
        vp_ref[...] = v_ref[...]

    @pl.when(i == np_tiles)
    def _():
        vs_ref[...] = v_ref[0:n_s, :]


def _qk_norm(proj, q_gain, k_gain, layer, vp, vs):
    t = proj.shape[0]
    n_p, n_s = vp.shape[1], vs.shape[1]
    assert n_p % TM == 0 and n_s <= TM and t >= n_p + TM
    np_tiles = n_p // TM
    blk = lambda c: pl.BlockSpec((TM, D_ATT), lambda i, c=c: (i, c))
    gspec = pl.BlockSpec((1, DH_V), lambda i: (0, 0))
    hbm = pl.BlockSpec(memory_space=pl.ANY)
    p_out = pl.BlockSpec((None, TM, D_ATT), lambda i: (layer, jnp.minimum(i, np_tiles - 1), 0))
    s_out = pl.BlockSpec((None, n_s, D_ATT), lambda i: (layer, 0, 0))
    return pl.pallas_call(
        functools.partial(_qk_norm_kernel, np_tiles=np_tiles, n_s=n_s),
        grid=(t // TM,),
        in_specs=[blk(0), blk(1), blk(2), gspec, gspec, hbm, hbm],
        out_specs=[blk(0), blk(0), blk(0), blk(0), p_out, s_out],
        out_shape=[
            jax.ShapeDtypeStruct((t, D_ATT), BF16),
            jax.ShapeDtypeStruct((t, D_ATT), BF16),
            jax.ShapeDtypeStruct((t, D_ATT), BF16),
            jax.ShapeDtypeStruct((t, D_ATT), F32),
            jax.ShapeDtypeStruct(vp.shape, F32),
            jax.ShapeDtypeStruct(vs.shape, F32),
        ],
        input_output_aliases={5: 4, 6: 5},
        compiler_params=_cparams(("arbitrary",)),
        name="qk_norm",
    )(proj, proj, proj, q_gain.reshape(1, DH_V), k_gain.reshape(1, DH_V), vp, vs)


def _sub_norm(o, lam_ref, gsub):
    return _rms(o, gsub) * lam_ref[1]


def _attn_prompt_kernel(lam_ref, q_ref, k_ref, v_ref, gsub_ref, o_ref, *, tq, tk):
    qi = pl.program_id(2)
    q = q_ref[...]
    lane = lax.broadcasted_iota(jnp.int32, (1, DH_V), 1)
    zero = jnp.zeros((), BF16)
    q0 = jnp.where(lane < DH_QK, q, zero)
    q1 = jnp.where(lane >= DH_QK, q, zero)

    def block(start, mask):
        kb = k_ref[pl.ds(start, tk), :]
        s0 = lax.dot_general(q0, kb, _NT, preferred_element_type=F32)
        s1 = lax.dot_general(q1, kb, _NT, preferred_element_type=F32)
        if mask is not None:
            s0 = jnp.where(mask, s0, -jnp.inf)
            s1 = jnp.where(mask, s1, -jnp.inf)
        return s0, s1, v_ref[pl.ds(start, tk), :]

    def upd(s, vb, m, l, a):
        mn = jnp.maximum(m, jnp.max(s, axis=-1, keepdims=True))
        alpha = jnp.exp(m - mn)
        p = jnp.exp(s - mn)
        l = alpha * l + jnp.sum(p, axis=-1, keepdims=True)
        a = alpha * a + jnp.dot(p.astype(BF16), vb, preferred_element_type=F32)
        return mn, l, a

    row = jnp.right_shift(lax.broadcasted_iota(jnp.int32, (tq, tk), 0), CHUNK_SHIFT)
    col = jnp.right_shift(lax.broadcasted_iota(jnp.int32, (tq, tk), 1), CHUNK_SHIFT)
    base = pl.multiple_of(qi * tq, tq)
    s0, s1, vb = block(base, col <= row)
    m0 = jnp.max(s0, axis=-1, keepdims=True)
    m1 = jnp.max(s1, axis=-1, keepdims=True)
    p0 = jnp.exp(s0 - m0)
    p1 = jnp.exp(s1 - m1)
    l0 = jnp.sum(p0, axis=-1, keepdims=True)
    l1 = jnp.sum(p1, axis=-1, keepdims=True)
    a0 = jnp.dot(p0.astype(BF16), vb, preferred_element_type=F32)
    a1 = jnp.dot(p1.astype(BF16), vb, preferred_element_type=F32)
    for dblk in range(1, tq // tk):
        s0, s1, vb = block(base + dblk * tk, col + dblk * (tk // CHUNK) <= row)
        m0, l0, a0 = upd(s0, vb, m0, l0, a0)
        m1, l1, a1 = upd(s1, vb, m1, l1, a1)

    def body(j, carry):
        m0, l0, a0, m1, l1, a1 = carry
        s0, s1, vb = block(pl.multiple_of(j * tk, tk), None)
        m0, l0, a0 = upd(s0, vb, m0, l0, a0)
        m1, l1, a1 = upd(s1, vb, m1, l1, a1)
        return m0, l0, a0, m1, l1, a1

    m0, l0, a0, m1, l1, a1 = lax.fori_loop(0, qi * (tq // tk), body, (m0, l0, a0, m1, l1, a1))
    o = a0 / l0 - lam_ref[0] * (a1 / l1)
    o_ref[...] = _sub_norm(o, lam_ref, gsub_ref[...]).astype(BF16)


def _attn_prompt(lam, qn, kb, vb, g_sub, nb, seq, tq=512, tk=512):
    tq = min(tq, seq)
    tk = min(tk, tq)
    nq = seq // tq
    return pl.pallas_call(
        functools.partial(_attn_prompt_kernel, tq=tq, tk=tk),
        grid=(nb, H_A, nq),
        in_specs=[
            pl.BlockSpec(memory_space=pltpu.SMEM),
            pl.BlockSpec((tq, DH_V), lambda b, h, i: (b * nq + i, h)),
            pl.BlockSpec((seq, DH_V), lambda b, h, i: (b, h)),
            pl.BlockSpec((seq, DH_V), lambda b, h, i: (b, h)),
            pl.BlockSpec((1, DH_V), lambda b, h, i: (0, 0)),
        ],
        out_specs=pl.BlockSpec((tq, DH_V), lambda b, h, i: (b * nq + i, h)),
        out_shape=jax.ShapeDtypeStruct((nb * seq, D_ATT), BF16),
        compiler_params=_cparams(("parallel", "parallel", "arbitrary")),
        name="attn_prompt",
    )(lam, qn, kb, vb, g_sub.reshape(1, DH_V))


def _attn_sample_kernel(lam_ref, q_ref, kn_ref, vn_ref, ckt_ref, cv_ref, gsub_ref, o_ref, m_ref, l_ref, acc_ref, *, tk):
    c = pl.program_id(1)
    nq = q_ref.shape[0]
    lane = lax.broadcasted_iota(jnp.int32, (1, DH_V), 1)
    zero = jnp.zeros((), BF16)

    @pl.when(c == 0)
    def _():
        m_ref[...] = jnp.full(m_ref.shape, -jnp.inf, F32)
        l_ref[...] = jnp.zeros(l_ref.shape, F32)
        acc_ref[...] = jnp.zeros(acc_ref.shape, F32)

    def q01(h):
        q = q_ref[:, h * DH_V:(h + 1) * DH_V]
        return jnp.concatenate([jnp.where(lane < DH_QK, q, zero), jnp.where(lane >= DH_QK, q, zero)], axis=0)

    def update(h, s, v):
        m_old = m_ref[h]
        m_new = jnp.maximum(m_old, jnp.max(s, axis=-1, keepdims=True))
        alpha = jnp.exp(m_old - m_new)
        p = jnp.exp(s - m_new[:, 0:1])
        l_ref[h] = alpha * l_ref[h] + jnp.sum(p, axis=-1, keepdims=True)
        acc_ref[h] = alpha * acc_ref[h] + jnp.dot(p.astype(BF16), v, preferred_element_type=F32)
        m_ref[h] = m_new

    for h in range(H_A):
        kt = ckt_ref[h].reshape(2 * DH_QK, tk).astype(BF16)
        s = jnp.dot(q01(h), kt, preferred_element_type=F32)
        update(h, s, cv_ref[pl.ds(h, tk, stride=H_A), :].astype(BF16))

    @pl.when(c == pl.num_programs(1) - 1)
    def _():
        for h in range(H_A):
            sl = slice(h * DH_V, (h + 1) * DH_V)
            s = lax.dot_general(q01(h), kn_ref[:, sl], _NT, preferred_element_type=F32)
            update(h, s, vn_ref[:, sl])
            a = acc_ref[h] / l_ref[h]
            o = a[0:nq] - lam_ref[0] * a[nq:2 * nq]
            o_ref[:, sl] = _sub_norm(o, lam_ref, gsub_ref[...]).astype(BF16)


def _attn_sample(lam, qn, kb, vb, ckt, cv2, layer, g_sub, row0, nb, nq, tk=1024):
    past = ckt.shape[-1]
    tk = min(tk, past)
    nc = past // tk
    r0 = row0 // nq
    new = lambda: pl.BlockSpec((nq, D_ATT), lambda b, c: (r0 + b, 0))
    return pl.pallas_call(
        functools.partial(_attn_sample_kernel, tk=tk),
        grid=(nb, nc),
        in_specs=[pl.BlockSpec(memory_space=pltpu.SMEM), new(), new(), new(),
                  pl.BlockSpec((None, None, H_A, 2, DH_QK, tk), lambda b, c: (layer, b, 0, 0, 0, c)),
                  pl.BlockSpec((tk * H_A, DH_V), lambda b, c: ((layer * nb + b) * nc + c, 0)),
                  pl.BlockSpec((1, DH_V), lambda b, c: (0, 0))],
        out_specs=pl.BlockSpec((nq, D_ATT), lambda b, c: (b, 0)),
        out_shape=jax.ShapeDtypeStruct((nb * nq, D_ATT), BF16),
        scratch_shapes=[pltpu.VMEM((H_A, 2 * nq, DH_V), F32), pltpu.VMEM((H_A, 2 * nq, DH_V), F32),
                        pltpu.VMEM((H_A, 2 * nq, DH_V), F32)],
        compiler_params=_cparams(("parallel", "arbitrary")),
        name="attn_sample",
    )(lam, qn, kb, vb, ckt, cv2, g_sub.reshape(1, DH_V))


def _split3(x, axis):
    hi = x.astype(BF16)
    r = x - hi.astype(F32)
    mid = r.astype(BF16)
    lo = (r - mid.astype(F32)).astype(BF16)
    return jnp.concatenate([hi, mid, lo], axis=axis)


def _ssd_kernel(xs_ref, b_ref, c_ref, z_ref, dt_ref, prev_ref, h0_ref, cw_ref, cb_ref, dtb_ref,
                aneg_ref, dexp_ref, gssd_ref, expand_ref, y_ref, hout_ref, ext_ref, ht_ref, *, valid):
    q = SSD_Q
    c = pl.program_id(1)
    halo = 8

    @pl.when(c == 0)
    def _():
        ext_ref[0:halo, :] = prev_ref[...]
        ht_ref[...] = h0_ref[...].T

    ext_ref[halo:halo + q, 0:D_SSM] = xs_ref[...]
    ext_ref[halo:halo + q, D_SSM:D_SSM + SSM_G * SSM_N] = b_ref[...]
    ext_ref[halo:halo + q, D_SSM + SSM_G * SSM_N:CONV_DIM] = c_ref[...]
    conv = cb_ref[...]
    for j in range(CONV_W):
        r0 = halo - (CONV_W - 1) + j
        conv = conv + ext_ref[r0:r0 + q, :] * cw_ref[j:j + 1, :]
    tail = ext_ref[q:q + halo, :]
    ext_ref[0:halo, :] = tail
    xbc = conv * jax.nn.sigmoid(conv)
    xs = xbc[:, 0:D_SSM]
    bm = xbc[:, D_SSM:D_SSM + SSM_G * SSM_N]
    cm = xbc[:, D_SSM + SSM_G * SSM_N:CONV_DIM]

    dt = jax.nn.softplus(dt_ref[...] + dtb_ref[...])
    if valid < q:
        rowv = lax.broadcasted_iota(jnp.int32, (q, LANE), 0) < valid
        dt = jnp.where(rowv, dt, 0.0)
    a = dt * aneg_ref[...]
    r_i = lax.broadcasted_iota(jnp.int32, (q, q), 0)
    c_i = lax.broadcasted_iota(jnp.int32, (q, q), 1)
    tril = c_i <= r_i
    tril3 = jnp.tile(tril.astype(BF16), (1, 3))
    acum = jnp.dot(tril3, _split3(a, 0), preferred_element_type=F32)
    eye3 = jnp.tile((lax.broadcasted_iota(jnp.int32, (LANE, LANE), 0)
                     == lax.broadcasted_iota(jnp.int32, (LANE, LANE), 1)).astype(BF16), (1, 3))
    acum_t = lax.dot_general(eye3, _split3(acum, 1), _NT, preferred_element_type=F32)
    a_last = acum[q - 1:q, :]
    decay_in = jnp.exp(a_last - acum)
    stack = jnp.concatenate(
        [dt, dt * decay_in, jnp.exp(acum), jnp.broadcast_to(jnp.exp(a_last), (8, LANE))], axis=0)
    ex = jnp.dot(_split3(stack, 1), expand_ref[...], preferred_element_type=F32)
    dt_e = ex[0:q]
    dtd_e = ex[q:2 * q]
    eacum_e = ex[2 * q:3 * q]
    cd_e = ex[3 * q:3 * q + 1]

    xdt = (xs * dt_e).astype(BF16)
    xdtd = (xs * dtd_e).astype(BF16)
    lane = lax.broadcasted_iota(jnp.int32, (1, LANE), 1)
    zero = jnp.zeros((), BF16)
    gw = D_SSM // SSM_G
    hpg = H_S // SSM_G
    y_diag_parts = []
    y_off_parts = []
    for g in range(SSM_G):
        gs = slice(g * gw, (g + 1) * gw)
        bg = bm[:, g * SSM_N:(g + 1) * SSM_N]
        cg = cm[:, g * SSM_N:(g + 1) * SSM_N].astype(BF16)
        cb = lax.dot_general(cg, bg.astype(BF16), _NT, preferred_element_type=F32)
        ht_prev = ht_ref[:, gs]
        y_off = jnp.dot(cg, ht_prev.astype(BF16), preferred_element_type=F32)
        ht_ref[:, gs] = cd_e[:, gs] * ht_prev + jnp.dot(
            bg.T.astype(BF16), xdtd[:, gs], preferred_element_type=F32)
        for k in range(hpg // 2):
            ms = []
            for h in (g * hpg + 2 * k, g * hpg + 2 * k + 1):
                seg = acum[:, h:h + 1] - acum_t[h:h + 1, :]
                ms.append((cb * jnp.exp(jnp.where(tril, seg, -jnp.inf))).astype(BF16))
            pair = slice((g * hpg + 2 * k) * SSM_P, (g * hpg + 2 * k + 2) * SSM_P)
            xp = xdt[:, pair]
            rhs = jnp.concatenate([jnp.where(lane < SSM_P, xp, zero), jnp.where(lane >= SSM_P, xp, zero)], axis=0)
            y_diag_parts.append(jnp.dot(jnp.concatenate(ms, axis=1), rhs, preferred_element_type=F32))
        y_off_parts.append(y_off)
    y_diag = jnp.concatenate(y_diag_parts, axis=1)
    y_off = jnp.concatenate(y_off_parts, axis=1)
    y = y_diag + y_off * eacum_e + dexp_ref[...] * xs
    zz = z_ref[...]
    y = y * (zz * jax.nn.sigmoid(zz))
    for g in range(SSM_G):
        gs = slice(g * gw, (g + 1) * gw)
        y_ref[:, gs] = _rms(y[:, gs], gssd_ref[:, gs]).astype(BF16)

    @pl.when(c == pl.num_programs(1) - 1)
    def _():
        hout_ref[...] = ht_ref[...].T


def _ssd(proj, dtp, prev, h0, lw, nb, seq, valid):
    nc = seq // SSD_Q
    row = lambda w, col: pl.BlockSpec((SSD_Q, w), lambda b, c, col=col: (b * nc + c, col))
    const = lambda shp: pl.BlockSpec(shp, lambda b, c: (0,) * len(shp))
    per_b = lambda shp: pl.BlockSpec((None,) + shp, lambda b, c: (b, 0, 0))
    return pl.pallas_call(
        functools.partial(_ssd_kernel, valid=valid),
        grid=(nb, nc),
        in_specs=[
            row(D_SSM, 4), row(SSM_G * SSM_N, 20), row(SSM_G * SSM_N, 21), row(D_SSM, 3), row(LANE, 0),
            per_b((8, CONV_DIM)), per_b((D_SSM, SSM_N)),
            const((CONV_W, CONV_DIM)), const((1, CONV_DIM)), const((1, LANE)), const((1, LANE)),
            const((1, D_SSM)), const((1, D_SSM)), const((3 * LANE, D_SSM)),
        ],
        out_specs=[row(D_SSM, 0), per_b((D_SSM, SSM_N))],
        out_shape=[jax.ShapeDtypeStruct((nb * seq, D_SSM), BF16),
                   jax.ShapeDtypeStruct((nb, D_SSM, SSM_N), F32)],
        scratch_shapes=[pltpu.VMEM((SSD_Q + 8, CONV_DIM), F32), pltpu.VMEM((SSM_N, D_SSM), F32)],
        compiler_params=_cparams(("parallel", "arbitrary")),
        name="ssd",
    )(proj, proj, proj, proj, dtp, prev, h0, lw["conv_w"], lw["conv_b"], lw["dt_bias"], lw["a_neg"],
      lw["d_exp"], lw["g_ssd"], lw["expand"])


def _out_proj_kernel(y_ref, o_ref, w_ref, h_ref, out_ref):
    acc = jnp.dot(y_ref[...], w_ref[0:D_SSM, :], preferred_element_type=F32)
    acc = acc + jnp.dot(o_ref[...], w_ref[D_SSM:D_SSM + D_ATT, :], preferred_element_type=F32)
    out_ref[...] = h_ref[...] + acc


def _out_proj(y, o, w, layer, h, tn=512):
    t, d = h.shape
    tm = _row_tile(t)
    return pl.pallas_call(
        _out_proj_kernel,
        grid=(t // tm, d // tn),
        in_specs=[
            pl.BlockSpec((tm, D_SSM), lambda i, j: (i, 0)),
            pl.BlockSpec((tm, D_ATT), lambda i, j: (i, 0)),
            pl.BlockSpec((None, D_SSM + D_ATT, tn), lambda i, j: (layer, 0, j)),
            pl.BlockSpec((tm, tn), lambda i, j: (i, j)),
        ],
        out_specs=pl.BlockSpec((tm, tn), lambda i, j: (i, j)),
        out_shape=jax.ShapeDtypeStruct((t, d), F32),
        compiler_params=_cparams(("parallel", "parallel")),
        name="out_proj",
    )(y, o, w, h)


def _top16(s, n_out=None):
    n, t = s.shape
    iota = lax.broadcasted_iota(jnp.int32, (n, t), 0)
    row16 = lax.broadcasted_iota(jnp.int32, (PEER_TOPK, t), 0)
    rank = jnp.full((n, t), PEER_TOPK, jnp.int32)
    vals = jnp.zeros((PEER_TOPK, t), F32)
    cnt = jnp.zeros((n_out, t), jnp.int32) if n_out else None
    for r in range(PEER_TOPK):
        m = jnp.max(s, axis=0, keepdims=True)
        idx = jnp.min(jnp.where(s == m, iota, n), axis=0, keepdims=True)
        sel = iota == idx
        rank = jnp.where(sel, r, rank)
        s = jnp.where(sel, -jnp.inf, s)
        vals = jnp.where(row16 == r, m, vals)
        if n_out:
            cnt = cnt + (row16 == jnp.right_shift(idx, TOPK_SHIFT)).astype(jnp.int32)
    return vals, rank, cnt


_MARK = 2.0 ** 126


def _peel16(s):
    t = s.shape[1]
    row16 = lax.broadcasted_iota(jnp.int32, (PEER_TOPK, t), 0)
    vals = jnp.zeros((PEER_TOPK, t), F32)
    for r in range(PEER_TOPK):
        m = jnp.max(s, axis=0, keepdims=True)
        s = jnp.where(s == m, -(1.0 + (r + 1) / 32.0) * _MARK, s)
        vals = jnp.where(row16 == r, m, vals)
    marked = s < -_MARK
    rank = jnp.where(marked, s * (-32.0 / _MARK) - 33.0, float(PEER_TOPK))
    count = jnp.sum(jnp.where(marked, 1.0, 0.0), axis=0, keepdims=True)
    return vals, rank, count


def _route_outputs(s1, s2, v1, v2, rank1, rank2, cnt, zsum):
    n1 = jnp.zeros(s1.shape, F32)
    for j in range(PEER_TOPK):
        n1 = jnp.where(rank1 == j, cnt[j:j + 1, :], n1)
    e1 = jnp.where(rank1 < PEER_TOPK, jnp.exp(s1 - v1[0:1, :]) / zsum, 0.0)
    e2 = jnp.where(rank2 < PEER_TOPK, jnp.exp(s2 - v2[0:1, :]), 0.0)
    return e1, n1, e2.astype(BF16), rank2.astype(BF16)


def _route_exact(s1, s2):
    v1, rank1, _ = _top16(s1)
    v2, rank2, _ = _top16(s2)
    cand = jnp.concatenate([v1[j:j + 1, :] + v2 for j in range(PEER_TOPK)], axis=0)
    sc, _, cnt = _top16(cand, n_out=PEER_TOPK)
    zsum = jnp.sum(jnp.exp(sc - sc[0:1, :]), axis=0, keepdims=True)
    return _route_outputs(s1, s2, v1, v2, rank1.astype(F32), rank2.astype(F32), cnt.astype(F32), zsum)


def _route_fast(s1, s2):
    half = PEER_TOPK // 2
    t = s1.shape[1]
    v1, rank1, c1 = _peel16(s1)
    v2, rank2, c2 = _peel16(s2)
    pieces = [v1[j:j + 1, :] + v2[0:half, :] for j in range(half)]
    pieces.append(v1[half:, :] + v2[0:1, :])
    pieces.append(v1[0:1, :] + v2[half:, :])
    cand = jnp.concatenate(pieces, axis=0)
    sc, rankc, cc = _peel16(cand)
    sel = rankc < PEER_TOPK
    zsum = jnp.sum(jnp.where(sel, jnp.exp(cand - sc[0:1, :]), 0.0), axis=0, keepdims=True)
    mk = jnp.where(sel, 1.0, 0.0)
    row8 = lax.broadcasted_iota(jnp.int32, (half, t), 0)
    n_lo = jnp.zeros((half, t), F32)
    for j in range(half):
        rs = jnp.sum(mk[j * half:(j + 1) * half, :], axis=0, keepdims=True)
        if j == 0:
            rs = rs + jnp.sum(mk[(half + 1) * half:, :], axis=0, keepdims=True)
        n_lo = jnp.where(row8 == j, rs, n_lo)
    cnt = jnp.concatenate([n_lo, mk[half * half:(half + 1) * half, :]], axis=0)
    bad = jnp.abs(c1 - PEER_TOPK) + jnp.abs(c2 - PEER_TOPK) + jnp.abs(cc - PEER_TOPK)
    return _route_outputs(s1, s2, v1, v2, rank1, rank2, cnt, zsum), bad


def _route_kernel(q_ref, k1_ref, k2_ref, e1_ref, n1_ref, e2_ref, r2_ref, *, tm):
    def store(ls, outs):
        e1_ref[:, ls], n1_ref[:, ls], e2_ref[:, ls], r2_ref[:, ls] = outs

    def scores(ls):
        qb = q_ref[ls, :].astype(BF16)
        s1 = lax.dot_general(k1_ref[...], qb[:, 0:N_KEYS], _NT, preferred_element_type=F32)
        s2 = lax.dot_general(k2_ref[...], qb[:, N_KEYS:2 * N_KEYS], _NT, preferred_element_type=F32)
        return s1, s2

    group = 4
    for g0 in range(0, tm // LANE, group):
        tiles = [slice(lt * LANE, (lt + 1) * LANE) for lt in range(g0, min(g0 + group, tm // LANE))]
        flags = []
        for ls in tiles:
            outs, bad = _route_fast(*scores(ls))
            store(ls, outs)
            flags.append(jnp.max(bad) > 0.0)
        for ls, flag in zip(tiles, flags):
            @pl.when(flag)
            def _(ls=ls):
                store(ls, _route_exact(*scores(ls)))


def _route(q, k1, k2, layer, tm=TM):
    t = q.shape[0]
    kd = k1.shape[-1]
    out = lambda: pl.BlockSpec((None, N_KEYS, tm), lambda i, h: (h, 0, i))
    key = lambda: pl.BlockSpec((None, None, N_KEYS, kd), lambda i, h: (layer, h, 0, 0))
    return pl.pallas_call(
        functools.partial(_route_kernel, tm=tm),
        grid=(t // tm, PEER_HEADS),
        in_specs=[pl.BlockSpec((tm, 2 * kd), lambda i, h: (i, h)), key(), key()],
        out_specs=[out(), out(), out(), out()],
        out_shape=[jax.ShapeDtypeStruct((PEER_HEADS, N_KEYS, t), F32),
                   jax.ShapeDtypeStruct((PEER_HEADS, N_KEYS, t), F32),
                   jax.ShapeDtypeStruct((PEER_HEADS, N_KEYS, t), BF16),
                   jax.ShapeDtypeStruct((PEER_HEADS, N_KEYS, t), BF16)],
        compiler_params=_cparams(("parallel", "parallel")),
        name="peer_route",
    )(q, k1, k2)


def _peer_kernel(x_ref, u_ref, vt_ref, e1_ref, n1_ref, e2_ref, r2_ref, *rest, te):
    out_ref = rest[-1]

    @pl.when(pl.program_id(1) == 0)
    def _():
        out_ref[...] = jnp.zeros_like(out_ref)

    at = lax.dot_general(u_ref[...], x_ref[...], _NT, preferred_element_type=F32)
    act = (0.5 * at * (1.0 + lax.erf(at * (2.0 ** -0.5)))).astype(BF16)
    zero = jnp.zeros((), BF16)
    parts = []
    for c in range(te // N_KEYS):
        w = None
        for h in range(PEER_HEADS):
            n1 = n1_ref[h, c:c + 1, :].astype(BF16)
            e1 = e1_ref[h, c:c + 1, :].astype(BF16)
            wh = jnp.where(r2_ref[h] < n1, e1 * e2_ref[h], zero)
            w = wh if w is None else w + wh
        parts.append(act[c * N_KEYS:(c + 1) * N_KEYS, :] * w)
    ht = jnp.concatenate(parts, axis=0)
    out_ref[...] += jnp.dot(vt_ref[...], ht, preferred_element_type=F32)


def _peer_call(x, u, vt, layer, e1, n1, e2, r2, prev, tm, first, count, te=1024):
    t, d = x.shape
    ne = u.shape[1]
    rows = te // N_KEYS
    once = pl.Buffered(1)
    in_specs = [
        pl.BlockSpec((tm, d), lambda i, j: (first + i, 0), pipeline_mode=once),
        pl.BlockSpec((None, te, d), lambda i, j: (layer, j, 0)),
        pl.BlockSpec((None, d, te), lambda i, j: (layer, 0, j)),
        pl.BlockSpec((PEER_HEADS, rows, tm), lambda i, j: (0, j, first + i)),
        pl.BlockSpec((PEER_HEADS, rows, tm), lambda i, j: (0, j, first + i)),
        pl.BlockSpec((PEER_HEADS, N_KEYS, tm), lambda i, j: (0, 0, first + i), pipeline_mode=once),
        pl.BlockSpec((PEER_HEADS, N_KEYS, tm), lambda i, j: (0, 0, first + i), pipeline_mode=once),
    ]
    args = [x, u, vt, e1, n1, e2, r2]
    aliases = {}
    if prev is not None:
        in_specs.append(pl.BlockSpec(memory_space=pl.ANY))
        args.append(prev)
        aliases = {len(args) - 1: 0}
    return pl.pallas_call(
        functools.partial(_peer_kernel, te=te),
        grid=(count, ne // te),
        in_specs=in_specs,
        out_specs=pl.BlockSpec((d, tm), lambda i, j: (0, first + i)),
        out_shape=jax.ShapeDtypeStruct((d, t), F32),
        input_output_aliases=aliases,
        compiler_params=_cparams(("parallel", "arbitrary")),
        name="peer_experts",
    )(*args)


def _peer(x, u, vt, layer, e1, n1, e2, r2):
    t = x.shape[0]
    n_wide = t // PEER_TM
    out = None
    if n_wide:
        out = _peer_call(x, u, vt, layer, e1, n1, e2, r2, None, PEER_TM, 0, n_wide)
    n_rest = (t - n_wide * PEER_TM) // TM
    if n_rest:
        out = _peer_call(x, u, vt, layer, e1, n1, e2, r2, out, TM, n_wide * (PEER_TM // TM), n_rest)
    return out


def _ple_kernel(h_ref, dt_ref, g_ref, p_ref, wg_ref, wp_ref, out_ref):
    h2 = h_ref[...] + dt_ref[...].T
    xn = _rms(h2, g_ref[...]).astype(BF16)
    gate = jax.nn.sigmoid(jnp.dot(xn, wg_ref[...], preferred_element_type=F32))
    emb = jnp.dot(p_ref[...].astype(BF16), wp_ref[...], preferred_element_type=F32)
    out_ref[...] = h2 + emb * gate


def _ple(h, delta_t, g, p, wg, wp, layer):
    t, d = h.shape
    pd = p.shape[2]
    tm = TM
    once = pl.Buffered(1)
    return pl.pallas_call(
        _ple_kernel,
        grid=(t // tm,),
        in_specs=[
            pl.BlockSpec((tm, d), lambda i: (i, 0)),
            pl.BlockSpec((d, tm), lambda i: (0, i)),
            pl.BlockSpec((1, d), lambda i: (0, 0)),
            pl.BlockSpec((None, tm, pd), lambda i: (layer, i, 0)),
            pl.BlockSpec((None, d, d), lambda i: (layer, 0, 0), pipeline_mode=once),
            pl.BlockSpec((None, pd, d), lambda i: (layer, 0, 0), pipeline_mode=once),
        ],
        out_specs=pl.BlockSpec((tm, d), lambda i: (i, 0)),
        out_shape=jax.ShapeDtypeStruct((t, d), F32),
        compiler_params=_cparams(("parallel",)),
        name="ple_gate",
    )(h, delta_t, g.reshape(1, d), p, wg, wp)


def _lambda_init(layer_idx):
    return 0.8 - 0.6 * math.exp(-0.3 * layer_idx)


def _pad_rows(x, rows):
    return jnp.pad(x, ((0, rows - x.shape[0]),) + ((0, 0),) * (x.ndim - 1))


def kernel(x_prompt, x_sample, cache_k, cache_v, state_ssm, state_conv, p_prompt, p_sample, g_mix, w_in,
           conv_w, conv_b, dt_bias, a_log, d_skip, g_ssd, q_gain, k_gain, lam_q1, lam_k1, lam_q2, lam_k2,
           g_sub, w_out, g_ffn, peer_wq, peer_k1, peer_k2, peer_u, peer_v, g_ple, w_ple, w_pgate):
    nbp, seq, d = x_prompt.shape
    nbs, dseq, _ = x_sample.shape
    depth = w_in.shape[0]
    past = cache_k.shape[2]
    n_p = nbp * seq
    n_s = nbs * dseq
    t_pad = -(-(n_p + n_s) // TM) * TM

    xp = x_prompt.reshape(n_p, d)
    h = jnp.concatenate([xp, x_sample.reshape(n_s, d), xp[:t_pad - n_p - n_s]], axis=0)
    ple_dim = p_prompt.shape[-1]
    p_all = jnp.concatenate([p_prompt.reshape(depth, n_p, ple_dim), p_sample.reshape(depth, n_s, ple_dim)], axis=1)
    p_all = jnp.pad(p_all, ((0, 0), (0, t_pad - n_p - n_s), (0, 0)))
    ck = jnp.transpose(cache_k, (0, 1, 3, 4, 5, 2))
    cv = cache_v.reshape(depth * nbs * past * H_A, DH_V)
    expand = jnp.tile((jnp.arange(LANE)[:, None] == jnp.arange(D_SSM)[None, :] // SSM_P).astype(BF16), (3, 1))
    prev_zero = jnp.zeros((nbp, 8, CONV_DIM), F32)
    h0_zero = jnp.zeros((nbp, D_SSM, SSM_N), F32)

    w_main = w_in.astype(BF16)
    w_dt = jnp.pad(w_in[:, :, MAIN_DIM:], ((0, 0), (0, 0), (0, LANE - H_S))).astype(BF16)
    w_out_b = w_out.astype(BF16)
    wq_b = peer_wq.astype(BF16)
    k1_b = peer_k1.astype(BF16)
    k2_b = peer_k2.astype(BF16)
    u_b = peer_u.astype(BF16)
    vt_b = jnp.swapaxes(peer_v, 1, 2).astype(BF16)
    wg_b = w_pgate.astype(BF16)
    wp_b = w_ple.astype(BF16)

    vp_all = jnp.zeros((depth, n_p, D_ATT), F32)
    vs_all = jnp.zeros((depth, n_s, D_ATT), F32)

    outs = [[] for _ in range(6)]
    for l in range(depth):
        lam0 = _lambda_init(l)
        lam = (jnp.exp(jnp.sum(lam_q1[l] * lam_k1[l])) - jnp.exp(jnp.sum(lam_q2[l] * lam_k2[l])) + lam0)
        lam_s = jnp.stack([lam, jnp.asarray(1.0 - lam0, F32)]).astype(F32)
        lw = dict(
            conv_w=conv_w[l], conv_b=conv_b[l].reshape(1, CONV_DIM),
            dt_bias=_pad_rows(dt_bias[l], LANE).reshape(1, LANE),
            a_neg=_pad_rows(-jnp.exp(a_log[l]), LANE).reshape(1, LANE),
            d_exp=jnp.repeat(d_skip[l], SSM_P).reshape(1, D_SSM),
            g_ssd=g_ssd[l].reshape(1, D_SSM), expand=expand)

        proj, dtp = _norm_proj(h, g_mix[l], w_main, l, MAIN_DIM, wdt=w_dt)
        qn, kb, vb, kf, vp_all, vs_all = _qk_norm(proj, q_gain[l], k_gain[l], l, vp_all, vs_all)
        o_p = _attn_prompt(lam_s, qn, kb, vb, g_sub[l], nbp, seq)
        o_s = _attn_sample(lam_s, qn, kb, vb, ck, cv, l, g_sub[l], n_p, nbs, dseq)
        y_p, hT_p = _ssd(proj, dtp, prev_zero, h0_zero, lw, nbp, seq, SSD_Q)
        proj_s = proj[n_p:n_p + n_s].reshape(nbs, dseq, MAIN_DIM)
        pad_seq = lambda a: jnp.pad(a, ((0, 0), (0, SSD_Q - dseq), (0, 0))).reshape(nbs * SSD_Q, a.shape[-1])
        prev_s = jnp.pad(state_conv[l], ((0, 0), (8 - (CONV_W - 1), 0), (0, 0)))
        y_s, hT_s = _ssd(pad_seq(proj_s), pad_seq(dtp[n_p:n_p + n_s].reshape(nbs, dseq, LANE)), prev_s,
                         state_ssm[l].reshape(nbs, D_SSM, SSM_N), lw, nbs, SSD_Q, dseq)
        y_s = y_s.reshape(nbs, SSD_Q, D_SSM)[:, :dseq].reshape(n_s, D_SSM)
        y_all = _pad_rows(jnp.concatenate([y_p, y_s], axis=0), t_pad)
        o_all = _pad_rows(jnp.concatenate([o_p, o_s], axis=0), t_pad)
        h = _out_proj(y_all, o_all, w_out_b, l, h)

        q_peer, c = _norm_proj(h, g_ffn[l], wq_b, l, d, with_xn=True)
        e1, n1, e2, r2 = _route(q_peer, k1_b, k2_b, l)
        delta_t = _peer(c, u_b, vt_b, l, e1, n1, e2, r2)

        h = _ple(h, delta_t, g_ple[l], p_all, wg_b, wp_b, l)

        xbc0 = 2 * D_ATT + D_ATT + D_SSM
        tail = CONV_W - 1
        outs[0].append(hT_p.reshape(nbp, H_S, SSM_P, SSM_N))
        outs[1].append(jnp.stack([proj[(b + 1) * seq - tail:(b + 1) * seq, xbc0:] for b in range(nbp)]))
        outs[2].append(hT_s.reshape(nbs, H_S, SSM_P, SSM_N))
        outs[3].append(proj_s[:, dseq - tail:, xbc0:])
        outs[4].append(kf[:n_p].reshape(nbp, seq, H_A, 2, DH_QK))
        outs[5].append(kf[n_p:n_p + n_s].reshape(nbs, dseq, H_A, 2, DH_QK))

    y_prompt = h[:n_p].reshape(nbp, seq, d)
    y_sample = h[n_p:n_p + n_s].reshape(nbs, dseq, d)
    ssm_p, conv_p, ssm_s, conv_s, k_p, k_s = (jnp.stack(o) for o in outs)
    return (y_prompt, y_sample,
            k_p, vp_all.reshape(depth, nbp, seq, H_A, DH_V), ssm_p, conv_p,
            k_s, vs_all.reshape(depth, nbs, dseq, H_A, DH_V), ssm_s, conv_s)
```

```python
import functools
import math

import jax
import jax.numpy as jnp
from jax import lax
from jax.experimental import pallas as pl
from jax.experimental.pallas import tpu as pltpu

F32 = jnp.float32
BF16 = jnp.bfloat16

EPS = 1e-6
CHUNK = 64
CHUNK_SHIFT = 6
H_A = 8
DH_QK = 64
DH_V = 128
D_ATT = H_A * DH_V
ATT_SCALE = DH_QK ** -0.5
SSM_P = 64
H_S = 16
SSM_N = 128
SSM_G = 2
D_SSM = H_S * SSM_P
CONV_W = 4
CONV_DIM = D_SSM + 2 * SSM_G * SSM_N
MAIN_DIM = 2 * D_ATT + D_ATT + D_SSM + CONV_DIM
N_KEYS = 128
PEER_HEADS = 8
PEER_TOPK = 16
TOPK_SHIFT = 4

LANE = 128
TM = 512
PEER_TM = 1024
MM_ROWS = 1152
SSD_Q = 128
VMEM_LIMIT = 56 * 1024 * 1024

_NT = (((1,), (1,)), ((), ()))


def _cparams(sem):
    return pltpu.CompilerParams(dimension_semantics=sem, vmem_limit_bytes=VMEM_LIMIT)


def _row_tile(t):
    for k in range(1, t // 16 + 1):
        if t % k == 0 and (t // k) % 16 == 0 and t // k <= MM_ROWS:
            return t // k
    return TM


def _rms(x, g):
    ms = jnp.mean(x * x, axis=-1, keepdims=True)
    return x * lax.rsqrt(ms + EPS) * g


def _norm_proj_kernel(x_ref, g_ref, w_ref, *rest, with_dt, with_xn):
    rest = list(rest)
    wdt_ref = rest.pop(0) if with_dt else None
    o_ref = rest.pop(0)
    odt_ref = rest.pop(0) if with_dt else None
    oxn_ref = rest.pop(0) if with_xn else None
    xn_ref = rest.pop(0)

    @pl.when(pl.program_id(1) == 0)
    def _():
        xn = _rms(x_ref[...], g_ref[...]).astype(BF16)
        xn_ref[...] = xn
        if with_xn:
            oxn_ref[...] = xn
        if with_dt:
            odt_ref[...] = jnp.dot(xn, wdt_ref[...], preferred_element_type=F32)

    o_ref[...] = jnp.dot(xn_ref[...], w_ref[...], preferred_element_type=F32)


def _norm_proj(x, g, w, layer, n, wdt=None, with_xn=False, tn=512):
    t, d = x.shape
    tm = _row_tile(t)
    with_dt = wdt is not None
    in_specs = [
        pl.BlockSpec((tm, d), lambda i, j: (i, 0)),
        pl.BlockSpec((1, d), lambda i, j: (0, 0)),
        pl.BlockSpec((None, d, tn), lambda i, j: (layer, 0, j)),
    ]
    args = [x, g.reshape(1, d), w]
    out_shape = [jax.ShapeDtypeStruct((t, n), F32)]
    out_specs = [pl.BlockSpec((tm, tn), lambda i, j: (i, j))]
    if with_dt:
        in_specs.append(pl.BlockSpec((None, d, LANE), lambda i, j: (layer, 0, 0)))
        args.append(wdt)
        out_shape.append(jax.ShapeDtypeStruct((t, LANE), F32))
        out_specs.append(pl.BlockSpec((tm, LANE), lambda i, j: (i, 0)))
    if with_xn:
        out_shape.append(jax.ShapeDtypeStruct((t, d), BF16))
        out_specs.append(pl.BlockSpec((tm, d), lambda i, j: (i, 0)))
    return pl.pallas_call(
        functools.partial(_norm_proj_kernel, with_dt=with_dt, with_xn=with_xn),
        grid=(t // tm, n // tn),
        in_specs=in_specs,
        out_specs=out_specs,
        out_shape=out_shape,
        scratch_shapes=[pltpu.VMEM((tm, d), BF16)],
        compiler_params=_cparams(("parallel", "arbitrary")),
        name="norm_proj",
    )(*args)


def _qk_norm_kernel(q_ref, k_ref, v_ref, qg_ref, kg_ref, vp_in, vs_in,
                    qn_ref, kb_ref, vb_ref, kf_ref, vp_ref, vs_ref, *, np_tiles, n_s):
    del vp_in, vs_in
    i = pl.program_id(0)
    lane = lax.broadcasted_iota(jnp.int32, (1, DH_V), 1)
    lo = lane < DH_QK

    def norm(x, g):
        xx = x * x
        s_lo = jnp.sum(jnp.where(lo, xx, 0.0), axis=-1, keepdims=True)
        s_all = jnp.sum(xx, axis=-1, keepdims=True)
        ms = jnp.where(lo, s_lo, s_all - s_lo) * (1.0 / DH_QK)
        return x * lax.rsqrt(ms + EPS) * g

    for h in range(H_A):
        sl = slice(h * DH_V, (h + 1) * DH_V)
        qn_ref[:, sl] = (norm(q_ref[:, sl], qg_ref[...]) * ATT_SCALE).astype(BF16)
        kn = norm(k_ref[:, sl], kg_ref[...])
        kf_ref[:, sl] = kn
        kb_ref[:, sl] = kn.astype(BF16)
    vb_ref[...] = v_ref[...].astype(BF16)

    @pl.when(i < np_tiles)
    def _():
        vp_ref[...] = v_ref[...]

    @pl.when(i == np_tiles)
    def _():
        vs_ref[...] = v_ref[0:n_s, :]


def _qk_norm(proj, q_gain, k_gain, layer, vp, vs):
    t = proj.shape[0]
    n_p, n_s = vp.shape[1], vs.shape[1]
    assert n_p % TM == 0 and n_s <= TM and t >= n_p + TM
    np_tiles = n_p // TM
    blk = lambda c: pl.BlockSpec((TM, D_ATT), lambda i, c=c: (i, c))
    gspec = pl.BlockSpec((1, DH_V), lambda i: (0, 0))
    hbm = pl.BlockSpec(memory_space=pl.ANY)
    p_out = pl.BlockSpec((None, TM, D_ATT), lambda i: (layer, jnp.minimum(i, np_tiles - 1), 0))
    s_out = pl.BlockSpec((None, n_s, D_ATT), lambda i: (layer, 0, 0))
    return pl.pallas_call(
        functools.partial(_qk_norm_kernel, np_tiles=np_tiles, n_s=n_s),
        grid=(t // TM,),
        in_specs=[blk(0), blk(1), blk(2), gspec, gspec, hbm, hbm],
        out_specs=[blk(0), blk(0), blk(0), blk(0), p_out, s_out],
        out_shape=[
            jax.ShapeDtypeStruct((t, D_ATT), BF16),
            jax.ShapeDtypeStruct((t, D_ATT), BF16),
            jax.ShapeDtypeStruct((t, D_ATT), BF16),
            jax.ShapeDtypeStruct((t, D_ATT), F32),
            jax.ShapeDtypeStruct(vp.shape, F32),
            jax.ShapeDtypeStruct(vs.shape, F32),
        ],
        input_output_aliases={5: 4, 6: 5},
        compiler_params=_cparams(("arbitrary",)),
        name="qk_norm",
    )(proj, proj, proj, q_gain.reshape(1, DH_V), k_gain.reshape(1, DH_V), vp, vs)


def _sub_norm(o, lam_ref, gsub):
    return _rms(o, gsub) * lam_ref[1]


def _attn_prompt_kernel(lam_ref, q_ref, k_ref, v_ref, gsub_ref, o_ref, *, tq, tk):
    qi = pl.program_id(2)
    q = q_ref[...]
    lane = lax.broadcasted_iota(jnp.int32, (1, DH_V), 1)
    zero = jnp.zeros((), BF16)
    q0 = jnp.where(lane < DH_QK, q, zero)
    q1 = jnp.where(lane >= DH_QK, q, zero)

    def block(start, mask):
        kb = k_ref[pl.ds(start, tk), :]
        s0 = lax.dot_general(q0, kb, _NT, preferred_element_type=F32)
        s1 = lax.dot_general(q1, kb, _NT, preferred_element_type=F32)
        if mask is not None:
            s0 = jnp.where(mask, s0, -jnp.inf)
            s1 = jnp.where(mask, s1, -jnp.inf)
        return s0, s1, v_ref[pl.ds(start, tk), :]

    def upd(s, vb, m, l, a):
        mn = jnp.maximum(m, jnp.max(s, axis=-1, keepdims=True))
        alpha = jnp.exp(m - mn)
        p = jnp.exp(s - mn)
        l = alpha * l + jnp.sum(p, axis=-1, keepdims=True)
        a = alpha * a + jnp.dot(p.astype(BF16), vb, preferred_element_type=F32)
        return mn, l, a

    row = jnp.right_shift(lax.broadcasted_iota(jnp.int32, (tq, tk), 0), CHUNK_SHIFT)
    col = jnp.right_shift(lax.broadcasted_iota(jnp.int32, (tq, tk), 1), CHUNK_SHIFT)
    base = pl.multiple_of(qi * tq, tq)
    s0, s1, vb = block(base, col <= row)
    m0 = jnp.max(s0, axis=-1, keepdims=True)
    m1 = jnp.max(s1, axis=-1, keepdims=True)
    p0 = jnp.exp(s0 - m0)
    p1 = jnp.exp(s1 - m1)
    l0 = jnp.sum(p0, axis=-1, keepdims=True)
    l1 = jnp.sum(p1, axis=-1, keepdims=True)
    a0 = jnp.dot(p0.astype(BF16), vb, preferred_element_type=F32)
    a1 = jnp.dot(p1.astype(BF16), vb, preferred_element_type=F32)
    for dblk in range(1, tq // tk):
        s0, s1, vb = block(base + dblk * tk, col + dblk * (tk // CHUNK) <= row)
        m0, l0, a0 = upd(s0, vb, m0, l0, a0)
        m1, l1, a1 = upd(s1, vb, m1, l1, a1)

    def body(j, carry):
        m0, l0, a0, m1, l1, a1 = carry
        s0, s1, vb = block(pl.multiple_of(j * tk, tk), None)
        m0, l0, a0 = upd(s0, vb, m0, l0, a0)
        m1, l1, a1 = upd(s1, vb, m1, l1, a1)
        return m0, l0, a0, m1, l1, a1

    m0, l0, a0, m1, l1, a1 = lax.fori_loop(0, qi * (tq // tk), body, (m0, l0, a0, m1, l1, a1))
    o = a0 / l0 - lam_ref[0] * (a1 / l1)
    o_ref[...] = _sub_norm(o, lam_ref, gsub_ref[...]).astype(BF16)


def _attn_prompt(lam, qn, kb, vb, g_sub, nb, seq, tq=512, tk=512):
    tq = min(tq, seq)
    tk = min(tk, tq)
    nq = seq // tq
    return pl.pallas_call(
        functools.partial(_attn_prompt_kernel, tq=tq, tk=tk),
        grid=(nb, H_A, nq),
        in_specs=[
            pl.BlockSpec(memory_space=pltpu.SMEM),
            pl.BlockSpec((tq, DH_V), lambda b, h, i: (b * nq + i, h)),
            pl.BlockSpec((seq, DH_V), lambda b, h, i: (b, h)),
            pl.BlockSpec((seq, DH_V), lambda b, h, i: (b, h)),
            pl.BlockSpec((1, DH_V), lambda b, h, i: (0, 0)),
        ],
        out_specs=pl.BlockSpec((tq, DH_V), lambda b, h, i: (b * nq + i, h)),
        out_shape=jax.ShapeDtypeStruct((nb * seq, D_ATT), BF16),
        compiler_params=_cparams(("parallel", "parallel", "arbitrary")),
        name="attn_prompt",
    )(lam, qn, kb, vb, g_sub.reshape(1, DH_V))


def _attn_sample_kernel(lam_ref, q_ref, kn_ref, vn_ref, ckt_ref, cv_ref, gsub_ref, o_ref, m_ref, l_ref, acc_ref, *, tk):
    c = pl.program_id(1)
    nq = q_ref.shape[0]
    lane = lax.broadcasted_iota(jnp.int32, (1, DH_V), 1)
    zero = jnp.zeros((), BF16)

    @pl.when(c == 0)
    def _():
        m_ref[...] = jnp.full(m_ref.shape, -jnp.inf, F32)
        l_ref[...] = jnp.zeros(l_ref.shape, F32)
        acc_ref[...] = jnp.zeros(acc_ref.shape, F32)

    def q01(h):
        q = q_ref[:, h * DH_V:(h + 1) * DH_V]
        return jnp.concatenate([jnp.where(lane < DH_QK, q, zero), jnp.where(lane >= DH_QK, q, zero)], axis=0)

    def update(h, s, v):
        m_old = m_ref[h]
        m_new = jnp.maximum(m_old, jnp.max(s, axis=-1, keepdims=True))
        alpha = jnp.exp(m_old - m_new)
        p = jnp.exp(s - m_new[:, 0:1])
        l_ref[h] = alpha * l_ref[h] + jnp.sum(p, axis=-1, keepdims=True)
        acc_ref[h] = alpha * acc_ref[h] + jnp.dot(p.astype(BF16), v, preferred_element_type=F32)
        m_ref[h] = m_new

    for h in range(H_A):
        kt = ckt_ref[h].reshape(2 * DH_QK, tk).astype(BF16)
        s = jnp.dot(q01(h), kt, preferred_element_type=F32)
        update(h, s, cv_ref[pl.ds(h, tk, stride=H_A), :].astype(BF16))

    @pl.when(c == pl.num_programs(1) - 1)
    def _():
        for h in range(H_A):
            sl = slice(h * DH_V, (h + 1) * DH_V)
            s = lax.dot_general(q01(h), kn_ref[:, sl], _NT, preferred_element_type=F32)
            update(h, s, vn_ref[:, sl])
            a = acc_ref[h] / l_ref[h]
            o = a[0:nq] - lam_ref[0] * a[nq:2 * nq]
            o_ref[:, sl] = _sub_norm(o, lam_ref, gsub_ref[...]).astype(BF16)


def _attn_sample(lam, qn, kb, vb, ckt, cv2, layer, g_sub, row0, nb, nq, tk=1024):
    past = ckt.shape[-1]
    tk = min(tk, past)
    nc = past // tk
    r0 = row0 // nq
    new = lambda: pl.BlockSpec((nq, D_ATT), lambda b, c: (r0 + b, 0))
    return pl.pallas_call(
        functools.partial(_attn_sample_kernel, tk=tk),
        grid=(nb, nc),
        in_specs=[pl.BlockSpec(memory_space=pltpu.SMEM), new(), new(), new(),
                  pl.BlockSpec((None, None, H_A, 2, DH_QK, tk), lambda b, c: (layer, b, 0, 0, 0, c)),
                  pl.BlockSpec((tk * H_A, DH_V), lambda b, c: ((layer * nb + b) * nc + c, 0)),
                  pl.BlockSpec((1, DH_V), lambda b, c: (0, 0))],
        out_specs=pl.BlockSpec((nq, D_ATT), lambda b, c: (b, 0)),
        out_shape=jax.ShapeDtypeStruct((nb * nq, D_ATT), BF16),
        scratch_shapes=[pltpu.VMEM((H_A, 2 * nq, DH_V), F32), pltpu.VMEM((H_A, 2 * nq, DH_V), F32),
                        pltpu.VMEM((H_A, 2 * nq, DH_V), F32)],
        compiler_params=_cparams(("parallel", "arbitrary")),
        name="attn_sample",
    )(lam, qn, kb, vb, ckt, cv2, g_sub.reshape(1, DH_V))


def _split3(x, axis):
    hi = x.astype(BF16)
    r = x - hi.astype(F32)
    mid = r.astype(BF16)
    lo = (r - mid.astype(F32)).astype(BF16)
    return jnp.concatenate([hi, mid, lo], axis=axis)


def _ssd_kernel(xs_ref, b_ref, c_ref, z_ref, dt_ref, prev_ref, h0_ref, cw_ref, cb_ref, dtb_ref,
                aneg_ref, dexp_ref, gssd_ref, expand_ref, y_ref, hout_ref, ext_ref, ht_ref, *, valid):
    q = SSD_Q
    c = pl.program_id(1)
    halo = 8

    @pl.when(c == 0)
    def _():
        ext_ref[0:halo, :] = prev_ref[...]
        ht_ref[...] = h0_ref[...].T

    ext_ref[halo:halo + q, 0:D_SSM] = xs_ref[...]
    ext_ref[halo:halo + q, D_SSM:D_SSM + SSM_G * SSM_N] = b_ref[...]
    ext_ref[halo:halo + q, D_SSM + SSM_G * SSM_N:CONV_DIM] = c_ref[...]
    conv = cb_ref[...]
    for j in range(CONV_W):
        r0 = halo - (CONV_W - 1) + j
        conv = conv + ext_ref[r0:r0 + q, :] * cw_ref[j:j + 1, :]
    tail = ext_ref[q:q + halo, :]
    ext_ref[0:halo, :] = tail
    xbc = conv * jax.nn.sigmoid(conv)
    xs = xbc[:, 0:D_SSM]
    bm = xbc[:, D_SSM:D_SSM + SSM_G * SSM_N]
    cm = xbc[:, D_SSM + SSM_G * SSM_N:CONV_DIM]

    dt = jax.nn.softplus(dt_ref[...] + dtb_ref[...])
    if valid < q:
        rowv = lax.broadcasted_iota(jnp.int32, (q, LANE), 0) < valid
        dt = jnp.where(rowv, dt, 0.0)
    a = dt * aneg_ref[...]
    r_i = lax.broadcasted_iota(jnp.int32, (q, q), 0)
    c_i = lax.broadcasted_iota(jnp.int32, (q, q), 1)
    tril = c_i <= r_i
    tril3 = jnp.tile(tril.astype(BF16), (1, 3))
    acum = jnp.dot(tril3, _split3(a, 0), preferred_element_type=F32)
    eye3 = jnp.tile((lax.broadcasted_iota(jnp.int32, (LANE, LANE), 0)
                     == lax.broadcasted_iota(jnp.int32, (LANE, LANE), 1)).astype(BF16), (1, 3))
    acum_t = lax.dot_general(eye3, _split3(acum, 1), _NT, preferred_element_type=F32)
    a_last = acum[q - 1:q, :]
    decay_in = jnp.exp(a_last - acum)
    stack = jnp.concatenate(
        [dt, dt * decay_in, jnp.exp(acum), jnp.broadcast_to(jnp.exp(a_last), (8, LANE))], axis=0)
    ex = jnp.dot(_split3(stack, 1), expand_ref[...], preferred_element_type=F32)
    dt_e = ex[0:q]
    dtd_e = ex[q:2 * q]
    eacum_e = ex[2 * q:3 * q]
    cd_e = ex[3 * q:3 * q + 1]

    xdt = (xs * dt_e).astype(BF16)
    xdtd = (xs * dtd_e).astype(BF16)
    lane = lax.broadcasted_iota(jnp.int32, (1, LANE), 1)
    zero = jnp.zeros((), BF16)
    gw = D_SSM // SSM_G
    hpg = H_S // SSM_G
    y_diag_parts = []
    y_off_parts = []
    for g in range(SSM_G):
        gs = slice(g * gw, (g + 1) * gw)
        bg = bm[:, g * SSM_N:(g + 1) * SSM_N]
        cg = cm[:, g * SSM_N:(g + 1) * SSM_N].astype(BF16)
        cb = lax.dot_general(cg, bg.astype(BF16), _NT, preferred_element_type=F32)
        ht_prev = ht_ref[:, gs]
        y_off = jnp.dot(cg, ht_prev.astype(BF16), preferred_element_type=F32)
        ht_ref[:, gs] = cd_e[:, gs] * ht_prev + jnp.dot(
            bg.T.astype(BF16), xdtd[:, gs], preferred_element_type=F32)
        for k in range(hpg // 2):
            ms = []
            for h in (g * hpg + 2 * k, g * hpg + 2 * k + 1):
                seg = acum[:, h:h + 1] - acum_t[h:h + 1, :]
                ms.append((cb * jnp.exp(jnp.where(tril, seg, -jnp.inf))).astype(BF16))
            pair = slice((g * hpg + 2 * k) * SSM_P, (g * hpg + 2 * k + 2) * SSM_P)
            xp = xdt[:, pair]
            rhs = jnp.concatenate([jnp.where(lane < SSM_P, xp, zero), jnp.where(lane >= SSM_P, xp, zero)], axis=0)
            y_diag_parts.append(jnp.dot(jnp.concatenate(ms, axis=1), rhs, preferred_element_type=F32))
        y_off_parts.append(y_off)
    y_diag = jnp.concatenate(y_diag_parts, axis=1)
    y_off = jnp.concatenate(y_off_parts, axis=1)
    y = y_diag + y_off * eacum_e + dexp_ref[...] * xs
    zz = z_ref[...]
    y = y * (zz * jax.nn.sigmoid(zz))
    for g in range(SSM_G):
        gs = slice(g * gw, (g + 1) * gw)
        y_ref[:, gs] = _rms(y[:, gs], gssd_ref[:, gs]).astype(BF16)

    @pl.when(c == pl.num_programs(1) - 1)
    def _():
        hout_ref[...] = ht_ref[...].T


def _ssd(proj, dtp, prev, h0, lw, nb, seq, valid):
    nc = seq // SSD_Q
    row = lambda w, col: pl.BlockSpec((SSD_Q, w), lambda b, c, col=col: (b * nc + c, col))
    const = lambda shp: pl.BlockSpec(shp, lambda b, c: (0,) * len(shp))
    per_b = lambda shp: pl.BlockSpec((None,) + shp, lambda b, c: (b, 0, 0))
    return pl.pallas_call(
        functools.partial(_ssd_kernel, valid=valid),
        grid=(nb, nc),
        in_specs=[
            row(D_SSM, 4), row(SSM_G * SSM_N, 20), row(SSM_G * SSM_N, 21), row(D_SSM, 3), row(LANE, 0),
            per_b((8, CONV_DIM)), per_b((D_SSM, SSM_N)),
            const((CONV_W, CONV_DIM)), const((1, CONV_DIM)), const((1, LANE)), const((1, LANE)),
            const((1, D_SSM)), const((1, D_SSM)), const((3 * LANE, D_SSM)),
        ],
        out_specs=[row(D_SSM, 0), per_b((D_SSM, SSM_N))],
        out_shape=[jax.ShapeDtypeStruct((nb * seq, D_SSM), BF16),
                   jax.ShapeDtypeStruct((nb, D_SSM, SSM_N), F32)],
        scratch_shapes=[pltpu.VMEM((SSD_Q + 8, CONV_DIM), F32), pltpu.VMEM((SSM_N, D_SSM), F32)],
        compiler_params=_cparams(("parallel", "arbitrary")),
        name="ssd",
    )(proj, proj, proj, proj, dtp, prev, h0, lw["conv_w"], lw["conv_b"], lw["dt_bias"], lw["a_neg"],
      lw["d_exp"], lw["g_ssd"], lw["expand"])


def _out_proj_kernel(y_ref, o_ref, w_ref, h_ref, out_ref):
    acc = jnp.dot(y_ref[...], w_ref[0:D_SSM, :], preferred_element_type=F32)
    acc = acc + jnp.dot(o_ref[...], w_ref[D_SSM:D_SSM + D_ATT, :], preferred_element_type=F32)
    out_ref[...] = h_ref[...] + acc


def _out_proj(y, o, w, layer, h, tn=512):
    t, d = h.shape
    tm = _row_tile(t)
    return pl.pallas_call(
        _out_proj_kernel,
        grid=(t // tm, d // tn),
        in_specs=[
            pl.BlockSpec((tm, D_SSM), lambda i, j: (i, 0)),
            pl.BlockSpec((tm, D_ATT), lambda i, j: (i, 0)),
            pl.BlockSpec((None, D_SSM + D_ATT, tn), lambda i, j: (layer, 0, j)),
            pl.BlockSpec((tm, tn), lambda i, j: (i, j)),
        ],
        out_specs=pl.BlockSpec((tm, tn), lambda i, j: (i, j)),
        out_shape=jax.ShapeDtypeStruct((t, d), F32),
        compiler_params=_cparams(("parallel", "parallel")),
        name="out_proj",
    )(y, o, w, h)


def _top16(s, n_out=None):
    n, t = s.shape
    iota = lax.broadcasted_iota(jnp.int32, (n, t), 0)
    row16 = lax.broadcasted_iota(jnp.int32, (PEER_TOPK, t), 0)
    rank = jnp.full((n, t), PEER_TOPK, jnp.int32)
    vals = jnp.zeros((PEER_TOPK, t), F32)
    cnt = jnp.zeros((n_out, t), jnp.int32) if n_out else None
    for r in range(PEER_TOPK):
        m = jnp.max(s, axis=0, keepdims=True)
        idx = jnp.min(jnp.where(s == m, iota, n), axis=0, keepdims=True)
        sel = iota == idx
        rank = jnp.where(sel, r, rank)
        s = jnp.where(sel, -jnp.inf, s)
        vals = jnp.where(row16 == r, m, vals)
        if n_out:
            cnt = cnt + (row16 == jnp.right_shift(idx, TOPK_SHIFT)).astype(jnp.int32)
    return vals, rank, cnt


_MARK = 2.0 ** 126


def _peel16(s):
    t = s.shape[1]
    row16 = lax.broadcasted_iota(jnp.int32, (PEER_TOPK, t), 0)
    vals = jnp.zeros((PEER_TOPK, t), F32)
    for r in range(PEER_TOPK):
        m = jnp.max(s, axis=0, keepdims=True)
        s = jnp.where(s == m, -(1.0 + (r + 1) / 32.0) * _MARK, s)
        vals = jnp.where(row16 == r, m, vals)
    marked = s < -_MARK
    rank = jnp.where(marked, s * (-32.0 / _MARK) - 33.0, float(PEER_TOPK))
    count = jnp.sum(jnp.where(marked, 1.0, 0.0), axis=0, keepdims=True)
    return vals, rank, count


def _route_outputs(s1, s2, v1, v2, rank1, rank2, cnt, zsum):
    n1 = jnp.zeros(s1.shape, F32)
    for j in range(PEER_TOPK):
        n1 = jnp.where(rank1 == j, cnt[j:j + 1, :], n1)
    e1 = jnp.where(rank1 < PEER_TOPK, jnp.exp(s1 - v1[0:1, :]) / zsum, 0.0)
    e2 = jnp.where(rank2 < PEER_TOPK, jnp.exp(s2 - v2[0:1, :]), 0.0)
    return e1, n1, e2.astype(BF16), rank2.astype(BF16)


def _route_exact(s1, s2):
    v1, rank1, _ = _top16(s1)
    v2, rank2, _ = _top16(s2)
    cand = jnp.concatenate([v1[j:j + 1, :] + v2 for j in range(PEER_TOPK)], axis=0)
    sc, _, cnt = _top16(cand, n_out=PEER_TOPK)
    zsum = jnp.sum(jnp.exp(sc - sc[0:1, :]), axis=0, keepdims=True)
    return _route_outputs(s1, s2, v1, v2, rank1.astype(F32), rank2.astype(F32), cnt.astype(F32), zsum)


def _route_fast(s1, s2):
    half = PEER_TOPK // 2
    t = s1.shape[1]
    v1, rank1, c1 = _peel16(s1)
    v2, rank2, c2 = _peel16(s2)
    pieces = [v1[j:j + 1, :] + v2[0:half, :] for j in range(half)]
    pieces.append(v1[half:, :] + v2[0:1, :])
    pieces.append(v1[0:1, :] + v2[half:, :])
    cand = jnp.concatenate(pieces, axis=0)
    sc, rankc, cc = _peel16(cand)
    sel = rankc < PEER_TOPK
    zsum = jnp.sum(jnp.where(sel, jnp.exp(cand - sc[0:1, :]), 0.0), axis=0, keepdims=True)
    mk = jnp.where(sel, 1.0, 0.0)
    row8 = lax.broadcasted_iota(jnp.int32, (half, t), 0)
    n_lo = jnp.zeros((half, t), F32)
    for j in range(half):
        rs = jnp.sum(mk[j * half:(j + 1) * half, :], axis=0, keepdims=True)
        if j == 0:
            rs = rs + jnp.sum(mk[(half + 1) * half:, :], axis=0, keepdims=True)
        n_lo = jnp.where(row8 == j, rs, n_lo)
    cnt = jnp.concatenate([n_lo, mk[half * half:(half + 1) * half, :]], axis=0)
    bad = jnp.abs(c1 - PEER_TOPK) + jnp.abs(c2 - PEER_TOPK) + jnp.abs(cc - PEER_TOPK)
    return _route_outputs(s1, s2, v1, v2, rank1, rank2, cnt, zsum), bad


def _route_kernel(q_ref, k1_ref, k2_ref, e1_ref, n1_ref, e2_ref, r2_ref, *, tm):
    def store(ls, outs):
        e1_ref[:, ls], n1_ref[:, ls], e2_ref[:, ls], r2_ref[:, ls] = outs

    def scores(ls):
        qb = q_ref[ls, :].astype(BF16)
        s1 = lax.dot_general(k1_ref[...], qb[:, 0:N_KEYS], _NT, preferred_element_type=F32)
        s2 = lax.dot_general(k2_ref[...], qb[:, N_KEYS:2 * N_KEYS], _NT, preferred_element_type=F32)
        return s1, s2

    group = 4
    for g0 in range(0, tm // LANE, group):
        tiles = [slice(lt * LANE, (lt + 1) * LANE) for lt in range(g0, min(g0 + group, tm // LANE))]
        flags = []
        for ls in tiles:
            outs, bad = _route_fast(*scores(ls))
            store(ls, outs)
            flags.append(jnp.max(bad) > 0.0)
        for ls, flag in zip(tiles, flags):
            @pl.when(flag)
            def _(ls=ls):
                store(ls, _route_exact(*scores(ls)))


def _route(q, k1, k2, layer, tm=TM):
    t = q.shape[0]
    kd = k1.shape[-1]
    out = lambda: pl.BlockSpec((None, N_KEYS, tm), lambda i, h: (h, 0, i))
    key = lambda: pl.BlockSpec((None, None, N_KEYS, kd), lambda i, h: (layer, h, 0, 0))
    return pl.pallas_call(
        functools.partial(_route_kernel, tm=tm),
        grid=(t // tm, PEER_HEADS),
        in_specs=[pl.BlockSpec((tm, 2 * kd), lambda i, h: (i, h)), key(), key()],
        out_specs=[out(), out(), out(), out()],
        out_shape=[jax.ShapeDtypeStruct((PEER_HEADS, N_KEYS, t), F32),
                   jax.ShapeDtypeStruct((PEER_HEADS, N_KEYS, t), F32),
                   jax.ShapeDtypeStruct((PEER_HEADS, N_KEYS, t), BF16),
                   jax.ShapeDtypeStruct((PEER_HEADS, N_KEYS, t), BF16)],
        compiler_params=_cparams(("parallel", "parallel")),
        name="peer_route",
    )(q, k1, k2)


def _peer_kernel(x_ref, u_ref, vt_ref, e1_ref, n1_ref, e2_ref, r2_ref, out_ref, *, te):
    @pl.when(pl.program_id(1) == 0)
    def _():
        out_ref[...] = jnp.zeros_like(out_ref)

    at = lax.dot_general(u_ref[...], x_ref[...], _NT, preferred_element_type=F32)
    act = (0.5 * at * (1.0 + lax.erf(at * (2.0 ** -0.5)))).astype(BF16)
    zero = jnp.zeros((), BF16)
    parts = []
    for c in range(te // N_KEYS):
        w = None
        for h in range(PEER_HEADS):
            n1 = n1_ref[h, c:c + 1, :].astype(BF16)
            e1 = e1_ref[h, c:c + 1, :].astype(BF16)
            wh = jnp.where(r2_ref[h] < n1, e1 * e2_ref[h], zero)
            w = wh if w is None else w + wh
        parts.append(act[c * N_KEYS:(c + 1) * N_KEYS, :] * w)
    ht = jnp.concatenate(parts, axis=0)
    out_ref[...] += jnp.dot(vt_ref[...], ht, preferred_element_type=F32)


def _peer_call(x, u, vt, layer, e1, n1, e2, r2, tm, first, count, te=1024):
    t, d = x.shape
    ne = u.shape[1]
    rows = te // N_KEYS
    once = pl.Buffered(1)
    in_specs = [
        pl.BlockSpec((tm, d), lambda i, j: (first + i, 0), pipeline_mode=once),
        pl.BlockSpec((None, te, d), lambda i, j: (layer, j, 0)),
        pl.BlockSpec((None, d, te), lambda i, j: (layer, 0, j)),
        pl.BlockSpec((PEER_HEADS, rows, tm), lambda i, j: (0, j, first + i)),
        pl.BlockSpec((PEER_HEADS, rows, tm), lambda i, j: (0, j, first + i)),
        pl.BlockSpec((PEER_HEADS, N_KEYS, tm), lambda i, j: (0, 0, first + i), pipeline_mode=once),
        pl.BlockSpec((PEER_HEADS, N_KEYS, tm), lambda i, j: (0, 0, first + i), pipeline_mode=once),
    ]
    return pl.pallas_call(
        functools.partial(_peer_kernel, te=te),
        grid=(count, ne // te),
        in_specs=in_specs,
        out_specs=pl.BlockSpec((d, tm), lambda i, j: (0, i)),
        out_shape=jax.ShapeDtypeStruct((d, count * tm), F32),
        compiler_params=_cparams(("parallel", "arbitrary")),
        name="peer_experts",
    )(x, u, vt, e1, n1, e2, r2)


def _peer(x, u, vt, layer, e1, n1, e2, r2):
    t = x.shape[0]
    n_wide = t // PEER_TM
    per = PEER_TM // TM
    parts = []
    if n_wide:
        parts.append((_peer_call(x, u, vt, layer, e1, n1, e2, r2, PEER_TM, 0, n_wide), 0, n_wide * per))
    n_rest = t // TM - n_wide * per
    if n_rest:
        parts.append((_peer_call(x, u, vt, layer, e1, n1, e2, r2, TM, n_wide * per, n_rest), n_wide * per, n_rest))
    return parts


def _ple_kernel(h_ref, g_ref, p_ref, wg_ref, wp_ref, *rest, firsts):
    out_ref = rest[-1]
    i = pl.program_id(0)
    dt = rest[0][...]
    for part_ref, first in zip(rest[1:-1], firsts[1:]):
        dt = jnp.where(i >= first, part_ref[...], dt)
    h2 = h_ref[...] + dt.T
    xn = _rms(h2, g_ref[...]).astype(BF16)
    gate = jax.nn.sigmoid(jnp.dot(xn, wg_ref[...], preferred_element_type=F32))
    emb = jnp.dot(p_ref[...].astype(BF16), wp_ref[...], preferred_element_type=F32)
    out_ref[...] = h2 + emb * gate


def _ple(h, delta_parts, g, p, wg, wp, layer):
    t, d = h.shape
    pd = p.shape[2]
    tm = TM
    once = pl.Buffered(1)
    part_spec = lambda first, n: pl.BlockSpec((d, tm), lambda i: (0, jnp.clip(i - first, 0, n - 1)))
    return pl.pallas_call(
        functools.partial(_ple_kernel, firsts=tuple(first for _, first, _ in delta_parts)),
        grid=(t // tm,),
        in_specs=[
            pl.BlockSpec((tm, d), lambda i: (i, 0)),
            pl.BlockSpec((1, d), lambda i: (0, 0)),
            pl.BlockSpec((None, tm, pd), lambda i: (layer, i, 0)),
            pl.BlockSpec((None, d, d), lambda i: (layer, 0, 0), pipeline_mode=once),
            pl.BlockSpec((None, pd, d), lambda i: (layer, 0, 0), pipeline_mode=once),
        ] + [part_spec(first, n) for _, first, n in delta_parts],
        out_specs=pl.BlockSpec((tm, d), lambda i: (i, 0)),
        out_shape=jax.ShapeDtypeStruct((t, d), F32),
        compiler_params=_cparams(("parallel",)),
        name="ple_gate",
    )(h, g.reshape(1, d), p, wg, wp, *[a for a, _, _ in delta_parts])


def _lambda_init(layer_idx):
    return 0.8 - 0.6 * math.exp(-0.3 * layer_idx)


def _pad_rows(x, rows):
    return jnp.pad(x, ((0, rows - x.shape[0]),) + ((0, 0),) * (x.ndim - 1))


def kernel(x_prompt, x_sample, cache_k, cache_v, state_ssm, state_conv, p_prompt, p_sample, g_mix, w_in,
           conv_w, conv_b, dt_bias, a_log, d_skip, g_ssd, q_gain, k_gain, lam_q1, lam_k1, lam_q2, lam_k2,
           g_sub, w_out, g_ffn, peer_wq, peer_k1, peer_k2, peer_u, peer_v, g_ple, w_ple, w_pgate):
    nbp, seq, d = x_prompt.shape
    nbs, dseq, _ = x_sample.shape
    depth = w_in.shape[0]
    past = cache_k.shape[2]
    n_p = nbp * seq
    n_s = nbs * dseq
    t_pad = -(-(n_p + n_s) // TM) * TM

    xp = x_prompt.reshape(n_p, d)
    h = jnp.concatenate([xp, x_sample.reshape(n_s, d), xp[:t_pad - n_p - n_s]], axis=0)
    ple_dim = p_prompt.shape[-1]
    p_all = jnp.concatenate([p_prompt.reshape(depth, n_p, ple_dim), p_sample.reshape(depth, n_s, ple_dim)], axis=1)
    p_all = jnp.pad(p_all, ((0, 0), (0, t_pad - n_p - n_s), (0, 0)))
    ck = jnp.transpose(cache_k, (0, 1, 3, 4, 5, 2))
    cv = cache_v.reshape(depth * nbs * past * H_A, DH_V)
    expand = jnp.tile((jnp.arange(LANE)[:, None] == jnp.arange(D_SSM)[None, :] // SSM_P).astype(BF16), (3, 1))
    prev_zero = jnp.zeros((nbp, 8, CONV_DIM), F32)
    h0_zero = jnp.zeros((nbp, D_SSM, SSM_N), F32)

    w_main = w_in.astype(BF16)
    w_dt = jnp.pad(w_in[:, :, MAIN_DIM:], ((0, 0), (0, 0), (0, LANE - H_S))).astype(BF16)
    w_out_b = w_out.astype(BF16)
    wq_b = peer_wq.astype(BF16)
    k1_b = peer_k1.astype(BF16)
    k2_b = peer_k2.astype(BF16)
    u_b = peer_u.astype(BF16)
    vt_b = jnp.swapaxes(peer_v, 1, 2).astype(BF16)
    wg_b = w_pgate.astype(BF16)
    wp_b = w_ple.astype(BF16)

    vp_all = jnp.zeros((depth, n_p, D_ATT), F32)
    vs_all = jnp.zeros((depth, n_s, D_ATT), F32)

    outs = [[] for _ in range(6)]
    for l in range(depth):
        lam0 = _lambda_init(l)
        lam = (jnp.exp(jnp.sum(lam_q1[l] * lam_k1[l])) - jnp.exp(jnp.sum(lam_q2[l] * lam_k2[l])) + lam0)
        lam_s = jnp.stack([lam, jnp.asarray(1.0 - lam0, F32)]).astype(F32)
        lw = dict(
            conv_w=conv_w[l], conv_b=conv_b[l].reshape(1, CONV_DIM),
            dt_bias=_pad_rows(dt_bias[l], LANE).reshape(1, LANE),
            a_neg=_pad_rows(-jnp.exp(a_log[l]), LANE).reshape(1, LANE),
            d_exp=jnp.repeat(d_skip[l], SSM_P).reshape(1, D_SSM),
            g_ssd=g_ssd[l].reshape(1, D_SSM), expand=expand)

        proj, dtp = _norm_proj(h, g_mix[l], w_main, l, MAIN_DIM, wdt=w_dt)
        qn, kb, vb, kf, vp_all, vs_all = _qk_norm(proj, q_gain[l], k_gain[l], l, vp_all, vs_all)
        o_p = _attn_prompt(lam_s, qn, kb, vb, g_sub[l], nbp, seq)
        o_s = _attn_sample(lam_s, qn, kb, vb, ck, cv, l, g_sub[l], n_p, nbs, dseq)
        y_p, hT_p = _ssd(proj, dtp, prev_zero, h0_zero, lw, nbp, seq, SSD_Q)
        proj_s = proj[n_p:n_p + n_s].reshape(nbs, dseq, MAIN_DIM)
        pad_seq = lambda a: jnp.pad(a, ((0, 0), (0, SSD_Q - dseq), (0, 0))).reshape(nbs * SSD_Q, a.shape[-1])
        prev_s = jnp.pad(state_conv[l], ((0, 0), (8 - (CONV_W - 1), 0), (0, 0)))
        y_s, hT_s = _ssd(pad_seq(proj_s), pad_seq(dtp[n_p:n_p + n_s].reshape(nbs, dseq, LANE)), prev_s,
                         state_ssm[l].reshape(nbs, D_SSM, SSM_N), lw, nbs, SSD_Q, dseq)
        y_s = y_s.reshape(nbs, SSD_Q, D_SSM)[:, :dseq].reshape(n_s, D_SSM)
        y_all = _pad_rows(jnp.concatenate([y_p, y_s], axis=0), t_pad)
        o_all = _pad_rows(jnp.concatenate([o_p, o_s], axis=0), t_pad)
        h = _out_proj(y_all, o_all, w_out_b, l, h)

        q_peer, c = _norm_proj(h, g_ffn[l], wq_b, l, d, with_xn=True)
        e1, n1, e2, r2 = _route(q_peer, k1_b, k2_b, l)
        delta_t = _peer(c, u_b, vt_b, l, e1, n1, e2, r2)

        h = _ple(h, delta_t, g_ple[l], p_all, wg_b, wp_b, l)

        xbc0 = 2 * D_ATT + D_ATT + D_SSM
        tail = CONV_W - 1
        outs[0].append(hT_p.reshape(nbp, H_S, SSM_P, SSM_N))
        outs[1].append(jnp.stack([proj[(b + 1) * seq - tail:(b + 1) * seq, xbc0:] for b in range(nbp)]))
        outs[2].append(hT_s.reshape(nbs, H_S, SSM_P, SSM_N))
        outs[3].append(proj_s[:, dseq - tail:, xbc0:])
        outs[4].append(kf[:n_p].reshape(nbp, seq, H_A, 2, DH_QK))
        outs[5].append(kf[n_p:n_p + n_s].reshape(nbs, dseq, H_A, 2, DH_QK))

    y_prompt = h[:n_p].reshape(nbp, seq, d)
    y_sample = h[n_p:n_p + n_s].reshape(nbs, dseq, d)
    ssm_p, conv_p, ssm_s, conv_s, k_p, k_s = (jnp.stack(o) for o in outs)
    return (y_prompt, y_sample,
            k_p, vp_all.reshape(depth, nbp, seq, H_A, DH_V), ssm_p, conv_p,
            k_s, vs_all.reshape(depth, nbs, dseq, H_A, DH_V), ssm_s, conv_s)
```

```python
import functools
import math

import jax
import jax.numpy as jnp
from jax import lax
from jax.experimental import pallas as pl
from jax.experimental.pallas import tpu as pltpu

F32 = jnp.float32
BF16 = jnp.bfloat16

EPS = 1e-6
CHUNK = 64
CHUNK_SHIFT = 6
H_A = 8
DH_QK = 64
DH_V = 128
D_ATT = H_A * DH_V
ATT_SCALE = DH_QK ** -0.5
SSM_P = 64
H_S = 16
SSM_N = 128
SSM_G = 2
D_SSM = H_S * SSM_P
CONV_W = 4
CONV_DIM = D_SSM + 2 * SSM_G * SSM_N
MAIN_DIM = 2 * D_ATT + D_ATT + D_SSM + CONV_DIM
N_KEYS = 128
PEER_HEADS = 8
PEER_TOPK = 16
TOPK_SHIFT = 4

LANE = 128
TM = 512
PEER_TM = 1024
MM_ROWS = 1152
RES_ROWS = 576
SSD_Q = 128
VMEM_LIMIT = 56 * 1024 * 1024

_NT = (((1,), (1,)), ((), ()))


def _cparams(sem):
    return pltpu.CompilerParams(dimension_semantics=sem, vmem_limit_bytes=VMEM_LIMIT)


def _row_tile(t, cap=None):
    cap = MM_ROWS if cap is None else cap
    for k in range(1, t // 16 + 1):
        if t % k == 0 and (t // k) % 16 == 0 and t // k <= cap:
            return t // k
    return TM


def _rms(x, g):
    ms = jnp.mean(x * x, axis=-1, keepdims=True)
    return x * lax.rsqrt(ms + EPS) * g


def _in_proj_kernel(x_ref, g_ref, w_ref, wdt_ref, o_ref, odt_ref, xn_ref):
    @pl.when(pl.program_id(1) == 0)
    def _():
        xn = _rms(x_ref[...], g_ref[...]).astype(BF16)
        xn_ref[...] = xn
        odt_ref[...] = jnp.dot(xn, wdt_ref[...], preferred_element_type=F32)

    o_ref[...] = jnp.dot(xn_ref[...], w_ref[...], preferred_element_type=F32)


def _in_proj(x, g, w, wdt, layer, n, tn=512):
    t, d = x.shape
    tm = _row_tile(t)
    return pl.pallas_call(
        _in_proj_kernel,
        grid=(t // tm, n // tn),
        in_specs=[
            pl.BlockSpec((tm, d), lambda i, j: (i, 0)),
            pl.BlockSpec((1, d), lambda i, j: (0, 0)),
            pl.BlockSpec((None, d, tn), lambda i, j: (layer, 0, j)),
            pl.BlockSpec((None, d, LANE), lambda i, j: (layer, 0, 0)),
        ],
        out_specs=[pl.BlockSpec((tm, tn), lambda i, j: (i, j)), pl.BlockSpec((tm, LANE), lambda i, j: (i, 0))],
        out_shape=[jax.ShapeDtypeStruct((t, n), F32), jax.ShapeDtypeStruct((t, LANE), F32)],
        scratch_shapes=[pltpu.VMEM((tm, d), BF16)],
        compiler_params=_cparams(("parallel", "arbitrary")),
        name="in_proj",
    )(x, g.reshape(1, d), w, wdt)


def _qk_norm_kernel(q_ref, k_ref, v_ref, qg_ref, kg_ref, vp_in, vs_in,
                    qn_ref, kb_ref, vb_ref, kf_ref, vp_ref, vs_ref, *, np_tiles, n_s):
    del vp_in, vs_in
    i = pl.program_id(0)
    lane = lax.broadcasted_iota(jnp.int32, (1, DH_V), 1)
    lo = lane < DH_QK

    def norm(x, g):
        xx = x * x
        s_lo = jnp.sum(jnp.where(lo, xx, 0.0), axis=-1, keepdims=True)
        s_all = jnp.sum(xx, axis=-1, keepdims=True)
        ms = jnp.where(lo, s_lo, s_all - s_lo) * (1.0 / DH_QK)
        return x * lax.rsqrt(ms + EPS) * g

    for h in range(H_A):
        sl = slice(h * DH_V, (h + 1) * DH_V)
        qn_ref[:, sl] = (norm(q_ref[:, sl], qg_ref[...]) * ATT_SCALE).astype(BF16)
        kn = norm(k_ref[:, sl], kg_ref[...])
        kf_ref[:, sl] = kn
        kb_ref[:, sl] = kn.astype(BF16)
    vb_ref[...] = v_ref[...].astype(BF16)

    @pl.when(i < np_tiles)
    def _():
        vp_ref[...] = v_ref[...]

    @pl.when(i == np_tiles)
    def _():
        vs_ref[...] = v_ref[0:n_s, :]


def _qk_norm(proj, q_gain, k_gain, layer, vp, vs):
    t = proj.shape[0]
    n_p, n_s = vp.shape[1], vs.shape[1]
    assert n_p % TM == 0 and n_s <= TM and t >= n_p + TM
    np_tiles = n_p // TM
    blk = lambda c: pl.BlockSpec((TM, D_ATT), lambda i, c=c: (i, c))
    gspec = pl.BlockSpec((1, DH_V), lambda i: (0, 0))
    hbm = pl.BlockSpec(memory_space=pl.ANY)
    p_out = pl.BlockSpec((None, TM, D_ATT), lambda i: (layer, jnp.minimum(i, np_tiles - 1), 0))
    s_out = pl.BlockSpec((None, n_s, D_ATT), lambda i: (layer, 0, 0))
    return pl.pallas_call(
        functools.partial(_qk_norm_kernel, np_tiles=np_tiles, n_s=n_s),
        grid=(t // TM,),
        in_specs=[blk(0), blk(1), blk(2), gspec, gspec, hbm, hbm],
        out_specs=[blk(0), blk(0), blk(0), blk(0), p_out, s_out],
        out_shape=[
            jax.ShapeDtypeStruct((t, D_ATT), BF16),
            jax.ShapeDtypeStruct((t, D_ATT), BF16),
            jax.ShapeDtypeStruct((t, D_ATT), BF16),
            jax.ShapeDtypeStruct((t, D_ATT), F32),
            jax.ShapeDtypeStruct(vp.shape, F32),
            jax.ShapeDtypeStruct(vs.shape, F32),
        ],
        input_output_aliases={5: 4, 6: 5},
        compiler_params=_cparams(("arbitrary",)),
        name="qk_norm",
    )(proj, proj, proj, q_gain.reshape(1, DH_V), k_gain.reshape(1, DH_V), vp, vs)


def _sub_norm(o, lam_ref, gsub):
    return _rms(o, gsub) * lam_ref[1]


def _attn_prompt_kernel(lam_ref, q_ref, k_ref, v_ref, gsub_ref, o_ref, *, tq, tk):
    qi = pl.program_id(2)
    q = q_ref[...]
    lane = lax.broadcasted_iota(jnp.int32, (1, DH_V), 1)
    zero = jnp.zeros((), BF16)
    q0 = jnp.where(lane < DH_QK, q, zero)
    q1 = jnp.where(lane >= DH_QK, q, zero)

    def block(start, mask):
        kb = k_ref[pl.ds(start, tk), :]
        s0 = lax.dot_general(q0, kb, _NT, preferred_element_type=F32)
        s1 = lax.dot_general(q1, kb, _NT, preferred_element_type=F32)
        if mask is not None:
            s0 = jnp.where(mask, s0, -jnp.inf)
            s1 = jnp.where(mask, s1, -jnp.inf)
        return s0, s1, v_ref[pl.ds(start, tk), :]

    def upd(s, vb, m, l, a):
        mn = jnp.maximum(m, jnp.max(s, axis=-1, keepdims=True))
        alpha = jnp.exp(m - mn)
        p = jnp.exp(s - mn)
        l = alpha * l + jnp.sum(p, axis=-1, keepdims=True)
        a = alpha * a + jnp.dot(p.astype(BF16), vb, preferred_element_type=F32)
        return mn, l, a

    row = jnp.right_shift(lax.broadcasted_iota(jnp.int32, (tq, tk), 0), CHUNK_SHIFT)
    col = jnp.right_shift(lax.broadcasted_iota(jnp.int32, (tq, tk), 1), CHUNK_SHIFT)
    base = pl.multiple_of(qi * tq, tq)
    s0, s1, vb = block(base, col <= row)
    m0 = jnp.max(s0, axis=-1, keepdims=True)
    m1 = jnp.max(s1, axis=-1, keepdims=True)
    p0 = jnp.exp(s0 - m0)
    p1 = jnp.exp(s1 - m1)
    l0 = jnp.sum(p0, axis=-1, keepdims=True)
    l1 = jnp.sum(p1, axis=-1, keepdims=True)
    a0 = jnp.dot(p0.astype(BF16), vb, preferred_element_type=F32)
    a1 = jnp.dot(p1.astype(BF16), vb, preferred_element_type=F32)
    for dblk in range(1, tq // tk):
        s0, s1, vb = block(base + dblk * tk, col + dblk * (tk // CHUNK) <= row)
        m0, l0, a0 = upd(s0, vb, m0, l0, a0)
        m1, l1, a1 = upd(s1, vb, m1, l1, a1)

    def body(j, carry):
        m0, l0, a0, m1, l1, a1 = carry
        s0, s1, vb = block(pl.multiple_of(j * tk, tk), None)
        m0, l0, a0 = upd(s0, vb, m0, l0, a0)
        m1, l1, a1 = upd(s1, vb, m1, l1, a1)
        return m0, l0, a0, m1, l1, a1

    m0, l0, a0, m1, l1, a1 = lax.fori_loop(0, qi * (tq // tk), body, (m0, l0, a0, m1, l1, a1))
    o = a0 / l0 - lam_ref[0] * (a1 / l1)
    o_ref[...] = _sub_norm(o, lam_ref, gsub_ref[...]).astype(BF16)


def _attn_prompt(lam, qn, kb, vb, g_sub, nb, seq, tq=512, tk=512):
    tq = min(tq, seq)
    tk = min(tk, tq)
    nq = seq // tq
    return pl.pallas_call(
        functools.partial(_attn_prompt_kernel, tq=tq, tk=tk),
        grid=(nb, H_A, nq),
        in_specs=[
            pl.BlockSpec(memory_space=pltpu.SMEM),
            pl.BlockSpec((tq, DH_V), lambda b, h, i: (b * nq + i, h)),
            pl.BlockSpec((seq, DH_V), lambda b, h, i: (b, h)),
            pl.BlockSpec((seq, DH_V), lambda b, h, i: (b, h)),
            pl.BlockSpec((1, DH_V), lambda b, h, i: (0, 0)),
        ],
        out_specs=pl.BlockSpec((tq, DH_V), lambda b, h, i: (b * nq + i, h)),
        out_shape=jax.ShapeDtypeStruct((nb * seq, D_ATT), BF16),
        compiler_params=_cparams(("parallel", "parallel", "arbitrary")),
        name="attn_prompt",
    )(lam, qn, kb, vb, g_sub.reshape(1, DH_V))


def _attn_sample_kernel(lam_ref, q_ref, kn_ref, vn_ref, ckt_ref, cv_ref, gsub_ref, o_ref, m_ref, l_ref, acc_ref, *, tk):
    c = pl.program_id(1)
    nq = q_ref.shape[0]
    lane = lax.broadcasted_iota(jnp.int32, (1, DH_V), 1)
    zero = jnp.zeros((), BF16)

    @pl.when(c == 0)
    def _():
        m_ref[...] = jnp.full(m_ref.shape, -jnp.inf, F32)
        l_ref[...] = jnp.zeros(l_ref.shape, F32)
        acc_ref[...] = jnp.zeros(acc_ref.shape, F32)

    def q01(h):
        q = q_ref[:, h * DH_V:(h + 1) * DH_V]
        return jnp.concatenate([jnp.where(lane < DH_QK, q, zero), jnp.where(lane >= DH_QK, q, zero)], axis=0)

    def update(h, s, v):
        m_old = m_ref[h]
        m_new = jnp.maximum(m_old, jnp.max(s, axis=-1, keepdims=True))
        alpha = jnp.exp(m_old - m_new)
        p = jnp.exp(s - m_new[:, 0:1])
        l_ref[h] = alpha * l_ref[h] + jnp.sum(p, axis=-1, keepdims=True)
        acc_ref[h] = alpha * acc_ref[h] + jnp.dot(p.astype(BF16), v, preferred_element_type=F32)
        m_ref[h] = m_new

    for h in range(H_A):
        kt = ckt_ref[h].reshape(2 * DH_QK, tk).astype(BF16)
        s = jnp.dot(q01(h), kt, preferred_element_type=F32)
        update(h, s, cv_ref[pl.ds(h, tk, stride=H_A), :].astype(BF16))

    @pl.when(c == pl.num_programs(1) - 1)
    def _():
        for h in range(H_A):
            sl = slice(h * DH_V, (h + 1) * DH_V)
            s = lax.dot_general(q01(h), kn_ref[:, sl], _NT, preferred_element_type=F32)
            update(h, s, vn_ref[:, sl])
            a = acc_ref[h] / l_ref[h]
            o = a[0:nq] - lam_ref[0] * a[nq:2 * nq]
            o_ref[:, sl] = _sub_norm(o, lam_ref, gsub_ref[...]).astype(BF16)


def _attn_sample(lam, qn, kb, vb, ckt, cv2, layer, g_sub, row0, nb, nq, tk=1024):
    past = ckt.shape[-1]
    tk = min(tk, past)
    nc = past // tk
    r0 = row0 // nq
    new = lambda: pl.BlockSpec((nq, D_ATT), lambda b, c: (r0 + b, 0))
    return pl.pallas_call(
        functools.partial(_attn_sample_kernel, tk=tk),
        grid=(nb, nc),
        in_specs=[pl.BlockSpec(memory_space=pltpu.SMEM), new(), new(), new(),
                  pl.BlockSpec((None, None, H_A, 2, DH_QK, tk), lambda b, c: (layer, b, 0, 0, 0, c)),
                  pl.BlockSpec((tk * H_A, DH_V), lambda b, c: ((layer * nb + b) * nc + c, 0)),
                  pl.BlockSpec((1, DH_V), lambda b, c: (0, 0))],
        out_specs=pl.BlockSpec((nq, D_ATT), lambda b, c: (b, 0)),
        out_shape=jax.ShapeDtypeStruct((nb * nq, D_ATT), BF16),
        scratch_shapes=[pltpu.VMEM((H_A, 2 * nq, DH_V), F32), pltpu.VMEM((H_A, 2 * nq, DH_V), F32),
                        pltpu.VMEM((H_A, 2 * nq, DH_V), F32)],
        compiler_params=_cparams(("parallel", "arbitrary")),
        name="attn_sample",
    )(lam, qn, kb, vb, ckt, cv2, g_sub.reshape(1, DH_V))


def _split3(x, axis):
    hi = x.astype(BF16)
    r = x - hi.astype(F32)
    mid = r.astype(BF16)
    lo = (r - mid.astype(F32)).astype(BF16)
    return jnp.concatenate([hi, mid, lo], axis=axis)


def _ssd_kernel(xs_ref, b_ref, c_ref, z_ref, dt_ref, prev_ref, h0_ref, cw_ref, cb_ref, dtb_ref,
                aneg_ref, dexp_ref, gssd_ref, expand_ref, y_ref, hout_ref, ext_ref, ht_ref, *, valid):
    q = SSD_Q
    c = pl.program_id(1)
    halo = 8

    @pl.when(c == 0)
    def _():
        ext_ref[0:halo, :] = prev_ref[...]
        ht_ref[...] = h0_ref[...].T

    ext_ref[halo:halo + q, 0:D_SSM] = xs_ref[...]
    ext_ref[halo:halo + q, D_SSM:D_SSM + SSM_G * SSM_N] = b_ref[...]
    ext_ref[halo:halo + q, D_SSM + SSM_G * SSM_N:CONV_DIM] = c_ref[...]
    conv = cb_ref[...]
    for j in range(CONV_W):
        r0 = halo - (CONV_W - 1) + j
        conv = conv + ext_ref[r0:r0 + q, :] * cw_ref[j:j + 1, :]
    tail = ext_ref[q:q + halo, :]
    ext_ref[0:halo, :] = tail
    xbc = conv * jax.nn.sigmoid(conv)
    xs = xbc[:, 0:D_SSM]
    bm = xbc[:, D_SSM:D_SSM + SSM_G * SSM_N]
    cm = xbc[:, D_SSM + SSM_G * SSM_N:CONV_DIM]

    dt = jax.nn.softplus(dt_ref[...] + dtb_ref[...])
    if valid < q:
        rowv = lax.broadcasted_iota(jnp.int32, (q, LANE), 0) < valid
        dt = jnp.where(rowv, dt, 0.0)
    a = dt * aneg_ref[...]
    r_i = lax.broadcasted_iota(jnp.int32, (q, q), 0)
    c_i = lax.broadcasted_iota(jnp.int32, (q, q), 1)
    tril = c_i <= r_i
    tril3 = jnp.tile(tril.astype(BF16), (1, 3))
    acum = jnp.dot(tril3, _split3(a, 0), preferred_element_type=F32)
    eye3 = jnp.tile((lax.broadcasted_iota(jnp.int32, (LANE, LANE), 0)
                     == lax.broadcasted_iota(jnp.int32, (LANE, LANE), 1)).astype(BF16), (1, 3))
    acum_t = lax.dot_general(eye3, _split3(acum, 1), _NT, preferred_element_type=F32)
    a_last = acum[q - 1:q, :]
    decay_in = jnp.exp(a_last - acum)
    stack = jnp.concatenate(
        [dt, dt * decay_in, jnp.exp(acum), jnp.broadcast_to(jnp.exp(a_last), (8, LANE))], axis=0)
    ex = jnp.dot(_split3(stack, 1), expand_ref[...], preferred_element_type=F32)
    dt_e = ex[0:q]
    dtd_e = ex[q:2 * q]
    eacum_e = ex[2 * q:3 * q]
    cd_e = ex[3 * q:3 * q + 1]

    xdt = (xs * dt_e).astype(BF16)
    xdtd = (xs * dtd_e).astype(BF16)
    lane = lax.broadcasted_iota(jnp.int32, (1, LANE), 1)
    zero = jnp.zeros((), BF16)
    gw = D_SSM // SSM_G
    hpg = H_S // SSM_G
    y_diag_parts = []
    y_off_parts = []
    for g in range(SSM_G):
        gs = slice(g * gw, (g + 1) * gw)
        bg = bm[:, g * SSM_N:(g + 1) * SSM_N]
        cg = cm[:, g * SSM_N:(g + 1) * SSM_N].astype(BF16)
        cb = lax.dot_general(cg, bg.astype(BF16), _NT, preferred_element_type=F32)
        ht_prev = ht_ref[:, gs]
        y_off = jnp.dot(cg, ht_prev.astype(BF16), preferred_element_type=F32)
        ht_ref[:, gs] = cd_e[:, gs] * ht_prev + jnp.dot(
            bg.T.astype(BF16), xdtd[:, gs], preferred_element_type=F32)
        for k in range(hpg // 2):
            ms = []
            for h in (g * hpg + 2 * k, g * hpg + 2 * k + 1):
                seg = acum[:, h:h + 1] - acum_t[h:h + 1, :]
                ms.append((cb * jnp.exp(jnp.where(tril, seg, -jnp.inf))).astype(BF16))
            pair = slice((g * hpg + 2 * k) * SSM_P, (g * hpg + 2 * k + 2) * SSM_P)
            xp = xdt[:, pair]
            rhs = jnp.concatenate([jnp.where(lane < SSM_P, xp, zero), jnp.where(lane >= SSM_P, xp, zero)], axis=0)
            y_diag_parts.append(jnp.dot(jnp.concatenate(ms, axis=1), rhs, preferred_element_type=F32))
        y_off_parts.append(y_off)
    y_diag = jnp.concatenate(y_diag_parts, axis=1)
    y_off = jnp.concatenate(y_off_parts, axis=1)
    y = y_diag + y_off * eacum_e + dexp_ref[...] * xs
    zz = z_ref[...]
    y = y * (zz * jax.nn.sigmoid(zz))
    for g in range(SSM_G):
        gs = slice(g * gw, (g + 1) * gw)
        y_ref[:, gs] = _rms(y[:, gs], gssd_ref[:, gs]).astype(BF16)

    @pl.when(c == pl.num_programs(1) - 1)
    def _():
        hout_ref[...] = ht_ref[...].T


def _ssd(proj, dtp, prev, h0, lw, nb, seq, valid):
    nc = seq // SSD_Q
    row = lambda w, col: pl.BlockSpec((SSD_Q, w), lambda b, c, col=col: (b * nc + c, col))
    const = lambda shp: pl.BlockSpec(shp, lambda b, c: (0,) * len(shp))
    per_b = lambda shp: pl.BlockSpec((None,) + shp, lambda b, c: (b, 0, 0))
    return pl.pallas_call(
        functools.partial(_ssd_kernel, valid=valid),
        grid=(nb, nc),
        in_specs=[
            row(D_SSM, 4), row(SSM_G * SSM_N, 20), row(SSM_G * SSM_N, 21), row(D_SSM, 3), row(LANE, 0),
            per_b((8, CONV_DIM)), per_b((D_SSM, SSM_N)),
            const((CONV_W, CONV_DIM)), const((1, CONV_DIM)), const((1, LANE)), const((1, LANE)),
            const((1, D_SSM)), const((1, D_SSM)), const((3 * LANE, D_SSM)),
        ],
        out_specs=[row(D_SSM, 0), per_b((D_SSM, SSM_N))],
        out_shape=[jax.ShapeDtypeStruct((nb * seq, D_SSM), BF16),
                   jax.ShapeDtypeStruct((nb, D_SSM, SSM_N), F32)],
        scratch_shapes=[pltpu.VMEM((SSD_Q + 8, CONV_DIM), F32), pltpu.VMEM((SSM_N, D_SSM), F32)],
        compiler_params=_cparams(("parallel", "arbitrary")),
        name="ssd",
    )(proj, proj, proj, proj, dtp, prev, h0, lw["conv_w"], lw["conv_b"], lw["dt_bias"], lw["a_neg"],
      lw["d_exp"], lw["g_ssd"], lw["expand"])


def _out_proj_kernel(y_ref, o_ref, w_ref, h_ref, out_ref):
    acc = jnp.dot(y_ref[...], w_ref[0:D_SSM, :], preferred_element_type=F32)
    acc = acc + jnp.dot(o_ref[...], w_ref[D_SSM:D_SSM + D_ATT, :], preferred_element_type=F32)
    out_ref[...] = h_ref[...] + acc


def _out_proj(y, o, w, layer, h):
    t, d = h.shape
    tm = _row_tile(t, RES_ROWS)
    return pl.pallas_call(
        _out_proj_kernel,
        grid=(t // tm,),
        in_specs=[
            pl.BlockSpec((tm, D_SSM), lambda i: (i, 0)),
            pl.BlockSpec((tm, D_ATT), lambda i: (i, 0)),
            pl.BlockSpec((None, D_SSM + D_ATT, d), lambda i: (layer, 0, 0), pipeline_mode=pl.Buffered(1)),
            pl.BlockSpec((tm, d), lambda i: (i, 0)),
        ],
        out_specs=pl.BlockSpec((tm, d), lambda i: (i, 0)),
        out_shape=jax.ShapeDtypeStruct((t, d), F32),
        compiler_params=_cparams(("parallel",)),
        name="out_proj",
    )(y, o, w, h)


def _norm_proj_resident_kernel(x_ref, g_ref, w_ref, o_ref, oxn_ref):
    xn = _rms(x_ref[...], g_ref[...]).astype(BF16)
    oxn_ref[...] = xn
    o_ref[...] = jnp.dot(xn, w_ref[...], preferred_element_type=F32).astype(o_ref.dtype)


def _norm_proj_resident(x, g, w, layer, out_dtype):
    t, d = x.shape
    n = w.shape[2]
    tm = _row_tile(t, RES_ROWS)
    return pl.pallas_call(
        _norm_proj_resident_kernel,
        grid=(t // tm,),
        in_specs=[
            pl.BlockSpec((tm, d), lambda i: (i, 0)),
            pl.BlockSpec((1, d), lambda i: (0, 0)),
            pl.BlockSpec((None, d, n), lambda i: (layer, 0, 0), pipeline_mode=pl.Buffered(1)),
        ],
        out_specs=[pl.BlockSpec((tm, n), lambda i: (i, 0)), pl.BlockSpec((tm, d), lambda i: (i, 0))],
        out_shape=[jax.ShapeDtypeStruct((t, n), out_dtype), jax.ShapeDtypeStruct((t, d), BF16)],
        compiler_params=_cparams(("parallel",)),
        name="norm_proj_resident",
    )(x, g.reshape(1, d), w)


def _top16(s, n_out=None):
    n, t = s.shape
    iota = lax.broadcasted_iota(jnp.int32, (n, t), 0)
    row16 = lax.broadcasted_iota(jnp.int32, (PEER_TOPK, t), 0)
    rank = jnp.full((n, t), PEER_TOPK, jnp.int32)
    vals = jnp.zeros((PEER_TOPK, t), F32)
    cnt = jnp.zeros((n_out, t), jnp.int32) if n_out else None
    for r in range(PEER_TOPK):
        m = jnp.max(s, axis=0, keepdims=True)
        idx = jnp.min(jnp.where(s == m, iota, n), axis=0, keepdims=True)
        sel = iota == idx
        rank = jnp.where(sel, r, rank)
        s = jnp.where(sel, -jnp.inf, s)
        vals = jnp.where(row16 == r, m, vals)
        if n_out:
            cnt = cnt + (row16 == jnp.right_shift(idx, TOPK_SHIFT)).astype(jnp.int32)
    return vals, rank, cnt


_MARK = 2.0 ** 126


def _peel16(s):
    t = s.shape[1]
    row16 = lax.broadcasted_iota(jnp.int32, (PEER_TOPK, t), 0)
    vals = jnp.zeros((PEER_TOPK, t), F32)
    for r in range(PEER_TOPK):
        m = jnp.max(s, axis=0, keepdims=True)
        s = jnp.where(s == m, -(1.0 + (r + 1) / 32.0) * _MARK, s)
        vals = jnp.where(row16 == r, m, vals)
    marked = s < -_MARK
    rank = jnp.where(marked, s * (-32.0 / _MARK) - 33.0, float(PEER_TOPK))
    count = jnp.sum(jnp.where(marked, 1.0, 0.0), axis=0, keepdims=True)
    return vals, rank, count


def _route_outputs(s1, s2, v1, v2, rank1, rank2, cnt, zsum):
    n1 = jnp.zeros(s1.shape, F32)
    for j in range(PEER_TOPK):
        n1 = jnp.where(rank1 == j, cnt[j:j + 1, :], n1)
    e1 = jnp.where(rank1 < PEER_TOPK, jnp.exp(s1 - v1[0:1, :]) / zsum, 0.0)
    e2 = jnp.where(rank2 < PEER_TOPK, jnp.exp(s2 - v2[0:1, :]), 0.0)
    return e1, n1, e2.astype(BF16), rank2.astype(BF16)


def _route_exact(s1, s2):
    v1, rank1, _ = _top16(s1)
    v2, rank2, _ = _top16(s2)
    cand = jnp.concatenate([v1[j:j + 1, :] + v2 for j in range(PEER_TOPK)], axis=0)
    sc, _, cnt = _top16(cand, n_out=PEER_TOPK)
    zsum = jnp.sum(jnp.exp(sc - sc[0:1, :]), axis=0, keepdims=True)
    return _route_outputs(s1, s2, v1, v2, rank1.astype(F32), rank2.astype(F32), cnt.astype(F32), zsum)


def _route_fast(s1, s2):
    half = PEER_TOPK // 2
    t = s1.shape[1]
    v1, rank1, c1 = _peel16(s1)
    v2, rank2, c2 = _peel16(s2)
    pieces = [v1[j:j + 1, :] + v2[0:half, :] for j in range(half)]
    pieces.append(v1[half:, :] + v2[0:1, :])
    pieces.append(v1[0:1, :] + v2[half:, :])
    cand = jnp.concatenate(pieces, axis=0)
    sc, rankc, cc = _peel16(cand)
    sel = rankc < PEER_TOPK
    zsum = jnp.sum(jnp.where(sel, jnp.exp(cand - sc[0:1, :]), 0.0), axis=0, keepdims=True)
    mk = jnp.where(sel, 1.0, 0.0)
    row8 = lax.broadcasted_iota(jnp.int32, (half, t), 0)
    n_lo = jnp.zeros((half, t), F32)
    for j in range(half):
        rs = jnp.sum(mk[j * half:(j + 1) * half, :], axis=0, keepdims=True)
        if j == 0:
            rs = rs + jnp.sum(mk[(half + 1) * half:, :], axis=0, keepdims=True)
        n_lo = jnp.where(row8 == j, rs, n_lo)
    cnt = jnp.concatenate([n_lo, mk[half * half:(half + 1) * half, :]], axis=0)
    bad = jnp.abs(c1 - PEER_TOPK) + jnp.abs(c2 - PEER_TOPK) + jnp.abs(cc - PEER_TOPK)
    return _route_outputs(s1, s2, v1, v2, rank1, rank2, cnt, zsum), bad


def _route_kernel(q_ref, k1_ref, k2_ref, e1_ref, n1_ref, e2_ref, r2_ref, *, tm):
    def store(ls, outs):
        e1_ref[:, ls], n1_ref[:, ls], e2_ref[:, ls], r2_ref[:, ls] = outs

    def scores(ls):
        qb = q_ref[ls, :].astype(BF16)
        s1 = lax.dot_general(k1_ref[...], qb[:, 0:N_KEYS], _NT, preferred_element_type=F32)
        s2 = lax.dot_general(k2_ref[...], qb[:, N_KEYS:2 * N_KEYS], _NT, preferred_element_type=F32)
        return s1, s2

    group = 4
    for g0 in range(0, tm // LANE, group):
        tiles = [slice(lt * LANE, (lt + 1) * LANE) for lt in range(g0, min(g0 + group, tm // LANE))]
        flags = []
        for ls in tiles:
            outs, bad = _route_fast(*scores(ls))
            store(ls, outs)
            flags.append(jnp.max(bad) > 0.0)
        for ls, flag in zip(tiles, flags):
            @pl.when(flag)
            def _(ls=ls):
                store(ls, _route_exact(*scores(ls)))


def _route(q, k1, k2, layer, tm=TM):
    t = q.shape[0]
    kd = k1.shape[-1]
    out = lambda: pl.BlockSpec((None, N_KEYS, tm), lambda i, h: (h, 0, i))
    key = lambda: pl.BlockSpec((None, None, N_KEYS, kd), lambda i, h: (layer, h, 0, 0))
    return pl.pallas_call(
        functools.partial(_route_kernel, tm=tm),
        grid=(t // tm, PEER_HEADS),
        in_specs=[pl.BlockSpec((tm, 2 * kd), lambda i, h: (i, h)), key(), key()],
        out_specs=[out(), out(), out(), out()],
        out_shape=[jax.ShapeDtypeStruct((PEER_HEADS, N_KEYS, t), F32),
                   jax.ShapeDtypeStruct((PEER_HEADS, N_KEYS, t), F32),
                   jax.ShapeDtypeStruct((PEER_HEADS, N_KEYS, t), BF16),
                   jax.ShapeDtypeStruct((PEER_HEADS, N_KEYS, t), BF16)],
        compiler_params=_cparams(("parallel", "parallel")),
        name="peer_route",
    )(q, k1, k2)


def _peer_kernel(x_ref, u_ref, vt_ref, e1_ref, n1_ref, e2_ref, r2_ref, out_ref, *, te):
    @pl.when(pl.program_id(1) == 0)
    def _():
        out_ref[...] = jnp.zeros_like(out_ref)

    at = lax.dot_general(u_ref[...], x_ref[...], _NT, preferred_element_type=F32)
    act = (0.5 * at * (1.0 + lax.erf(at * (2.0 ** -0.5)))).astype(BF16)
    zero = jnp.zeros((), BF16)
    parts = []
    for c in range(te // N_KEYS):
        w = None
        for h in range(PEER_HEADS):
            n1 = n1_ref[h, c:c + 1, :].astype(BF16)
            e1 = e1_ref[h, c:c + 1, :].astype(BF16)
            wh = jnp.where(r2_ref[h] < n1, e1 * e2_ref[h], zero)
            w = wh if w is None else w + wh
        parts.append(act[c * N_KEYS:(c + 1) * N_KEYS, :] * w)
    ht = jnp.concatenate(parts, axis=0)
    out_ref[...] += jnp.dot(vt_ref[...], ht, preferred_element_type=F32)


def _peer_call(x, u, vt, layer, e1, n1, e2, r2, tm, first, count, te=1024):
    t, d = x.shape
    ne = u.shape[1]
    rows = te // N_KEYS
    once = pl.Buffered(1)
    in_specs = [
        pl.BlockSpec((tm, d), lambda i, j: (first + i, 0), pipeline_mode=once),
        pl.BlockSpec((None, te, d), lambda i, j: (layer, j, 0)),
        pl.BlockSpec((None, d, te), lambda i, j: (layer, 0, j)),
        pl.BlockSpec((PEER_HEADS, rows, tm), lambda i, j: (0, j, first + i)),
        pl.BlockSpec((PEER_HEADS, rows, tm), lambda i, j: (0, j, first + i)),
        pl.BlockSpec((PEER_HEADS, N_KEYS, tm), lambda i, j: (0, 0, first + i), pipeline_mode=once),
        pl.BlockSpec((PEER_HEADS, N_KEYS, tm), lambda i, j: (0, 0, first + i), pipeline_mode=once),
    ]
    return pl.pallas_call(
        functools.partial(_peer_kernel, te=te),
        grid=(count, ne // te),
        in_specs=in_specs,
        out_specs=pl.BlockSpec((d, tm), lambda i, j: (0, i)),
        out_shape=jax.ShapeDtypeStruct((d, count * tm), F32),
        compiler_params=_cparams(("parallel", "arbitrary")),
        name="peer_experts",
    )(x, u, vt, e1, n1, e2, r2)


def _peer(x, u, vt, layer, e1, n1, e2, r2):
    t = x.shape[0]
    n_wide = t // PEER_TM
    per = PEER_TM // TM
    parts = []
    if n_wide:
        parts.append((_peer_call(x, u, vt, layer, e1, n1, e2, r2, PEER_TM, 0, n_wide), 0, n_wide * per))
    n_rest = t // TM - n_wide * per
    if n_rest:
        parts.append((_peer_call(x, u, vt, layer, e1, n1, e2, r2, TM, n_wide * per, n_rest), n_wide * per, n_rest))
    return parts


def _ple_kernel(h_ref, g_ref, p_ref, wg_ref, wp_ref, *rest, firsts):
    out_ref = rest[-1]
    i = pl.program_id(0)
    dt = rest[0][...]
    for part_ref, first in zip(rest[1:-1], firsts[1:]):
        dt = jnp.where(i >= first, part_ref[...], dt)
    h2 = h_ref[...] + dt.T
    xn = _rms(h2, g_ref[...]).astype(BF16)
    gate = jax.nn.sigmoid(jnp.dot(xn, wg_ref[...], preferred_element_type=F32))
    emb = jnp.dot(p_ref[...].astype(BF16), wp_ref[...], preferred_element_type=F32)
    out_ref[...] = h2 + emb * gate


def _ple(h, delta_parts, g, p, wg, wp, layer):
    t, d = h.shape
    pd = p.shape[2]
    tm = TM
    once = pl.Buffered(1)
    part_spec = lambda first, n: pl.BlockSpec((d, tm), lambda i: (0, jnp.clip(i - first, 0, n - 1)))
    return pl.pallas_call(
        functools.partial(_ple_kernel, firsts=tuple(first for _, first, _ in delta_parts)),
        grid=(t // tm,),
        in_specs=[
            pl.BlockSpec((tm, d), lambda i: (i, 0)),
            pl.BlockSpec((1, d), lambda i: (0, 0)),
            pl.BlockSpec((None, tm, pd), lambda i: (layer, i, 0)),
            pl.BlockSpec((None, d, d), lambda i: (layer, 0, 0), pipeline_mode=once),
            pl.BlockSpec((None, pd, d), lambda i: (layer, 0, 0), pipeline_mode=once),
        ] + [part_spec(first, n) for _, first, n in delta_parts],
        out_specs=pl.BlockSpec((tm, d), lambda i: (i, 0)),
        out_shape=jax.ShapeDtypeStruct((t, d), F32),
        compiler_params=_cparams(("parallel",)),
        name="ple_gate",
    )(h, g.reshape(1, d), p, wg, wp, *[a for a, _, _ in delta_parts])


def _lambda_init(layer_idx):
    return 0.8 - 0.6 * math.exp(-0.3 * layer_idx)


def _pad_rows(x, rows):
    return jnp.pad(x, ((0, rows - x.shape[0]),) + ((0, 0),) * (x.ndim - 1))


def kernel(x_prompt, x_sample, cache_k, cache_v, state_ssm, state_conv, p_prompt, p_sample, g_mix, w_in,
           conv_w, conv_b, dt_bias, a_log, d_skip, g_ssd, q_gain, k_gain, lam_q1, lam_k1, lam_q2, lam_k2,
           g_sub, w_out, g_ffn, peer_wq, peer_k1, peer_k2, peer_u, peer_v, g_ple, w_ple, w_pgate):
    nbp, seq, d = x_prompt.shape
    nbs, dseq, _ = x_sample.shape
    depth = w_in.shape[0]
    past = cache_k.shape[2]
    n_p = nbp * seq
    n_s = nbs * dseq
    t_pad = -(-(n_p + n_s) // TM) * TM

    xp = x_prompt.reshape(n_p, d)
    h = jnp.concatenate([xp, x_sample.reshape(n_s, d), xp[:t_pad - n_p - n_s]], axis=0)
    ple_dim = p_prompt.shape[-1]
    p_all = jnp.concatenate([p_prompt.reshape(depth, n_p, ple_dim), p_sample.reshape(depth, n_s, ple_dim)], axis=1)
    p_all = jnp.pad(p_all, ((0, 0), (0, t_pad - n_p - n_s), (0, 0)))
    ck = jnp.transpose(cache_k, (0, 1, 3, 4, 5, 2))
    cv = cache_v.reshape(depth * nbs * past * H_A, DH_V)
    expand = jnp.tile((jnp.arange(LANE)[:, None] == jnp.arange(D_SSM)[None, :] // SSM_P).astype(BF16), (3, 1))
    prev_zero = jnp.zeros((nbp, 8, CONV_DIM), F32)
    h0_zero = jnp.zeros((nbp, D_SSM, SSM_N), F32)

    w_main = w_in.astype(BF16)
    w_dt = jnp.pad(w_in[:, :, MAIN_DIM:], ((0, 0), (0, 0), (0, LANE - H_S))).astype(BF16)
    w_out_b = w_out.astype(BF16)
    wq_b = peer_wq.astype(BF16)
    k1_b = peer_k1.astype(BF16)
    k2_b = peer_k2.astype(BF16)
    u_b = peer_u.astype(BF16)
    vt_b = jnp.swapaxes(peer_v, 1, 2).astype(BF16)
    wg_b = w_pgate.astype(BF16)
    wp_b = w_ple.astype(BF16)

    vp_all = jnp.zeros((depth, n_p, D_ATT), F32)
    vs_all = jnp.zeros((depth, n_s, D_ATT), F32)

    outs = [[] for _ in range(6)]
    for l in range(depth):
        lam0 = _lambda_init(l)
        lam = (jnp.exp(jnp.sum(lam_q1[l] * lam_k1[l])) - jnp.exp(jnp.sum(lam_q2[l] * lam_k2[l])) + lam0)
        lam_s = jnp.stack([lam, jnp.asarray(1.0 - lam0, F32)]).astype(F32)
        lw = dict(
            conv_w=conv_w[l], conv_b=conv_b[l].reshape(1, CONV_DIM),
            dt_bias=_pad_rows(dt_bias[l], LANE).reshape(1, LANE),
            a_neg=_pad_rows(-jnp.exp(a_log[l]), LANE).reshape(1, LANE),
            d_exp=jnp.repeat(d_skip[l], SSM_P).reshape(1, D_SSM),
            g_ssd=g_ssd[l].reshape(1, D_SSM), expand=expand)

        proj, dtp = _in_proj(h, g_mix[l], w_main, w_dt, l, MAIN_DIM)
        qn, kb, vb, kf, vp_all, vs_all = _qk_norm(proj, q_gain[l], k_gain[l], l, vp_all, vs_all)
        o_p = _attn_prompt(lam_s, qn, kb, vb, g_sub[l], nbp, seq)
        o_s = _attn_sample(lam_s, qn, kb, vb, ck, cv, l, g_sub[l], n_p, nbs, dseq)
        y_p, hT_p = _ssd(proj, dtp, prev_zero, h0_zero, lw, nbp, seq, SSD_Q)
        proj_s = proj[n_p:n_p + n_s].reshape(nbs, dseq, MAIN_DIM)
        pad_seq = lambda a: jnp.pad(a, ((0, 0), (0, SSD_Q - dseq), (0, 0))).reshape(nbs * SSD_Q, a.shape[-1])
        prev_s = jnp.pad(state_conv[l], ((0, 0), (8 - (CONV_W - 1), 0), (0, 0)))
        y_s, hT_s = _ssd(pad_seq(proj_s), pad_seq(dtp[n_p:n_p + n_s].reshape(nbs, dseq, LANE)), prev_s,
                         state_ssm[l].reshape(nbs, D_SSM, SSM_N), lw, nbs, SSD_Q, dseq)
        y_s = y_s.reshape(nbs, SSD_Q, D_SSM)[:, :dseq].reshape(n_s, D_SSM)
        y_all = _pad_rows(jnp.concatenate([y_p, y_s], axis=0), t_pad)
        o_all = _pad_rows(jnp.concatenate([o_p, o_s], axis=0), t_pad)
        h = _out_proj(y_all, o_all, w_out_b, l, h)

        q_peer, c = _norm_proj_resident(h, g_ffn[l], wq_b, l, BF16)
        e1, n1, e2, r2 = _route(q_peer, k1_b, k2_b, l)
        delta_t = _peer(c, u_b, vt_b, l, e1, n1, e2, r2)

        h = _ple(h, delta_t, g_ple[l], p_all, wg_b, wp_b, l)

        xbc0 = 2 * D_ATT + D_ATT + D_SSM
        tail = CONV_W - 1
        outs[0].append(hT_p.reshape(nbp, H_S, SSM_P, SSM_N))
        outs[1].append(jnp.stack([proj[(b + 1) * seq - tail:(b + 1) * seq, xbc0:] for b in range(nbp)]))
        outs[2].append(hT_s.reshape(nbs, H_S, SSM_P, SSM_N))
        outs[3].append(proj_s[:, dseq - tail:, xbc0:])
        outs[4].append(kf[:n_p].reshape(nbp, seq, H_A, 2, DH_QK))
        outs[5].append(kf[n_p:n_p + n_s].reshape(nbs, dseq, H_A, 2, DH_QK))

    y_prompt = h[:n_p].reshape(nbp, seq, d)
    y_sample = h[n_p:n_p + n_s].reshape(nbs, dseq, d)
    ssm_p, conv_p, ssm_s, conv_s, k_p, k_s = (jnp.stack(o) for o in outs)
    return (y_prompt, y_sample,
            k_p, vp_all.reshape(depth, nbp, seq, H_A, DH_V), ssm_p, conv_p,
            k_s, vs_all.reshape(depth, nbs, dseq, H_A, DH_V), ssm_s, conv_s)
```

```python
import functools
import math

import jax
import jax.numpy as jnp
from jax import lax
from jax.experimental import pallas as pl
from jax.experimental.pallas import tpu as pltpu

F32 = jnp.float32
BF16 = jnp.bfloat16

EPS = 1e-6
CHUNK = 64
CHUNK_SHIFT = 6
H_A = 8
DH_QK = 64
DH_V = 128
D_ATT = H_A * DH_V
ATT_SCALE = DH_QK ** -0.5
SSM_P = 64
H_S = 16
SSM_N = 128
SSM_G = 2
D_SSM = H_S * SSM_P
CONV_W = 4
CONV_DIM = D_SSM + 2 * SSM_G * SSM_N
MAIN_DIM = 2 * D_ATT + D_ATT + D_SSM + CONV_DIM
N_KEYS = 128
PEER_HEADS = 8
PEER_TOPK = 16
TOPK_SHIFT = 4

LANE = 128
TM = 512
PEER_TM = 1024
MM_ROWS = 1152
RES_ROWS = 576
SSD_Q = 128
VMEM_LIMIT = 56 * 1024 * 1024

_NT = (((1,), (1,)), ((), ()))


def _cparams(sem):
    return pltpu.CompilerParams(dimension_semantics=sem, vmem_limit_bytes=VMEM_LIMIT)


def _row_tile(t, cap=None):
    cap = MM_ROWS if cap is None else cap
    for k in range(1, t // 16 + 1):
        if t % k == 0 and (t // k) % 16 == 0 and t // k <= cap:
            return t // k
    return TM


def _rms(x, g):
    ms = jnp.mean(x * x, axis=-1, keepdims=True)
    return x * lax.rsqrt(ms + EPS) * g


def _in_proj_kernel(x_ref, g_ref, w_ref, wdt_ref, o_ref, odt_ref):
    xn = _rms(x_ref[...], g_ref[...]).astype(BF16)
    o_ref[...] = jnp.dot(xn, w_ref[...], preferred_element_type=F32)

    @pl.when(pl.program_id(0) == 0)
    def _():
        odt_ref[...] = jnp.dot(xn, wdt_ref[...], preferred_element_type=F32)


def _in_proj(x, g, w, wdt, layer, n, halves=2):
    t, d = x.shape
    tn = n // halves
    tm = _row_tile(t, RES_ROWS)
    once = pl.Buffered(1)
    return pl.pallas_call(
        _in_proj_kernel,
        grid=(halves, t // tm),
        in_specs=[
            pl.BlockSpec((tm, d), lambda j, i: (i, 0)),
            pl.BlockSpec((1, d), lambda j, i: (0, 0)),
            pl.BlockSpec((None, d, tn), lambda j, i: (layer, 0, j), pipeline_mode=once),
            pl.BlockSpec((None, d, LANE), lambda j, i: (layer, 0, 0), pipeline_mode=once),
        ],
        out_specs=[pl.BlockSpec((tm, tn), lambda j, i: (i, j)),
                   pl.BlockSpec((tm, LANE), lambda j, i: (jnp.where(j == 0, i, t // tm - 1), 0))],
        out_shape=[jax.ShapeDtypeStruct((t, n), F32), jax.ShapeDtypeStruct((t, LANE), F32)],
        compiler_params=_cparams(("arbitrary", "arbitrary")),
        name="in_proj",
    )(x, g.reshape(1, d), w, wdt)


def _qk_norm_kernel(q_ref, k_ref, v_ref, qg_ref, kg_ref, vp_in, vs_in,
                    qn_ref, kb_ref, vb_ref, kf_ref, vp_ref, vs_ref, *, np_tiles, n_s):
    del vp_in, vs_in
    i = pl.program_id(0)
    lane = lax.broadcasted_iota(jnp.int32, (1, DH_V), 1)
    lo = lane < DH_QK

    def norm(x, g):
        xx = x * x
        s_lo = jnp.sum(jnp.where(lo, xx, 0.0), axis=-1, keepdims=True)
        s_all = jnp.sum(xx, axis=-1, keepdims=True)
        ms = jnp.where(lo, s_lo, s_all - s_lo) * (1.0 / DH_QK)
        return x * lax.rsqrt(ms + EPS) * g

    for h in range(H_A):
        sl = slice(h * DH_V, (h + 1) * DH_V)
        qn_ref[:, sl] = (norm(q_ref[:, sl], qg_ref[...]) * ATT_SCALE).astype(BF16)
        kn = norm(k_ref[:, sl], kg_ref[...])
        kf_ref[:, sl] = kn
        kb_ref[:, sl] = kn.astype(BF16)
    vb_ref[...] = v_ref[...].astype(BF16)

    @pl.when(i < np_tiles)
    def _():
        vp_ref[...] = v_ref[...]

    @pl.when(i == np_tiles)
    def _():
        vs_ref[...] = v_ref[0:n_s, :]


def _qk_norm(proj, q_gain, k_gain, layer, vp, vs):
    t = proj.shape[0]
    n_p, n_s = vp.shape[1], vs.shape[1]
    assert n_p % TM == 0 and n_s <= TM and t >= n_p + TM
    np_tiles = n_p // TM
    blk = lambda c: pl.BlockSpec((TM, D_ATT), lambda i, c=c: (i, c))
    gspec = pl.BlockSpec((1, DH_V), lambda i: (0, 0))
    hbm = pl.BlockSpec(memory_space=pl.ANY)
    p_out = pl.BlockSpec((None, TM, D_ATT), lambda i: (layer, jnp.minimum(i, np_tiles - 1), 0))
    s_out = pl.BlockSpec((None, n_s, D_ATT), lambda i: (layer, 0, 0))
    return pl.pallas_call(
        functools.partial(_qk_norm_kernel, np_tiles=np_tiles, n_s=n_s),
        grid=(t // TM,),
        in_specs=[blk(0), blk(1), blk(2), gspec, gspec, hbm, hbm],
        out_specs=[blk(0), blk(0), blk(0), blk(0), p_out, s_out],
        out_shape=[
            jax.ShapeDtypeStruct((t, D_ATT), BF16),
            jax.ShapeDtypeStruct((t, D_ATT), BF16),
            jax.ShapeDtypeStruct((t, D_ATT), BF16),
            jax.ShapeDtypeStruct((t, D_ATT), F32),
            jax.ShapeDtypeStruct(vp.shape, F32),
            jax.ShapeDtypeStruct(vs.shape, F32),
        ],
        input_output_aliases={5: 4, 6: 5},
        compiler_params=_cparams(("arbitrary",)),
        name="qk_norm",
    )(proj, proj, proj, q_gain.reshape(1, DH_V), k_gain.reshape(1, DH_V), vp, vs)


def _sub_norm(o, lam_ref, gsub):
    return _rms(o, gsub) * lam_ref[1]


def _attn_prompt_kernel(lam_ref, q_ref, k_ref, v_ref, gsub_ref, o_ref, *, tq, tk):
    qi = pl.program_id(2)
    q = q_ref[...]
    lane = lax.broadcasted_iota(jnp.int32, (1, DH_V), 1)
    zero = jnp.zeros((), BF16)
    q0 = jnp.where(lane < DH_QK, q, zero)
    q1 = jnp.where(lane >= DH_QK, q, zero)

    def block(start, mask):
        kb = k_ref[pl.ds(start, tk), :]
        s0 = lax.dot_general(q0, kb, _NT, preferred_element_type=F32)
        s1 = lax.dot_general(q1, kb, _NT, preferred_element_type=F32)
        if mask is not None:
            s0 = jnp.where(mask, s0, -jnp.inf)
            s1 = jnp.where(mask, s1, -jnp.inf)
        return s0, s1, v_ref[pl.ds(start, tk), :]

    def upd(s, vb, m, l, a):
        mn = jnp.maximum(m, jnp.max(s, axis=-1, keepdims=True))
        alpha = jnp.exp(m - mn)
        p = jnp.exp(s - mn)
        l = alpha * l + jnp.sum(p, axis=-1, keepdims=True)
        a = alpha * a + jnp.dot(p.astype(BF16), vb, preferred_element_type=F32)
        return mn, l, a

    row = jnp.right_shift(lax.broadcasted_iota(jnp.int32, (tq, tk), 0), CHUNK_SHIFT)
    col = jnp.right_shift(lax.broadcasted_iota(jnp.int32, (tq, tk), 1), CHUNK_SHIFT)
    base = pl.multiple_of(qi * tq, tq)
    s0, s1, vb = block(base, col <= row)
    m0 = jnp.max(s0, axis=-1, keepdims=True)
    m1 = jnp.max(s1, axis=-1, keepdims=True)
    p0 = jnp.exp(s0 - m0)
    p1 = jnp.exp(s1 - m1)
    l0 = jnp.sum(p0, axis=-1, keepdims=True)
    l1 = jnp.sum(p1, axis=-1, keepdims=True)
    a0 = jnp.dot(p0.astype(BF16), vb, preferred_element_type=F32)
    a1 = jnp.dot(p1.astype(BF16), vb, preferred_element_type=F32)
    for dblk in range(1, tq // tk):
        s0, s1, vb = block(base + dblk * tk, col + dblk * (tk // CHUNK) <= row)
        m0, l0, a0 = upd(s0, vb, m0, l0, a0)
        m1, l1, a1 = upd(s1, vb, m1, l1, a1)

    def body(j, carry):
        m0, l0, a0, m1, l1, a1 = carry
        s0, s1, vb = block(pl.multiple_of(j * tk, tk), None)
        m0, l0, a0 = upd(s0, vb, m0, l0, a0)
        m1, l1, a1 = upd(s1, vb, m1, l1, a1)
        return m0, l0, a0, m1, l1, a1

    m0, l0, a0, m1, l1, a1 = lax.fori_loop(0, qi * (tq // tk), body, (m0, l0, a0, m1, l1, a1))
    o = a0 / l0 - lam_ref[0] * (a1 / l1)
    o_ref[...] = _sub_norm(o, lam_ref, gsub_ref[...]).astype(BF16)


def _attn_prompt(lam, qn, kb, vb, g_sub, nb, seq, tq=512, tk=512):
    tq = min(tq, seq)
    tk = min(tk, tq)
    nq = seq // tq
    return pl.pallas_call(
        functools.partial(_attn_prompt_kernel, tq=tq, tk=tk),
        grid=(nb, H_A, nq),
        in_specs=[
            pl.BlockSpec(memory_space=pltpu.SMEM),
            pl.BlockSpec((tq, DH_V), lambda b, h, i: (b * nq + i, h)),
            pl.BlockSpec((seq, DH_V), lambda b, h, i: (b, h)),
            pl.BlockSpec((seq, DH_V), lambda b, h, i: (b, h)),
            pl.BlockSpec((1, DH_V), lambda b, h, i: (0, 0)),
        ],
        out_specs=pl.BlockSpec((tq, DH_V), lambda b, h, i: (b * nq + i, h)),
        out_shape=jax.ShapeDtypeStruct((nb * seq, D_ATT), BF16),
        compiler_params=_cparams(("parallel", "parallel", "arbitrary")),
        name="attn_prompt",
    )(lam, qn, kb, vb, g_sub.reshape(1, DH_V))


def _attn_sample_kernel(lam_ref, q_ref, kn_ref, vn_ref, ckt_ref, cv_ref, gsub_ref, o_ref, m_ref, l_ref, acc_ref, *, tk):
    c = pl.program_id(1)
    nq = q_ref.shape[0]
    lane = lax.broadcasted_iota(jnp.int32, (1, DH_V), 1)
    zero = jnp.zeros((), BF16)

    @pl.when(c == 0)
    def _():
        m_ref[...] = jnp.full(m_ref.shape, -jnp.inf, F32)
        l_ref[...] = jnp.zeros(l_ref.shape, F32)
        acc_ref[...] = jnp.zeros(acc_ref.shape, F32)

    def q01(h):
        q = q_ref[:, h * DH_V:(h + 1) * DH_V]
        return jnp.concatenate([jnp.where(lane < DH_QK, q, zero), jnp.where(lane >= DH_QK, q, zero)], axis=0)

    def update(h, s, v):
        m_old = m_ref[h]
        m_new = jnp.maximum(m_old, jnp.max(s, axis=-1, keepdims=True))
        alpha = jnp.exp(m_old - m_new)
        p = jnp.exp(s - m_new[:, 0:1])
        l_ref[h] = alpha * l_ref[h] + jnp.sum(p, axis=-1, keepdims=True)
        acc_ref[h] = alpha * acc_ref[h] + jnp.dot(p.astype(BF16), v, preferred_element_type=F32)
        m_ref[h] = m_new

    for h in range(H_A):
        kt = ckt_ref[h].reshape(2 * DH_QK, tk).astype(BF16)
        s = jnp.dot(q01(h), kt, preferred_element_type=F32)
        update(h, s, cv_ref[pl.ds(h, tk, stride=H_A), :].astype(BF16))

    @pl.when(c == pl.num_programs(1) - 1)
    def _():
        for h in range(H_A):
            sl = slice(h * DH_V, (h + 1) * DH_V)
            s = lax.dot_general(q01(h), kn_ref[:, sl], _NT, preferred_element_type=F32)
            update(h, s, vn_ref[:, sl])
            a = acc_ref[h] / l_ref[h]
            o = a[0:nq] - lam_ref[0] * a[nq:2 * nq]
            o_ref[:, sl] = _sub_norm(o, lam_ref, gsub_ref[...]).astype(BF16)


def _attn_sample(lam, qn, kb, vb, ckt, cv2, layer, g_sub, row0, nb, nq, tk=1024):
    past = ckt.shape[-1]
    tk = min(tk, past)
    nc = past // tk
    r0 = row0 // nq
    new = lambda: pl.BlockSpec((nq, D_ATT), lambda b, c: (r0 + b, 0))
    return pl.pallas_call(
        functools.partial(_attn_sample_kernel, tk=tk),
        grid=(nb, nc),
        in_specs=[pl.BlockSpec(memory_space=pltpu.SMEM), new(), new(), new(),
                  pl.BlockSpec((None, None, H_A, 2, DH_QK, tk), lambda b, c: (layer, b, 0, 0, 0, c)),
                  pl.BlockSpec((tk * H_A, DH_V), lambda b, c: ((layer * nb + b) * nc + c, 0)),
                  pl.BlockSpec((1, DH_V), lambda b, c: (0, 0))],
        out_specs=pl.BlockSpec((nq, D_ATT), lambda b, c: (b, 0)),
        out_shape=jax.ShapeDtypeStruct((nb * nq, D_ATT), BF16),
        scratch_shapes=[pltpu.VMEM((H_A, 2 * nq, DH_V), F32), pltpu.VMEM((H_A, 2 * nq, DH_V), F32),
                        pltpu.VMEM((H_A, 2 * nq, DH_V), F32)],
        compiler_params=_cparams(("parallel", "arbitrary")),
        name="attn_sample",
    )(lam, qn, kb, vb, ckt, cv2, g_sub.reshape(1, DH_V))


def _split3(x, axis):
    hi = x.astype(BF16)
    r = x - hi.astype(F32)
    mid = r.astype(BF16)
    lo = (r - mid.astype(F32)).astype(BF16)
    return jnp.concatenate([hi, mid, lo], axis=axis)


def _ssd_kernel(xs_ref, b_ref, c_ref, z_ref, dt_ref, prev_ref, h0_ref, cw_ref, cb_ref, dtb_ref,
                aneg_ref, dexp_ref, gssd_ref, expand_ref, y_ref, hout_ref, ext_ref, ht_ref, *, valid):
    q = SSD_Q
    c = pl.program_id(1)
    halo = 8

    @pl.when(c == 0)
    def _():
        ext_ref[0:halo, :] = prev_ref[...]
        ht_ref[...] = h0_ref[...].T

    ext_ref[halo:halo + q, 0:D_SSM] = xs_ref[...]
    ext_ref[halo:halo + q, D_SSM:D_SSM + SSM_G * SSM_N] = b_ref[...]
    ext_ref[halo:halo + q, D_SSM + SSM_G * SSM_N:CONV_DIM] = c_ref[...]
    conv = cb_ref[...]
    for j in range(CONV_W):
        r0 = halo - (CONV_W - 1) + j
        conv = conv + ext_ref[r0:r0 + q, :] * cw_ref[j:j + 1, :]
    tail = ext_ref[q:q + halo, :]
    ext_ref[0:halo, :] = tail
    xbc = conv * jax.nn.sigmoid(conv)
    xs = xbc[:, 0:D_SSM]
    bm = xbc[:, D_SSM:D_SSM + SSM_G * SSM_N]
    cm = xbc[:, D_SSM + SSM_G * SSM_N:CONV_DIM]

    dt = jax.nn.softplus(dt_ref[...] + dtb_ref[...])
    if valid < q:
        rowv = lax.broadcasted_iota(jnp.int32, (q, LANE), 0) < valid
        dt = jnp.where(rowv, dt, 0.0)
    a = dt * aneg_ref[...]
    r_i = lax.broadcasted_iota(jnp.int32, (q, q), 0)
    c_i = lax.broadcasted_iota(jnp.int32, (q, q), 1)
    tril = c_i <= r_i
    tril3 = jnp.tile(tril.astype(BF16), (1, 3))
    acum = jnp.dot(tril3, _split3(a, 0), preferred_element_type=F32)
    eye3 = jnp.tile((lax.broadcasted_iota(jnp.int32, (LANE, LANE), 0)
                     == lax.broadcasted_iota(jnp.int32, (LANE, LANE), 1)).astype(BF16), (1, 3))
    acum_t = lax.dot_general(eye3, _split3(acum, 1), _NT, preferred_element_type=F32)
    a_last = acum[q - 1:q, :]
    decay_in = jnp.exp(a_last - acum)
    stack = jnp.concatenate(
        [dt, dt * decay_in, jnp.exp(acum), jnp.broadcast_to(jnp.exp(a_last), (8, LANE))], axis=0)
    ex = jnp.dot(_split3(stack, 1), expand_ref[...], preferred_element_type=F32)
    dt_e = ex[0:q]
    dtd_e = ex[q:2 * q]
    eacum_e = ex[2 * q:3 * q]
    cd_e = ex[3 * q:3 * q + 1]

    xdt = (xs * dt_e).astype(BF16)
    xdtd = (xs * dtd_e).astype(BF16)
    lane = lax.broadcasted_iota(jnp.int32, (1, LANE), 1)
    zero = jnp.zeros((), BF16)
    gw = D_SSM // SSM_G
    hpg = H_S // SSM_G
    y_diag_parts = []
    y_off_parts = []
    for g in range(SSM_G):
        gs = slice(g * gw, (g + 1) * gw)
        bg = bm[:, g * SSM_N:(g + 1) * SSM_N]
        cg = cm[:, g * SSM_N:(g + 1) * SSM_N].astype(BF16)
        cb = lax.dot_general(cg, bg.astype(BF16), _NT, preferred_element_type=F32)
        ht_prev = ht_ref[:, gs]
        y_off = jnp.dot(cg, ht_prev.astype(BF16), preferred_element_type=F32)
        ht_ref[:, gs] = cd_e[:, gs] * ht_prev + jnp.dot(
            bg.T.astype(BF16), xdtd[:, gs], preferred_element_type=F32)
        for k in range(hpg // 2):
            ms = []
            for h in (g * hpg + 2 * k, g * hpg + 2 * k + 1):
                seg = acum[:, h:h + 1] - acum_t[h:h + 1, :]
                ms.append((cb * jnp.exp(jnp.where(tril, seg, -jnp.inf))).astype(BF16))
            pair = slice((g * hpg + 2 * k) * SSM_P, (g * hpg + 2 * k + 2) * SSM_P)
            xp = xdt[:, pair]
            rhs = jnp.concatenate([jnp.where(lane < SSM_P, xp, zero), jnp.where(lane >= SSM_P, xp, zero)], axis=0)
            y_diag_parts.append(jnp.dot(jnp.concatenate(ms, axis=1), rhs, preferred_element_type=F32))
        y_off_parts.append(y_off)
    y_diag = jnp.concatenate(y_diag_parts, axis=1)
    y_off = jnp.concatenate(y_off_parts, axis=1)
    y = y_diag + y_off * eacum_e + dexp_ref[...] * xs
    zz = z_ref[...]
    y = y * (zz * jax.nn.sigmoid(zz))
    for g in range(SSM_G):
        gs = slice(g * gw, (g + 1) * gw)
        y_ref[:, gs] = _rms(y[:, gs], gssd_ref[:, gs]).astype(BF16)

    @pl.when(c == pl.num_programs(1) - 1)
    def _():
        hout_ref[...] = ht_ref[...].T


def _ssd(proj, dtp, prev, h0, lw, nb, seq, valid):
    nc = seq // SSD_Q
    row = lambda w, col: pl.BlockSpec((SSD_Q, w), lambda b, c, col=col: (b * nc + c, col))
    const = lambda shp: pl.BlockSpec(shp, lambda b, c: (0,) * len(shp))
    per_b = lambda shp: pl.BlockSpec((None,) + shp, lambda b, c: (b, 0, 0))
    return pl.pallas_call(
        functools.partial(_ssd_kernel, valid=valid),
        grid=(nb, nc),
        in_specs=[
            row(D_SSM, 4), row(SSM_G * SSM_N, 20), row(SSM_G * SSM_N, 21), row(D_SSM, 3), row(LANE, 0),
            per_b((8, CONV_DIM)), per_b((D_SSM, SSM_N)),
            const((CONV_W, CONV_DIM)), const((1, CONV_DIM)), const((1, LANE)), const((1, LANE)),
            const((1, D_SSM)), const((1, D_SSM)), const((3 * LANE, D_SSM)),
        ],
        out_specs=[row(D_SSM, 0), per_b((D_SSM, SSM_N))],
        out_shape=[jax.ShapeDtypeStruct((nb * seq, D_SSM), BF16),
                   jax.ShapeDtypeStruct((nb, D_SSM, SSM_N), F32)],
        scratch_shapes=[pltpu.VMEM((SSD_Q + 8, CONV_DIM), F32), pltpu.VMEM((SSM_N, D_SSM), F32)],
        compiler_params=_cparams(("parallel", "arbitrary")),
        name="ssd",
    )(proj, proj, proj, proj, dtp, prev, h0, lw["conv_w"], lw["conv_b"], lw["dt_bias"], lw["a_neg"],
      lw["d_exp"], lw["g_ssd"], lw["expand"])


def _out_proj_kernel(y_ref, o_ref, w_ref, h_ref, out_ref):
    acc = jnp.dot(y_ref[...], w_ref[0:D_SSM, :], preferred_element_type=F32)
    acc = acc + jnp.dot(o_ref[...], w_ref[D_SSM:D_SSM + D_ATT, :], preferred_element_type=F32)
    out_ref[...] = h_ref[...] + acc


def _out_proj(y, o, w, layer, h):
    t, d = h.shape
    tm = _row_tile(t, RES_ROWS)
    return pl.pallas_call(
        _out_proj_kernel,
        grid=(t // tm,),
        in_specs=[
            pl.BlockSpec((tm, D_SSM), lambda i: (i, 0)),
            pl.BlockSpec((tm, D_ATT), lambda i: (i, 0)),
            pl.BlockSpec((None, D_SSM + D_ATT, d), lambda i: (layer, 0, 0), pipeline_mode=pl.Buffered(1)),
            pl.BlockSpec((tm, d), lambda i: (i, 0)),
        ],
        out_specs=pl.BlockSpec((tm, d), lambda i: (i, 0)),
        out_shape=jax.ShapeDtypeStruct((t, d), F32),
        compiler_params=_cparams(("parallel",)),
        name="out_proj",
    )(y, o, w, h)


def _norm_proj_resident_kernel(x_ref, g_ref, w_ref, o_ref, oxn_ref):
    xn = _rms(x_ref[...], g_ref[...]).astype(BF16)
    oxn_ref[...] = xn
    o_ref[...] = jnp.dot(xn, w_ref[...], preferred_element_type=F32).astype(o_ref.dtype)


def _norm_proj_resident(x, g, w, layer, out_dtype):
    t, d = x.shape
    n = w.shape[2]
    tm = _row_tile(t, RES_ROWS)
    return pl.pallas_call(
        _norm_proj_resident_kernel,
        grid=(t // tm,),
        in_specs=[
            pl.BlockSpec((tm, d), lambda i: (i, 0)),
            pl.BlockSpec((1, d), lambda i: (0, 0)),
            pl.BlockSpec((None, d, n), lambda i: (layer, 0, 0), pipeline_mode=pl.Buffered(1)),
        ],
        out_specs=[pl.BlockSpec((tm, n), lambda i: (i, 0)), pl.BlockSpec((tm, d), lambda i: (i, 0))],
        out_shape=[jax.ShapeDtypeStruct((t, n), out_dtype), jax.ShapeDtypeStruct((t, d), BF16)],
        compiler_params=_cparams(("parallel",)),
        name="norm_proj_resident",
    )(x, g.reshape(1, d), w)


def _top16(s, n_out=None):
    n, t = s.shape
    iota = lax.broadcasted_iota(jnp.int32, (n, t), 0)
    row16 = lax.broadcasted_iota(jnp.int32, (PEER_TOPK, t), 0)
    rank = jnp.full((n, t), PEER_TOPK, jnp.int32)
    vals = jnp.zeros((PEER_TOPK, t), F32)
    cnt = jnp.zeros((n_out, t), jnp.int32) if n_out else None
    for r in range(PEER_TOPK):
        m = jnp.max(s, axis=0, keepdims=True)
        idx = jnp.min(jnp.where(s == m, iota, n), axis=0, keepdims=True)
        sel = iota == idx
        rank = jnp.where(sel, r, rank)
        s = jnp.where(sel, -jnp.inf, s)
        vals = jnp.where(row16 == r, m, vals)
        if n_out:
            cnt = cnt + (row16 == jnp.right_shift(idx, TOPK_SHIFT)).astype(jnp.int32)
    return vals, rank, cnt


_MARK = 2.0 ** 126


def _peel16(s):
    t = s.shape[1]
    row16 = lax.broadcasted_iota(jnp.int32, (PEER_TOPK, t), 0)
    vals = jnp.zeros((PEER_TOPK, t), F32)
    for r in range(PEER_TOPK):
        m = jnp.max(s, axis=0, keepdims=True)
        s = jnp.where(s == m, -(1.0 + (r + 1) / 32.0) * _MARK, s)
        vals = jnp.where(row16 == r, m, vals)
    marked = s < -_MARK
    rank = jnp.where(marked, s * (-32.0 / _MARK) - 33.0, float(PEER_TOPK))
    count = jnp.sum(jnp.where(marked, 1.0, 0.0), axis=0, keepdims=True)
    return vals, rank, count


def _route_outputs(s1, s2, v1, v2, rank1, rank2, cnt, zsum):
    n1 = jnp.zeros(s1.shape, F32)
    for j in range(PEER_TOPK):
        n1 = jnp.where(rank1 == j, cnt[j:j + 1, :], n1)
    e1 = jnp.where(rank1 < PEER_TOPK, jnp.exp(s1 - v1[0:1, :]) / zsum, 0.0)
    e2 = jnp.where(rank2 < PEER_TOPK, jnp.exp(s2 - v2[0:1, :]), 0.0)
    return e1, n1, e2.astype(BF16), rank2.astype(BF16)


def _route_exact(s1, s2):
    v1, rank1, _ = _top16(s1)
    v2, rank2, _ = _top16(s2)
    cand = jnp.concatenate([v1[j:j + 1, :] + v2 for j in range(PEER_TOPK)], axis=0)
    sc, _, cnt = _top16(cand, n_out=PEER_TOPK)
    zsum = jnp.sum(jnp.exp(sc - sc[0:1, :]), axis=0, keepdims=True)
    return _route_outputs(s1, s2, v1, v2, rank1.astype(F32), rank2.astype(F32), cnt.astype(F32), zsum)


def _route_fast(s1, s2):
    half = PEER_TOPK // 2
    t = s1.shape[1]
    v1, rank1, c1 = _peel16(s1)
    v2, rank2, c2 = _peel16(s2)
    pieces = [v1[j:j + 1, :] + v2[0:half, :] for j in range(half)]
    pieces.append(v1[half:, :] + v2[0:1, :])
    pieces.append(v1[0:1, :] + v2[half:, :])
    cand = jnp.concatenate(pieces, axis=0)
    sc, rankc, cc = _peel16(cand)
    sel = rankc < PEER_TOPK
    zsum = jnp.sum(jnp.where(sel, jnp.exp(cand - sc[0:1, :]), 0.0), axis=0, keepdims=True)
    mk = jnp.where(sel, 1.0, 0.0)
    row8 = lax.broadcasted_iota(jnp.int32, (half, t), 0)
    n_lo = jnp.zeros((half, t), F32)
    for j in range(half):
        rs = jnp.sum(mk[j * half:(j + 1) * half, :], axis=0, keepdims=True)
        if j == 0:
            rs = rs + jnp.sum(mk[(half + 1) * half:, :], axis=0, keepdims=True)
        n_lo = jnp.where(row8 == j, rs, n_lo)
    cnt = jnp.concatenate([n_lo, mk[half * half:(half + 1) * half, :]], axis=0)
    bad = jnp.abs(c1 - PEER_TOPK) + jnp.abs(c2 - PEER_TOPK) + jnp.abs(cc - PEER_TOPK)
    return _route_outputs(s1, s2, v1, v2, rank1, rank2, cnt, zsum), bad


def _route_kernel(q_ref, k1_ref, k2_ref, e1_ref, n1_ref, e2_ref, r2_ref, *, tm):
    def store(ls, outs):
        e1_ref[:, ls], n1_ref[:, ls], e2_ref[:, ls], r2_ref[:, ls] = outs

    def scores(ls):
        qb = q_ref[ls, :].astype(BF16)
        s1 = lax.dot_general(k1_ref[...], qb[:, 0:N_KEYS], _NT, preferred_element_type=F32)
        s2 = lax.dot_general(k2_ref[...], qb[:, N_KEYS:2 * N_KEYS], _NT, preferred_element_type=F32)
        return s1, s2

    group = 4
    for g0 in range(0, tm // LANE, group):
        tiles = [slice(lt * LANE, (lt + 1) * LANE) for lt in range(g0, min(g0 + group, tm // LANE))]
        flags = []
        for ls in tiles:
            outs, bad = _route_fast(*scores(ls))
            store(ls, outs)
            flags.append(jnp.max(bad) > 0.0)
        for ls, flag in zip(tiles, flags):
            @pl.when(flag)
            def _(ls=ls):
                store(ls, _route_exact(*scores(ls)))


def _route(q, k1, k2, layer, tm=TM):
    t = q.shape[0]
    kd = k1.shape[-1]
    out = lambda: pl.BlockSpec((None, N_KEYS, tm), lambda i, h: (h, 0, i))
    key = lambda: pl.BlockSpec((None, None, N_KEYS, kd), lambda i, h: (layer, h, 0, 0))
    return pl.pallas_call(
        functools.partial(_route_kernel, tm=tm),
        grid=(t // tm, PEER_HEADS),
        in_specs=[pl.BlockSpec((tm, 2 * kd), lambda i, h: (i, h)), key(), key()],
        out_specs=[out(), out(), out(), out()],
        out_shape=[jax.ShapeDtypeStruct((PEER_HEADS, N_KEYS, t), F32),
                   jax.ShapeDtypeStruct((PEER_HEADS, N_KEYS, t), F32),
                   jax.ShapeDtypeStruct((PEER_HEADS, N_KEYS, t), BF16),
                   jax.ShapeDtypeStruct((PEER_HEADS, N_KEYS, t), BF16)],
        compiler_params=_cparams(("parallel", "parallel")),
        name="peer_route",
    )(q, k1, k2)


def _peer_kernel(x_ref, u_ref, vt_ref, e1_ref, n1_ref, e2_ref, r2_ref, out_ref, *, te):
    @pl.when(pl.program_id(1) == 0)
    def _():
        out_ref[...] = jnp.zeros_like(out_ref)

    at = lax.dot_general(u_ref[...], x_ref[...], _NT, preferred_element_type=F32)
    act = (0.5 * at * (1.0 + lax.erf(at * (2.0 ** -0.5)))).astype(BF16)
    zero = jnp.zeros((), BF16)
    parts = []
    for c in range(te // N_KEYS):
        w = None
        for h in range(PEER_HEADS):
            n1 = n1_ref[h, c:c + 1, :].astype(BF16)
            e1 = e1_ref[h, c:c + 1, :].astype(BF16)
            wh = jnp.where(r2_ref[h] < n1, e1 * e2_ref[h], zero)
            w = wh if w is None else w + wh
        parts.append(act[c * N_KEYS:(c + 1) * N_KEYS, :] * w)
    ht = jnp.concatenate(parts, axis=0)
    out_ref[...] += jnp.dot(vt_ref[...], ht, preferred_element_type=F32)


def _peer_call(x, u, vt, layer, e1, n1, e2, r2, tm, first, count, te=1024):
    t, d = x.shape
    ne = u.shape[1]
    rows = te // N_KEYS
    once = pl.Buffered(1)
    in_specs = [
        pl.BlockSpec((tm, d), lambda i, j: (first + i, 0), pipeline_mode=once),
        pl.BlockSpec((None, te, d), lambda i, j: (layer, j, 0)),
        pl.BlockSpec((None, d, te), lambda i, j: (layer, 0, j)),
        pl.BlockSpec((PEER_HEADS, rows, tm), lambda i, j: (0, j, first + i)),
        pl.BlockSpec((PEER_HEADS, rows, tm), lambda i, j: (0, j, first + i)),
        pl.BlockSpec((PEER_HEADS, N_KEYS, tm), lambda i, j: (0, 0, first + i), pipeline_mode=once),
        pl.BlockSpec((PEER_HEADS, N_KEYS, tm), lambda i, j: (0, 0, first + i), pipeline_mode=once),
    ]
    return pl.pallas_call(
        functools.partial(_peer_kernel, te=te),
        grid=(count, ne // te),
        in_specs=in_specs,
        out_specs=pl.BlockSpec((d, tm), lambda i, j: (0, i)),
        out_shape=jax.ShapeDtypeStruct((d, count * tm), F32),
        compiler_params=_cparams(("parallel", "arbitrary")),
        name="peer_experts",
    )(x, u, vt, e1, n1, e2, r2)


def _peer(x, u, vt, layer, e1, n1, e2, r2):
    t = x.shape[0]
    n_wide = t // PEER_TM
    per = PEER_TM // TM
    parts = []
    if n_wide:
        parts.append((_peer_call(x, u, vt, layer, e1, n1, e2, r2, PEER_TM, 0, n_wide), 0, n_wide * per))
    n_rest = t // TM - n_wide * per
    if n_rest:
        parts.append((_peer_call(x, u, vt, layer, e1, n1, e2, r2, TM, n_wide * per, n_rest), n_wide * per, n_rest))
    return parts


def _ple_kernel(h_ref, g_ref, p_ref, wg_ref, wp_ref, *rest, firsts):
    out_ref = rest[-1]
    i = pl.program_id(0)
    dt = rest[0][...]
    for part_ref, first in zip(rest[1:-1], firsts[1:]):
        dt = jnp.where(i >= first, part_ref[...], dt)
    h2 = h_ref[...] + dt.T
    xn = _rms(h2, g_ref[...]).astype(BF16)
    gate = jax.nn.sigmoid(jnp.dot(xn, wg_ref[...], preferred_element_type=F32))
    emb = jnp.dot(p_ref[...].astype(BF16), wp_ref[...], preferred_element_type=F32)
    out_ref[...] = h2 + emb * gate


def _ple(h, delta_parts, g, p, wg, wp, layer):
    t, d = h.shape
    pd = p.shape[2]
    tm = TM
    once = pl.Buffered(1)
    part_spec = lambda first, n: pl.BlockSpec((d, tm), lambda i: (0, jnp.clip(i - first, 0, n - 1)))
    return pl.pallas_call(
        functools.partial(_ple_kernel, firsts=tuple(first for _, first, _ in delta_parts)),
        grid=(t // tm,),
        in_specs=[
            pl.BlockSpec((tm, d), lambda i: (i, 0)),
            pl.BlockSpec((1, d), lambda i: (0, 0)),
            pl.BlockSpec((None, tm, pd), lambda i: (layer, i, 0)),
            pl.BlockSpec((None, d, d), lambda i: (layer, 0, 0), pipeline_mode=once),
            pl.BlockSpec((None, pd, d), lambda i: (layer, 0, 0), pipeline_mode=once),
        ] + [part_spec(first, n) for _, first, n in delta_parts],
        out_specs=pl.BlockSpec((tm, d), lambda i: (i, 0)),
        out_shape=jax.ShapeDtypeStruct((t, d), F32),
        compiler_params=_cparams(("parallel",)),
        name="ple_gate",
    )(h, g.reshape(1, d), p, wg, wp, *[a for a, _, _ in delta_parts])


def _lambda_init(layer_idx):
    return 0.8 - 0.6 * math.exp(-0.3 * layer_idx)


def _pad_rows(x, rows):
    return jnp.pad(x, ((0, rows - x.shape[0]),) + ((0, 0),) * (x.ndim - 1))


def kernel(x_prompt, x_sample, cache_k, cache_v, state_ssm, state_conv, p_prompt, p_sample, g_mix, w_in,
           conv_w, conv_b, dt_bias, a_log, d_skip, g_ssd, q_gain, k_gain, lam_q1, lam_k1, lam_q2, lam_k2,
           g_sub, w_out, g_ffn, peer_wq, peer_k1, peer_k2, peer_u, peer_v, g_ple, w_ple, w_pgate):
    nbp, seq, d = x_prompt.shape
    nbs, dseq, _ = x_sample.shape
    depth = w_in.shape[0]
    past = cache_k.shape[2]
    n_p = nbp * seq
    n_s = nbs * dseq
    t_pad = -(-(n_p + n_s) // TM) * TM

    xp = x_prompt.reshape(n_p, d)
    h = jnp.concatenate([xp, x_sample.reshape(n_s, d), xp[:t_pad - n_p - n_s]], axis=0)
    ple_dim = p_prompt.shape[-1]
    p_all = jnp.concatenate([p_prompt.reshape(depth, n_p, ple_dim), p_sample.reshape(depth, n_s, ple_dim)], axis=1)
    p_all = jnp.pad(p_all, ((0, 0), (0, t_pad - n_p - n_s), (0, 0)))
    ck = jnp.transpose(cache_k, (0, 1, 3, 4, 5, 2))
    cv = cache_v.reshape(depth * nbs * past * H_A, DH_V)
    expand = jnp.tile((jnp.arange(LANE)[:, None] == jnp.arange(D_SSM)[None, :] // SSM_P).astype(BF16), (3, 1))
    prev_zero = jnp.zeros((nbp, 8, CONV_DIM), F32)
    h0_zero = jnp.zeros((nbp, D_SSM, SSM_N), F32)

    w_main = w_in.astype(BF16)
    w_dt = jnp.pad(w_in[:, :, MAIN_DIM:], ((0, 0), (0, 0), (0, LANE - H_S))).astype(BF16)
    w_out_b = w_out.astype(BF16)
    wq_b = peer_wq.astype(BF16)
    k1_b = peer_k1.astype(BF16)
    k2_b = peer_k2.astype(BF16)
    u_b = peer_u.astype(BF16)
    vt_b = jnp.swapaxes(peer_v, 1, 2).astype(BF16)
    wg_b = w_pgate.astype(BF16)
    wp_b = w_ple.astype(BF16)

    vp_all = jnp.zeros((depth, n_p, D_ATT), F32)
    vs_all = jnp.zeros((depth, n_s, D_ATT), F32)

    outs = [[] for _ in range(6)]
    for l in range(depth):
        lam0 = _lambda_init(l)
        lam = (jnp.exp(jnp.sum(lam_q1[l] * lam_k1[l])) - jnp.exp(jnp.sum(lam_q2[l] * lam_k2[l])) + lam0)
        lam_s = jnp.stack([lam, jnp.asarray(1.0 - lam0, F32)]).astype(F32)
        lw = dict(
            conv_w=conv_w[l], conv_b=conv_b[l].reshape(1, CONV_DIM),
            dt_bias=_pad_rows(dt_bias[l], LANE).reshape(1, LANE),
            a_neg=_pad_rows(-jnp.exp(a_log[l]), LANE).reshape(1, LANE),
            d_exp=jnp.repeat(d_skip[l], SSM_P).reshape(1, D_SSM),
            g_ssd=g_ssd[l].reshape(1, D_SSM), expand=expand)

        proj, dtp = _in_proj(h, g_mix[l], w_main, w_dt, l, MAIN_DIM)
        qn, kb, vb, kf, vp_all, vs_all = _qk_norm(proj, q_gain[l], k_gain[l], l, vp_all, vs_all)
        o_p = _attn_prompt(lam_s, qn, kb, vb, g_sub[l], nbp, seq)
        o_s = _attn_sample(lam_s, qn, kb, vb, ck, cv, l, g_sub[l], n_p, nbs, dseq)
        y_p, hT_p = _ssd(proj, dtp, prev_zero, h0_zero, lw, nbp, seq, SSD_Q)
        proj_s = proj[n_p:n_p + n_s].reshape(nbs, dseq, MAIN_DIM)
        pad_seq = lambda a: jnp.pad(a, ((0, 0), (0, SSD_Q - dseq), (0, 0))).reshape(nbs * SSD_Q, a.shape[-1])
        prev_s = jnp.pad(state_conv[l], ((0, 0), (8 - (CONV_W - 1), 0), (0, 0)))
        y_s, hT_s = _ssd(pad_seq(proj_s), pad_seq(dtp[n_p:n_p + n_s].reshape(nbs, dseq, LANE)), prev_s,
                         state_ssm[l].reshape(nbs, D_SSM, SSM_N), lw, nbs, SSD_Q, dseq)
        y_s = y_s.reshape(nbs, SSD_Q, D_SSM)[:, :dseq].reshape(n_s, D_SSM)
        y_all = _pad_rows(jnp.concatenate([y_p, y_s], axis=0), t_pad)
        o_all = _pad_rows(jnp.concatenate([o_p, o_s], axis=0), t_pad)
        h = _out_proj(y_all, o_all, w_out_b, l, h)

        q_peer, c = _norm_proj_resident(h, g_ffn[l], wq_b, l, BF16)
        e1, n1, e2, r2 = _route(q_peer, k1_b, k2_b, l)
        delta_t = _peer(c, u_b, vt_b, l, e1, n1, e2, r2)

        h = _ple(h, delta_t, g_ple[l], p_all, wg_b, wp_b, l)

        xbc0 = 2 * D_ATT + D_ATT + D_SSM
        tail = CONV_W - 1
        outs[0].append(hT_p.reshape(nbp, H_S, SSM_P, SSM_N))
        outs[1].append(jnp.stack([proj[(b + 1) * seq - tail:(b + 1) * seq, xbc0:] for b in range(nbp)]))
        outs[2].append(hT_s.reshape(nbs, H_S, SSM_P, SSM_N))
        outs[3].append(proj_s[:, dseq - tail:, xbc0:])
        outs[4].append(kf[:n_p].reshape(nbp, seq, H_A, 2, DH_QK))
        outs[5].append(kf[n_p:n_p + n_s].reshape(nbs, dseq, H_A, 2, DH_QK))

    y_prompt = h[:n_p].reshape(nbp, seq, d)
    y_sample = h[n_p:n_p + n_s].reshape(nbs, dseq, d)
    ssm_p, conv_p, ssm_s, conv_s, k_p, k_s = (jnp.stack(o) for o in outs)
    return (y_prompt, y_sample,
            k_p, vp_all.reshape(depth, nbp, seq, H_A, DH_V), ssm_p, conv_p,
            k_s, vs_all.reshape(depth, nbs, dseq, H_A, DH_V), ssm_s, conv_s)
```

```python
import functools
import math

import jax
import jax.numpy as jnp
from jax import lax
from jax.experimental import pallas as pl
from jax.experimental.pallas import tpu as pltpu

F32 = jnp.float32
BF16 = jnp.bfloat16

EPS = 1e-6
CHUNK = 64
CHUNK_SHIFT = 6
H_A = 8
DH_QK = 64
DH_V = 128
D_ATT = H_A * DH_V
ATT_SCALE = DH_QK ** -0.5
SSM_P = 64
H_S = 16
SSM_N = 128
SSM_G = 2
D_SSM = H_S * SSM_P
CONV_W = 4
CONV_DIM = D_SSM + 2 * SSM_G * SSM_N
MAIN_DIM = 2 * D_ATT + D_ATT + D_SSM + CONV_DIM
N_KEYS = 128
PEER_HEADS = 8
PEER_TOPK = 16
TOPK_SHIFT = 4

LANE = 128
TM = 512
PEER_TM = 1024
RES_ROWS = 576
SSD_Q = 128
VMEM_LIMIT = 56 * 1024 * 1024

_NT = (((1,), (1,)), ((), ()))


def _cparams(sem):
    return pltpu.CompilerParams(dimension_semantics=sem, vmem_limit_bytes=VMEM_LIMIT)


def _row_tile(t, cap):
    for k in range(1, t // 16 + 1):
        if t % k == 0 and (t // k) % 16 == 0 and t // k <= cap:
            return t // k
    return TM


def _rms(x, g):
    ms = jnp.mean(x * x, axis=-1, keepdims=True)
    return x * lax.rsqrt(ms + EPS) * g


def _in_proj_kernel(x_ref, g_ref, w_ref, wdt_ref, o_ref, odt_ref):
    xn = _rms(x_ref[...], g_ref[...]).astype(BF16)
    o_ref[...] = jnp.dot(xn, w_ref[...], preferred_element_type=F32)

    @pl.when(pl.program_id(0) == 0)
    def _():
        odt_ref[...] = jnp.dot(xn, wdt_ref[...], preferred_element_type=F32)


def _in_proj(x, g, w, wdt, layer, n, halves=2):
    t, d = x.shape
    tn = n // halves
    tm = _row_tile(t, RES_ROWS)
    once = pl.Buffered(1)
    return pl.pallas_call(
        _in_proj_kernel,
        grid=(halves, t // tm),
        in_specs=[
            pl.BlockSpec((tm, d), lambda j, i: (i, 0)),
            pl.BlockSpec((1, d), lambda j, i: (0, 0)),
            pl.BlockSpec((None, d, tn), lambda j, i: (layer, 0, j), pipeline_mode=once),
            pl.BlockSpec((None, d, LANE), lambda j, i: (layer, 0, 0), pipeline_mode=once),
        ],
        out_specs=[pl.BlockSpec((tm, tn), lambda j, i: (i, j)),
                   pl.BlockSpec((tm, LANE), lambda j, i: (jnp.where(j == 0, i, t // tm - 1), 0))],
        out_shape=[jax.ShapeDtypeStruct((t, n), F32), jax.ShapeDtypeStruct((t, LANE), F32)],
        compiler_params=_cparams(("arbitrary", "arbitrary")),
        name="in_proj",
    )(x, g.reshape(1, d), w, wdt)


def _qk_norm_kernel(q_ref, k_ref, v_ref, qg_ref, kg_ref, vp_in, vs_in,
                    qn_ref, kb_ref, vb_ref, kf_ref, vp_ref, vs_ref, *, np_tiles, n_s):
    del vp_in, vs_in
    i = pl.program_id(0)
    lane = lax.broadcasted_iota(jnp.int32, (1, DH_V), 1)
    lo = lane < DH_QK

    def norm(x, g):
        xx = x * x
        s_lo = jnp.sum(jnp.where(lo, xx, 0.0), axis=-1, keepdims=True)
        s_all = jnp.sum(xx, axis=-1, keepdims=True)
        ms = jnp.where(lo, s_lo, s_all - s_lo) * (1.0 / DH_QK)
        return x * lax.rsqrt(ms + EPS) * g

    for h in range(H_A):
        sl = slice(h * DH_V, (h + 1) * DH_V)
        qn_ref[:, sl] = (norm(q_ref[:, sl], qg_ref[...]) * ATT_SCALE).astype(BF16)
        kn = norm(k_ref[:, sl], kg_ref[...])
        kf_ref[:, sl] = kn
        kb_ref[:, sl] = kn.astype(BF16)
    vb_ref[...] = v_ref[...].astype(BF16)

    @pl.when(i < np_tiles)
    def _():
        vp_ref[...] = v_ref[...]

    @pl.when(i == np_tiles)
    def _():
        vs_ref[...] = v_ref[0:n_s, :]


def _qk_norm(proj, q_gain, k_gain, layer, vp, vs):
    t = proj.shape[0]
    n_p, n_s = vp.shape[1], vs.shape[1]
    assert n_p % TM == 0 and n_s <= TM and t >= n_p + TM
    np_tiles = n_p // TM
    blk = lambda c: pl.BlockSpec((TM, D_ATT), lambda i, c=c: (i, c))
    gspec = pl.BlockSpec((1, DH_V), lambda i: (0, 0))
    hbm = pl.BlockSpec(memory_space=pl.ANY)
    p_out = pl.BlockSpec((None, TM, D_ATT), lambda i: (layer, jnp.minimum(i, np_tiles - 1), 0))
    s_out = pl.BlockSpec((None, n_s, D_ATT), lambda i: (layer, 0, 0))
    return pl.pallas_call(
        functools.partial(_qk_norm_kernel, np_tiles=np_tiles, n_s=n_s),
        grid=(t // TM,),
        in_specs=[blk(0), blk(1), blk(2), gspec, gspec, hbm, hbm],
        out_specs=[blk(0), blk(0), blk(0), blk(0), p_out, s_out],
        out_shape=[
            jax.ShapeDtypeStruct((t, D_ATT), BF16),
            jax.ShapeDtypeStruct((t, D_ATT), BF16),
            jax.ShapeDtypeStruct((t, D_ATT), BF16),
            jax.ShapeDtypeStruct((t, D_ATT), F32),
            jax.ShapeDtypeStruct(vp.shape, F32),
            jax.ShapeDtypeStruct(vs.shape, F32),
        ],
        input_output_aliases={5: 4, 6: 5},
        compiler_params=_cparams(("arbitrary",)),
        name="qk_norm",
    )(proj, proj, proj, q_gain.reshape(1, DH_V), k_gain.reshape(1, DH_V), vp, vs)


def _sub_norm(o, lam_ref, gsub):
    return _rms(o, gsub) * lam_ref[1]


def _attn_prompt_kernel(lam_ref, q_ref, k_ref, v_ref, gsub_ref, o_ref, *, tq, tk):
    qi = pl.program_id(2)
    q = q_ref[...]
    lane = lax.broadcasted_iota(jnp.int32, (1, DH_V), 1)
    zero = jnp.zeros((), BF16)
    q0 = jnp.where(lane < DH_QK, q, zero)
    q1 = jnp.where(lane >= DH_QK, q, zero)

    def block(start, mask):
        kb = k_ref[pl.ds(start, tk), :]
        s0 = lax.dot_general(q0, kb, _NT, preferred_element_type=F32)
        s1 = lax.dot_general(q1, kb, _NT, preferred_element_type=F32)
        if mask is not None:
            s0 = jnp.where(mask, s0, -jnp.inf)
            s1 = jnp.where(mask, s1, -jnp.inf)
        return s0, s1, v_ref[pl.ds(start, tk), :]

    def upd(s, vb, m, l, a):
        mn = jnp.maximum(m, jnp.max(s, axis=-1, keepdims=True))
        alpha = jnp.exp(m - mn)
        p = jnp.exp(s - mn)
        l = alpha * l + jnp.sum(p, axis=-1, keepdims=True)
        a = alpha * a + jnp.dot(p.astype(BF16), vb, preferred_element_type=F32)
        return mn, l, a

    row = jnp.right_shift(lax.broadcasted_iota(jnp.int32, (tq, tk), 0), CHUNK_SHIFT)
    col = jnp.right_shift(lax.broadcasted_iota(jnp.int32, (tq, tk), 1), CHUNK_SHIFT)
    base = pl.multiple_of(qi * tq, tq)
    s0, s1, vb = block(base, col <= row)
    m0 = jnp.max(s0, axis=-1, keepdims=True)
    m1 = jnp.max(s1, axis=-1, keepdims=True)
    p0 = jnp.exp(s0 - m0)
    p1 = jnp.exp(s1 - m1)
    l0 = jnp.sum(p0, axis=-1, keepdims=True)
    l1 = jnp.sum(p1, axis=-1, keepdims=True)
    a0 = jnp.dot(p0.astype(BF16), vb, preferred_element_type=F32)
    a1 = jnp.dot(p1.astype(BF16), vb, preferred_element_type=F32)
    for dblk in range(1, tq // tk):
        s0, s1, vb = block(base + dblk * tk, col + dblk * (tk // CHUNK) <= row)
        m0, l0, a0 = upd(s0, vb, m0, l0, a0)
        m1, l1, a1 = upd(s1, vb, m1, l1, a1)

    def body(j, carry):
        m0, l0, a0, m1, l1, a1 = carry
        s0, s1, vb = block(pl.multiple_of(j * tk, tk), None)
        m0, l0, a0 = upd(s0, vb, m0, l0, a0)
        m1, l1, a1 = upd(s1, vb, m1, l1, a1)
        return m0, l0, a0, m1, l1, a1

    m0, l0, a0, m1, l1, a1 = lax.fori_loop(0, qi * (tq // tk), body, (m0, l0, a0, m1, l1, a1))
    o = a0 / l0 - lam_ref[0] * (a1 / l1)
    o_ref[...] = _sub_norm(o, lam_ref, gsub_ref[...]).astype(BF16)


def _attn_prompt(lam, qn, kb, vb, g_sub, nb, seq, tq=512, tk=512):
    tq = min(tq, seq)
    tk = min(tk, tq)
    nq = seq // tq
    return pl.pallas_call(
        functools.partial(_attn_prompt_kernel, tq=tq, tk=tk),
        grid=(nb, H_A, nq),
        in_specs=[
            pl.BlockSpec(memory_space=pltpu.SMEM),
            pl.BlockSpec((tq, DH_V), lambda b, h, i: (b * nq + i, h)),
            pl.BlockSpec((seq, DH_V), lambda b, h, i: (b, h)),
            pl.BlockSpec((seq, DH_V), lambda b, h, i: (b, h)),
            pl.BlockSpec((1, DH_V), lambda b, h, i: (0, 0)),
        ],
        out_specs=pl.BlockSpec((tq, DH_V), lambda b, h, i: (b * nq + i, h)),
        out_shape=jax.ShapeDtypeStruct((nb * seq, D_ATT), BF16),
        compiler_params=_cparams(("parallel", "parallel", "arbitrary")),
        name="attn_prompt",
    )(lam, qn, kb, vb, g_sub.reshape(1, DH_V))


def _attn_sample_kernel(lam_ref, q_ref, kn_ref, vn_ref, ckt_ref, cv_ref, gsub_ref, o_ref, m_ref, l_ref, acc_ref, *, tk):
    c = pl.program_id(1)
    nq = q_ref.shape[0]
    lane = lax.broadcasted_iota(jnp.int32, (1, DH_V), 1)
    zero = jnp.zeros((), BF16)

    @pl.when(c == 0)
    def _():
        m_ref[...] = jnp.full(m_ref.shape, -jnp.inf, F32)
        l_ref[...] = jnp.zeros(l_ref.shape, F32)
        acc_ref[...] = jnp.zeros(acc_ref.shape, F32)

    def q01(h):
        q = q_ref[:, h * DH_V:(h + 1) * DH_V]
        return jnp.concatenate([jnp.where(lane < DH_QK, q, zero), jnp.where(lane >= DH_QK, q, zero)], axis=0)

    def update(h, s, v):
        m_old = m_ref[h]
        m_new = jnp.maximum(m_old, jnp.max(s, axis=-1, keepdims=True))
        alpha = jnp.exp(m_old - m_new)
        p = jnp.exp(s - m_new[:, 0:1])
        l_ref[h] = alpha * l_ref[h] + jnp.sum(p, axis=-1, keepdims=True)
        acc_ref[h] = alpha * acc_ref[h] + jnp.dot(p.astype(BF16), v, preferred_element_type=F32)
        m_ref[h] = m_new

    for h in range(H_A):
        kt = ckt_ref[h].reshape(2 * DH_QK, tk).astype(BF16)
        s = jnp.dot(q01(h), kt, preferred_element_type=F32)
        update(h, s, cv_ref[pl.ds(h, tk, stride=H_A), :].astype(BF16))

    @pl.when(c == pl.num_programs(1) - 1)
    def _():
        for h in range(H_A):
            sl = slice(h * DH_V, (h + 1) * DH_V)
            s = lax.dot_general(q01(h), kn_ref[:, sl], _NT, preferred_element_type=F32)
            update(h, s, vn_ref[:, sl])
            a = acc_ref[h] / l_ref[h]
            o = a[0:nq] - lam_ref[0] * a[nq:2 * nq]
            o_ref[:, sl] = _sub_norm(o, lam_ref, gsub_ref[...]).astype(BF16)


def _attn_sample(lam, qn, kb, vb, ckt, cv2, layer, g_sub, row0, nb, nq, tk=1024):
    past = ckt.shape[-1]
    tk = min(tk, past)
    nc = past // tk
    r0 = row0 // nq
    new = lambda: pl.BlockSpec((nq, D_ATT), lambda b, c: (r0 + b, 0))
    return pl.pallas_call(
        functools.partial(_attn_sample_kernel, tk=tk),
        grid=(nb, nc),
        in_specs=[pl.BlockSpec(memory_space=pltpu.SMEM), new(), new(), new(),
                  pl.BlockSpec((None, None, H_A, 2, DH_QK, tk), lambda b, c: (layer, b, 0, 0, 0, c)),
                  pl.BlockSpec((tk * H_A, DH_V), lambda b, c: ((layer * nb + b) * nc + c, 0)),
                  pl.BlockSpec((1, DH_V), lambda b, c: (0, 0))],
        out_specs=pl.BlockSpec((nq, D_ATT), lambda b, c: (b, 0)),
        out_shape=jax.ShapeDtypeStruct((nb * nq, D_ATT), BF16),
        scratch_shapes=[pltpu.VMEM((H_A, 2 * nq, DH_V), F32), pltpu.VMEM((H_A, 2 * nq, DH_V), F32),
                        pltpu.VMEM((H_A, 2 * nq, DH_V), F32)],
        compiler_params=_cparams(("parallel", "arbitrary")),
        name="attn_sample",
    )(lam, qn, kb, vb, ckt, cv2, g_sub.reshape(1, DH_V))


def _split3(x, axis):
    hi = x.astype(BF16)
    r = x - hi.astype(F32)
    mid = r.astype(BF16)
    lo = (r - mid.astype(F32)).astype(BF16)
    return jnp.concatenate([hi, mid, lo], axis=axis)


def _ssd_kernel(xs_ref, b_ref, c_ref, z_ref, dt_ref, prev_ref, h0_ref, cw_ref, cb_ref, dtb_ref,
                aneg_ref, dexp_ref, gssd_ref, expand_ref, y_ref, hout_ref, ext_ref, ht_ref, *, valid):
    q = SSD_Q
    c = pl.program_id(1)
    halo = 8

    @pl.when(c == 0)
    def _():
        ext_ref[0:halo, :] = prev_ref[...]
        ht_ref[...] = h0_ref[...].T

    ext_ref[halo:halo + q, 0:D_SSM] = xs_ref[...]
    ext_ref[halo:halo + q, D_SSM:D_SSM + SSM_G * SSM_N] = b_ref[...]
    ext_ref[halo:halo + q, D_SSM + SSM_G * SSM_N:CONV_DIM] = c_ref[...]
    conv = cb_ref[...]
    for j in range(CONV_W):
        r0 = halo - (CONV_W - 1) + j
        conv = conv + ext_ref[r0:r0 + q, :] * cw_ref[j:j + 1, :]
    tail = ext_ref[q:q + halo, :]
    ext_ref[0:halo, :] = tail
    xbc = conv * jax.nn.sigmoid(conv)
    xs = xbc[:, 0:D_SSM]
    bm = xbc[:, D_SSM:D_SSM + SSM_G * SSM_N]
    cm = xbc[:, D_SSM + SSM_G * SSM_N:CONV_DIM]

    dt = jax.nn.softplus(dt_ref[...] + dtb_ref[...])
    if valid < q:
        rowv = lax.broadcasted_iota(jnp.int32, (q, LANE), 0) < valid
        dt = jnp.where(rowv, dt, 0.0)
    a = dt * aneg_ref[...]
    r_i = lax.broadcasted_iota(jnp.int32, (q, q), 0)
    c_i = lax.broadcasted_iota(jnp.int32, (q, q), 1)
    tril = c_i <= r_i
    tril3 = jnp.tile(tril.astype(BF16), (1, 3))
    acum = jnp.dot(tril3, _split3(a, 0), preferred_element_type=F32)
    eye3 = jnp.tile((lax.broadcasted_iota(jnp.int32, (LANE, LANE), 0)
                     == lax.broadcasted_iota(jnp.int32, (LANE, LANE), 1)).astype(BF16), (1, 3))
    acum_t = lax.dot_general(eye3, _split3(acum, 1), _NT, preferred_element_type=F32)
    a_last = acum[q - 1:q, :]
    decay_in = jnp.exp(a_last - acum)
    stack = jnp.concatenate(
        [dt, dt * decay_in, jnp.exp(acum), jnp.broadcast_to(jnp.exp(a_last), (8, LANE))], axis=0)
    ex = jnp.dot(_split3(stack, 1), expand_ref[...], preferred_element_type=F32)
    dt_e = ex[0:q]
    dtd_e = ex[q:2 * q]
    eacum_e = ex[2 * q:3 * q]
    cd_e = ex[3 * q:3 * q + 1]

    xdt = (xs * dt_e).astype(BF16)
    xdtd = (xs * dtd_e).astype(BF16)
    lane = lax.broadcasted_iota(jnp.int32, (1, LANE), 1)
    zero = jnp.zeros((), BF16)
    gw = D_SSM // SSM_G
    hpg = H_S // SSM_G
    y_diag_parts = []
    y_off_parts = []
    for g in range(SSM_G):
        gs = slice(g * gw, (g + 1) * gw)
        bg = bm[:, g * SSM_N:(g + 1) * SSM_N]
        cg = cm[:, g * SSM_N:(g + 1) * SSM_N].astype(BF16)
        cb = lax.dot_general(cg, bg.astype(BF16), _NT, preferred_element_type=F32)
        ht_prev = ht_ref[:, gs]
        y_off = jnp.dot(cg, ht_prev.astype(BF16), preferred_element_type=F32)
        ht_ref[:, gs] = cd_e[:, gs] * ht_prev + jnp.dot(
            bg.T.astype(BF16), xdtd[:, gs], preferred_element_type=F32)
        for k in range(hpg // 2):
            ms = []
            for h in (g * hpg + 2 * k, g * hpg + 2 * k + 1):
                seg = acum[:, h:h + 1] - acum_t[h:h + 1, :]
                ms.append((cb * jnp.exp(jnp.where(tril, seg, -jnp.inf))).astype(BF16))
            pair = slice((g * hpg + 2 * k) * SSM_P, (g * hpg + 2 * k + 2) * SSM_P)
            xp = xdt[:, pair]
            rhs = jnp.concatenate([jnp.where(lane < SSM_P, xp, zero), jnp.where(lane >= SSM_P, xp, zero)], axis=0)
            y_diag_parts.append(jnp.dot(jnp.concatenate(ms, axis=1), rhs, preferred_element_type=F32))
        y_off_parts.append(y_off)
    y_diag = jnp.concatenate(y_diag_parts, axis=1)
    y_off = jnp.concatenate(y_off_parts, axis=1)
    y = y_diag + y_off * eacum_e + dexp_ref[...] * xs
    zz = z_ref[...]
    y = y * (zz * jax.nn.sigmoid(zz))
    for g in range(SSM_G):
        gs = slice(g * gw, (g + 1) * gw)
        y_ref[:, gs] = _rms(y[:, gs], gssd_ref[:, gs]).astype(BF16)

    @pl.when(c == pl.num_programs(1) - 1)
    def _():
        hout_ref[...] = ht_ref[...].T


def _ssd(proj, dtp, prev, h0, lw, nb, seq, valid):
    nc = seq // SSD_Q
    row = lambda w, col: pl.BlockSpec((SSD_Q, w), lambda b, c, col=col: (b * nc + c, col))
    const = lambda shp: pl.BlockSpec(shp, lambda b, c: (0,) * len(shp))
    per_b = lambda shp: pl.BlockSpec((None,) + shp, lambda b, c: (b, 0, 0))
    return pl.pallas_call(
        functools.partial(_ssd_kernel, valid=valid),
        grid=(nb, nc),
        in_specs=[
            row(D_SSM, 4), row(SSM_G * SSM_N, 20), row(SSM_G * SSM_N, 21), row(D_SSM, 3), row(LANE, 0),
            per_b((8, CONV_DIM)), per_b((D_SSM, SSM_N)),
            const((CONV_W, CONV_DIM)), const((1, CONV_DIM)), const((1, LANE)), const((1, LANE)),
            const((1, D_SSM)), const((1, D_SSM)), const((3 * LANE, D_SSM)),
        ],
        out_specs=[row(D_SSM, 0), per_b((D_SSM, SSM_N))],
        out_shape=[jax.ShapeDtypeStruct((nb * seq, D_SSM), BF16),
                   jax.ShapeDtypeStruct((nb, D_SSM, SSM_N), F32)],
        scratch_shapes=[pltpu.VMEM((SSD_Q + 8, CONV_DIM), F32), pltpu.VMEM((SSM_N, D_SSM), F32)],
        compiler_params=_cparams(("parallel", "arbitrary")),
        name="ssd",
    )(proj, proj, proj, proj, dtp, prev, h0, lw["conv_w"], lw["conv_b"], lw["dt_bias"], lw["a_neg"],
      lw["d_exp"], lw["g_ssd"], lw["expand"])


def _out_proj_kernel(y_ref, o_ref, w_ref, h_ref, out_ref):
    acc = jnp.dot(y_ref[...], w_ref[0:D_SSM, :], preferred_element_type=F32)
    acc = acc + jnp.dot(o_ref[...], w_ref[D_SSM:D_SSM + D_ATT, :], preferred_element_type=F32)
    out_ref[...] = h_ref[...] + acc


def _out_proj(y, o, w, layer, h):
    t, d = h.shape
    tm = _row_tile(t, RES_ROWS)
    return pl.pallas_call(
        _out_proj_kernel,
        grid=(t // tm,),
        in_specs=[
            pl.BlockSpec((tm, D_SSM), lambda i: (i, 0)),
            pl.BlockSpec((tm, D_ATT), lambda i: (i, 0)),
            pl.BlockSpec((None, D_SSM + D_ATT, d), lambda i: (layer, 0, 0), pipeline_mode=pl.Buffered(1)),
            pl.BlockSpec((tm, d), lambda i: (i, 0)),
        ],
        out_specs=pl.BlockSpec((tm, d), lambda i: (i, 0)),
        out_shape=jax.ShapeDtypeStruct((t, d), F32),
        compiler_params=_cparams(("parallel",)),
        name="out_proj",
    )(y, o, w, h)


def _norm_proj_resident_kernel(x_ref, g_ref, w_ref, o_ref, oxn_ref):
    xn = _rms(x_ref[...], g_ref[...]).astype(BF16)
    oxn_ref[...] = xn
    o_ref[...] = jnp.dot(xn, w_ref[...], preferred_element_type=F32).astype(o_ref.dtype)


def _norm_proj_resident(x, g, w, layer, out_dtype):
    t, d = x.shape
    n = w.shape[2]
    tm = _row_tile(t, RES_ROWS)
    return pl.pallas_call(
        _norm_proj_resident_kernel,
        grid=(t // tm,),
        in_specs=[
            pl.BlockSpec((tm, d), lambda i: (i, 0)),
            pl.BlockSpec((1, d), lambda i: (0, 0)),
            pl.BlockSpec((None, d, n), lambda i: (layer, 0, 0), pipeline_mode=pl.Buffered(1)),
        ],
        out_specs=[pl.BlockSpec((tm, n), lambda i: (i, 0)), pl.BlockSpec((tm, d), lambda i: (i, 0))],
        out_shape=[jax.ShapeDtypeStruct((t, n), out_dtype), jax.ShapeDtypeStruct((t, d), BF16)],
        compiler_params=_cparams(("parallel",)),
        name="norm_proj_resident",
    )(x, g.reshape(1, d), w)


def _top16(s, n_out=None):
    n, t = s.shape
    iota = lax.broadcasted_iota(jnp.int32, (n, t), 0)
    row16 = lax.broadcasted_iota(jnp.int32, (PEER_TOPK, t), 0)
    rank = jnp.full((n, t), PEER_TOPK, jnp.int32)
    vals = jnp.zeros((PEER_TOPK, t), F32)
    cnt = jnp.zeros((n_out, t), jnp.int32) if n_out else None
    for r in range(PEER_TOPK):
        m = jnp.max(s, axis=0, keepdims=True)
        idx = jnp.min(jnp.where(s == m, iota, n), axis=0, keepdims=True)
        sel = iota == idx
        rank = jnp.where(sel, r, rank)
        s = jnp.where(sel, -jnp.inf, s)
        vals = jnp.where(row16 == r, m, vals)
        if n_out:
            cnt = cnt + (row16 == jnp.right_shift(idx, TOPK_SHIFT)).astype(jnp.int32)
    return vals, rank, cnt


_MARK = 2.0 ** 126


def _peel16(s):
    t = s.shape[1]
    row16 = lax.broadcasted_iota(jnp.int32, (PEER_TOPK, t), 0)
    vals = jnp.zeros((PEER_TOPK, t), F32)
    for r in range(PEER_TOPK):
        m = jnp.max(s, axis=0, keepdims=True)
        s = jnp.where(s == m, -(1.0 + (r + 1) / 32.0) * _MARK, s)
        vals = jnp.where(row16 == r, m, vals)
    marked = s < -_MARK
    rank = jnp.where(marked, s * (-32.0 / _MARK) - 33.0, float(PEER_TOPK))
    count = jnp.sum(jnp.where(marked, 1.0, 0.0), axis=0, keepdims=True)
    return vals, rank, count


def _route_outputs(s1, s2, v1, v2, rank1, rank2, cnt, zsum):
    n1 = jnp.zeros(s1.shape, F32)
    for j in range(PEER_TOPK):
        n1 = jnp.where(rank1 == j, cnt[j:j + 1, :], n1)
    e1 = jnp.where(rank1 < PEER_TOPK, jnp.exp(s1 - v1[0:1, :]) / zsum, 0.0)
    e2 = jnp.where(rank2 < PEER_TOPK, jnp.exp(s2 - v2[0:1, :]), 0.0)
    return e1, n1, e2.astype(BF16), rank2.astype(BF16)


def _route_exact(s1, s2):
    v1, rank1, _ = _top16(s1)
    v2, rank2, _ = _top16(s2)
    cand = jnp.concatenate([v1[j:j + 1, :] + v2 for j in range(PEER_TOPK)], axis=0)
    sc, _, cnt = _top16(cand, n_out=PEER_TOPK)
    zsum = jnp.sum(jnp.exp(sc - sc[0:1, :]), axis=0, keepdims=True)
    return _route_outputs(s1, s2, v1, v2, rank1.astype(F32), rank2.astype(F32), cnt.astype(F32), zsum)


def _route_fast(s1, s2):
    half = PEER_TOPK // 2
    t = s1.shape[1]
    v1, rank1, c1 = _peel16(s1)
    v2, rank2, c2 = _peel16(s2)
    pieces = [v1[j:j + 1, :] + v2[0:half, :] for j in range(half)]
    pieces.append(v1[half:, :] + v2[0:1, :])
    pieces.append(v1[0:1, :] + v2[half:, :])
    cand = jnp.concatenate(pieces, axis=0)
    sc, rankc, cc = _peel16(cand)
    sel = rankc < PEER_TOPK
    zsum = jnp.sum(jnp.where(sel, jnp.exp(cand - sc[0:1, :]), 0.0), axis=0, keepdims=True)
    mk = jnp.where(sel, 1.0, 0.0)
    row8 = lax.broadcasted_iota(jnp.int32, (half, t), 0)
    n_lo = jnp.zeros((half, t), F32)
    for j in range(half):
        rs = jnp.sum(mk[j * half:(j + 1) * half, :], axis=0, keepdims=True)
        if j == 0:
            rs = rs + jnp.sum(mk[(half + 1) * half:, :], axis=0, keepdims=True)
        n_lo = jnp.where(row8 == j, rs, n_lo)
    cnt = jnp.concatenate([n_lo, mk[half * half:(half + 1) * half, :]], axis=0)
    bad = jnp.abs(c1 - PEER_TOPK) + jnp.abs(c2 - PEER_TOPK) + jnp.abs(cc - PEER_TOPK)
    return _route_outputs(s1, s2, v1, v2, rank1, rank2, cnt, zsum), bad


def _route_kernel(q_ref, k1_ref, k2_ref, e1_ref, n1_ref, e2_ref, r2_ref, *, tm):
    def store(ls, outs):
        e1_ref[:, ls], n1_ref[:, ls], e2_ref[:, ls], r2_ref[:, ls] = outs

    def scores(ls):
        qb = q_ref[ls, :].astype(BF16)
        s1 = lax.dot_general(k1_ref[...], qb[:, 0:N_KEYS], _NT, preferred_element_type=F32)
        s2 = lax.dot_general(k2_ref[...], qb[:, N_KEYS:2 * N_KEYS], _NT, preferred_element_type=F32)
        return s1, s2

    group = 4
    for g0 in range(0, tm // LANE, group):
        tiles = [slice(lt * LANE, (lt + 1) * LANE) for lt in range(g0, min(g0 + group, tm // LANE))]
        flags = []
        for ls in tiles:
            outs, bad = _route_fast(*scores(ls))
            store(ls, outs)
            flags.append(jnp.max(bad) > 0.0)
        for ls, flag in zip(tiles, flags):
            @pl.when(flag)
            def _(ls=ls):
                store(ls, _route_exact(*scores(ls)))


def _route(q, k1, k2, layer, tm=TM):
    t = q.shape[0]
    kd = k1.shape[-1]
    out = lambda: pl.BlockSpec((None, N_KEYS, tm), lambda i, h: (h, 0, i))
    key = lambda: pl.BlockSpec((None, None, N_KEYS, kd), lambda i, h: (layer, h, 0, 0))
    return pl.pallas_call(
        functools.partial(_route_kernel, tm=tm),
        grid=(t // tm, PEER_HEADS),
        in_specs=[pl.BlockSpec((tm, 2 * kd), lambda i, h: (i, h)), key(), key()],
        out_specs=[out(), out(), out(), out()],
        out_shape=[jax.ShapeDtypeStruct((PEER_HEADS, N_KEYS, t), F32),
                   jax.ShapeDtypeStruct((PEER_HEADS, N_KEYS, t), F32),
                   jax.ShapeDtypeStruct((PEER_HEADS, N_KEYS, t), BF16),
                   jax.ShapeDtypeStruct((PEER_HEADS, N_KEYS, t), BF16)],
        compiler_params=_cparams(("parallel", "parallel")),
        name="peer_route",
    )(q, k1, k2)


def _peer_kernel(x_ref, u_ref, vt_ref, e1_ref, n1_ref, e2_ref, r2_ref, out_ref, *, te):
    @pl.when(pl.program_id(1) == 0)
    def _():
        out_ref[...] = jnp.zeros_like(out_ref)

    at = lax.dot_general(u_ref[...], x_ref[...], _NT, preferred_element_type=F32)
    act = (0.5 * at * (1.0 + lax.erf(at * (2.0 ** -0.5)))).astype(BF16)
    zero = jnp.zeros((), BF16)
    parts = []
    for c in range(te // N_KEYS):
        w = None
        for h in range(PEER_HEADS):
            n1 = n1_ref[h, c:c + 1, :].astype(BF16)
            e1 = e1_ref[h, c:c + 1, :].astype(BF16)
            wh = jnp.where(r2_ref[h] < n1, e1 * e2_ref[h], zero)
            w = wh if w is None else w + wh
        parts.append(act[c * N_KEYS:(c + 1) * N_KEYS, :] * w)
    ht = jnp.concatenate(parts, axis=0)
    out_ref[...] += jnp.dot(vt_ref[...], ht, preferred_element_type=F32)


def _peer_call(x, u, vt, layer, e1, n1, e2, r2, tm, first, count, te=1024):
    t, d = x.shape
    ne = u.shape[1]
    rows = te // N_KEYS
    once = pl.Buffered(1)
    in_specs = [
        pl.BlockSpec((tm, d), lambda i, j: (first + i, 0), pipeline_mode=once),
        pl.BlockSpec((None, te, d), lambda i, j: (layer, j, 0)),
        pl.BlockSpec((None, d, te), lambda i, j: (layer, 0, j)),
        pl.BlockSpec((PEER_HEADS, rows, tm), lambda i, j: (0, j, first + i)),
        pl.BlockSpec((PEER_HEADS, rows, tm), lambda i, j: (0, j, first + i)),
        pl.BlockSpec((PEER_HEADS, N_KEYS, tm), lambda i, j: (0, 0, first + i), pipeline_mode=once),
        pl.BlockSpec((PEER_HEADS, N_KEYS, tm), lambda i, j: (0, 0, first + i), pipeline_mode=once),
    ]
    return pl.pallas_call(
        functools.partial(_peer_kernel, te=te),
        grid=(count, ne // te),
        in_specs=in_specs,
        out_specs=pl.BlockSpec((d, tm), lambda i, j: (0, i)),
        out_shape=jax.ShapeDtypeStruct((d, count * tm), F32),
        compiler_params=_cparams(("parallel", "arbitrary")),
        name="peer_experts",
    )(x, u, vt, e1, n1, e2, r2)


def _peer(x, u, vt, layer, e1, n1, e2, r2):
    t = x.shape[0]
    n_wide = t // PEER_TM
    per = PEER_TM // TM
    parts = []
    if n_wide:
        parts.append((_peer_call(x, u, vt, layer, e1, n1, e2, r2, PEER_TM, 0, n_wide), 0, n_wide * per))
    n_rest = t // TM - n_wide * per
    if n_rest:
        parts.append((_peer_call(x, u, vt, layer, e1, n1, e2, r2, TM, n_wide * per, n_rest), n_wide * per, n_rest))
    return parts


def _ple_kernel(h_ref, g_ref, p_ref, wg_ref, wp_ref, *rest, firsts):
    out_ref = rest[-1]
    i = pl.program_id(0)
    dt = rest[0][...]
    for part_ref, first in zip(rest[1:-1], firsts[1:]):
        dt = jnp.where(i >= first, part_ref[...], dt)
    h2 = h_ref[...] + dt.T
    xn = _rms(h2, g_ref[...]).astype(BF16)
    gate = jax.nn.sigmoid(jnp.dot(xn, wg_ref[...], preferred_element_type=F32))
    emb = jnp.dot(p_ref[...].astype(BF16), wp_ref[...], preferred_element_type=F32)
    out_ref[...] = h2 + emb * gate


def _ple(h, delta_parts, g, p, wg, wp, layer):
    t, d = h.shape
    pd = p.shape[2]
    tm = TM
    once = pl.Buffered(1)
    part_spec = lambda first, n: pl.BlockSpec((d, tm), lambda i: (0, jnp.clip(i - first, 0, n - 1)))
    return pl.pallas_call(
        functools.partial(_ple_kernel, firsts=tuple(first for _, first, _ in delta_parts)),
        grid=(t // tm,),
        in_specs=[
            pl.BlockSpec((tm, d), lambda i: (i, 0)),
            pl.BlockSpec((1, d), lambda i: (0, 0)),
            pl.BlockSpec((None, tm, pd), lambda i: (layer, i, 0)),
            pl.BlockSpec((None, d, d), lambda i: (layer, 0, 0), pipeline_mode=once),
            pl.BlockSpec((None, pd, d), lambda i: (layer, 0, 0), pipeline_mode=once),
        ] + [part_spec(first, n) for _, first, n in delta_parts],
        out_specs=pl.BlockSpec((tm, d), lambda i: (i, 0)),
        out_shape=jax.ShapeDtypeStruct((t, d), F32),
        compiler_params=_cparams(("parallel",)),
        name="ple_gate",
    )(h, g.reshape(1, d), p, wg, wp, *[a for a, _, _ in delta_parts])


def _lambda_init(layer_idx):
    return 0.8 - 0.6 * math.exp(-0.3 * layer_idx)


def _pad_rows(x, rows):
    return jnp.pad(x, ((0, rows - x.shape[0]),) + ((0, 0),) * (x.ndim - 1))


def kernel(x_prompt, x_sample, cache_k, cache_v, state_ssm, state_conv, p_prompt, p_sample, g_mix, w_in,
           conv_w, conv_b, dt_bias, a_log, d_skip, g_ssd, q_gain, k_gain, lam_q1, lam_k1, lam_q2, lam_k2,
           g_sub, w_out, g_ffn, peer_wq, peer_k1, peer_k2, peer_u, peer_v, g_ple, w_ple, w_pgate):
    nbp, seq, d = x_prompt.shape
    nbs, dseq, _ = x_sample.shape
    depth = w_in.shape[0]
    past = cache_k.shape[2]
    n_p = nbp * seq
    n_s = nbs * dseq
    t_pad = -(-(n_p + n_s) // TM) * TM

    xp = x_prompt.reshape(n_p, d)
    h = jnp.concatenate([xp, x_sample.reshape(n_s, d), xp[:t_pad - n_p - n_s]], axis=0)
    ple_dim = p_prompt.shape[-1]
    p_all = jnp.concatenate([p_prompt.reshape(depth, n_p, ple_dim), p_sample.reshape(depth, n_s, ple_dim)], axis=1)
    p_all = jnp.pad(p_all, ((0, 0), (0, t_pad - n_p - n_s), (0, 0)))
    ck = jnp.transpose(cache_k, (0, 1, 3, 4, 5, 2))
    cv = cache_v.reshape(depth * nbs * past * H_A, DH_V)
    expand = jnp.tile((jnp.arange(LANE)[:, None] == jnp.arange(D_SSM)[None, :] // SSM_P).astype(BF16), (3, 1))
    prev_zero = jnp.zeros((nbp, 8, CONV_DIM), F32)
    h0_zero = jnp.zeros((nbp, D_SSM, SSM_N), F32)

    w_main = w_in.astype(BF16)
    w_dt = jnp.pad(w_in[:, :, MAIN_DIM:], ((0, 0), (0, 0), (0, LANE - H_S))).astype(BF16)
    w_out_b = w_out.astype(BF16)
    wq_b = peer_wq.astype(BF16)
    k1_b = peer_k1.astype(BF16)
    k2_b = peer_k2.astype(BF16)
    u_b = peer_u.astype(BF16)
    vt_b = jnp.swapaxes(peer_v, 1, 2).astype(BF16)
    wg_b = w_pgate.astype(BF16)
    wp_b = w_ple.astype(BF16)

    vp_all = jnp.zeros((depth, n_p, D_ATT), F32)
    vs_all = jnp.zeros((depth, n_s, D_ATT), F32)

    outs = [[] for _ in range(6)]
    for l in range(depth):
        lam0 = _lambda_init(l)
        lam = (jnp.exp(jnp.sum(lam_q1[l] * lam_k1[l])) - jnp.exp(jnp.sum(lam_q2[l] * lam_k2[l])) + lam0)
        lam_s = jnp.stack([lam, jnp.asarray(1.0 - lam0, F32)]).astype(F32)
        lw = dict(
            conv_w=conv_w[l], conv_b=conv_b[l].reshape(1, CONV_DIM),
            dt_bias=_pad_rows(dt_bias[l], LANE).reshape(1, LANE),
            a_neg=_pad_rows(-jnp.exp(a_log[l]), LANE).reshape(1, LANE),
            d_exp=jnp.repeat(d_skip[l], SSM_P).reshape(1, D_SSM),
            g_ssd=g_ssd[l].reshape(1, D_SSM), expand=expand)

        proj, dtp = _in_proj(h, g_mix[l], w_main, w_dt, l, MAIN_DIM)
        qn, kb, vb, kf, vp_all, vs_all = _qk_norm(proj, q_gain[l], k_gain[l], l, vp_all, vs_all)
        o_p = _attn_prompt(lam_s, qn, kb, vb, g_sub[l], nbp, seq)
        o_s = _attn_sample(lam_s, qn, kb, vb, ck, cv, l, g_sub[l], n_p, nbs, dseq)
        y_p, hT_p = _ssd(proj, dtp, prev_zero, h0_zero, lw, nbp, seq, SSD_Q)
        proj_s = proj[n_p:n_p + n_s].reshape(nbs, dseq, MAIN_DIM)
        pad_seq = lambda a: jnp.pad(a, ((0, 0), (0, SSD_Q - dseq), (0, 0))).reshape(nbs * SSD_Q, a.shape[-1])
        prev_s = jnp.pad(state_conv[l], ((0, 0), (8 - (CONV_W - 1), 0), (0, 0)))
        y_s, hT_s = _ssd(pad_seq(proj_s), pad_seq(dtp[n_p:n_p + n_s].reshape(nbs, dseq, LANE)), prev_s,
                         state_ssm[l].reshape(nbs, D_SSM, SSM_N), lw, nbs, SSD_Q, dseq)
        y_s = y_s.reshape(nbs, SSD_Q, D_SSM)[:, :dseq].reshape(n_s, D_SSM)
        y_all = _pad_rows(jnp.concatenate([y_p, y_s], axis=0), t_pad)
        o_all = _pad_rows(jnp.concatenate([o_p, o_s], axis=0), t_pad)
        h = _out_proj(y_all, o_all, w_out_b, l, h)

        q_peer, c = _norm_proj_resident(h, g_ffn[l], wq_b, l, BF16)
        e1, n1, e2, r2 = _route(q_peer, k1_b, k2_b, l)
        delta_t = _peer(c, u_b, vt_b, l, e1, n1, e2, r2)

        h = _ple(h, delta_t, g_ple[l], p_all, wg_b, wp_b, l)

        xbc0 = 2 * D_ATT + D_ATT + D_SSM
        tail = CONV_W - 1
        outs[0].append(hT_p.reshape(nbp, H_S, SSM_P, SSM_N))
        outs[1].append(jnp.stack([proj[(b + 1) * seq - tail:(b + 1) * seq, xbc0:] for b in range(nbp)]))
        outs[2].append(hT_s.reshape(nbs, H_S, SSM_P, SSM_N))
        outs[3].append(proj_s[:, dseq - tail:, xbc0:])
        outs[4].append(kf[:n_p].reshape(nbp, seq, H_A, 2, DH_QK))
        outs[5].append(kf[n_p:n_p + n_s].reshape(nbs, dseq, H_A, 2, DH_QK))

    y_prompt = h[:n_p].reshape(nbp, seq, d)
    y_sample = h[n_p:n_p + n_s].reshape(nbs, dseq, d)
    ssm_p, conv_p, ssm_s, conv_s, k_p, k_s = (jnp.stack(o) for o in outs)
    return (y_prompt, y_sample,
            k_p, vp_all.reshape(depth, nbp, seq, H_A, DH_V), ssm_p, conv_p,
            k_s, vs_all.reshape(depth, nbs, dseq, H_A, DH_V), ssm_s, conv_s)
```
